```python
import jax
import jax.numpy as jnp
from jax import lax
import numpy as np

D_MODEL = 4096
BATCH = 4
SEQ = 4096
DEPTH = 2

BRANCH_WIDTH = D_MODEL // 4
N_BRANCHES = 4
CONV_WIDTH = 4
CHUNK = 64
NORM_EPS = 1e-6

LRU_BLOCKS = 8
LRU_BLOCK = BRANCH_WIDTH // LRU_BLOCKS
LRU_C = 8.0

GLA_HEADS = 4
GLA_DV = BRANCH_WIDTH // GLA_HEADS
GLA_DK = GLA_DV // 2
GLA_RANK = 16
GLA_TAU = 16.0

GDN_HEAD_DIM = 128
GDN_HEADS = BRANCH_WIDTH // GDN_HEAD_DIM

MLSTM_HEADS = 4
MLSTM_DV = BRANCH_WIDTH // MLSTM_HEADS
MLSTM_DQK = MLSTM_DV // 2

N_GROUPS = 4
EXPERTS_PER_GROUP = 8
N_EXPERTS = N_GROUPS * EXPERTS_PER_GROUP
TOP_K = 2
D_EXPERT = D_MODEL // 8

IN_SPLITS = (
    BRANCH_WIDTH, BRANCH_WIDTH,
    GLA_HEADS * GLA_DK, GLA_HEADS * GLA_DK, BRANCH_WIDTH, BRANCH_WIDTH, GLA_RANK,
    BRANCH_WIDTH, BRANCH_WIDTH, BRANCH_WIDTH, BRANCH_WIDTH, GDN_HEADS, GDN_HEADS,
    MLSTM_HEADS * MLSTM_DQK, MLSTM_HEADS * MLSTM_DQK, BRANCH_WIDTH, BRANCH_WIDTH,
    MLSTM_HEADS, MLSTM_HEADS,
)
D_IN = sum(IN_SPLITS)

kernel_name = "hybrid_gated_parallel_mixers_hmoe"


def rmsnorm(x, g):
    xf = x.astype(jnp.float32)
    y = xf * lax.rsqrt(jnp.mean(xf * xf, axis=-1, keepdims=True) + NORM_EPS)
    return (y * g.astype(jnp.float32)).astype(x.dtype)


def l2norm(t):
    return t * lax.rsqrt(jnp.sum(t * t, axis=-1, keepdims=True) + NORM_EPS)


def split_cols(t, sizes):
    outs, start = [], 0
    for n in sizes:
        outs.append(t[..., start:start + n])
        start += n
    return outs


def causal_dwconv(x, w):
    k_w, s = w.shape[0], x.shape[1]
    xp = jnp.pad(x, ((0, 0), (k_w - 1, 0), (0, 0)))
    y = xp[:, 0:s] * w[0]
    for k in range(1, k_w):
        y = y + xp[:, k:k + s] * w[k]
    return y


def to_chunks(t, n_heads):
    b, s, hd = t.shape
    return t.reshape(b, s // CHUNK, CHUNK, n_heads, hd // n_heads).transpose(0, 3, 1, 2, 4)


def heads_to_chunks(t):
    b, s, h = t.shape
    return t.reshape(b, s // CHUNK, CHUNK, h).transpose(0, 3, 1, 2)


def from_chunks(t):
    b, h, n, c, d = t.shape
    return t.transpose(0, 2, 3, 1, 4).reshape(b, n * c, h * d)


def causal_masks():
    tril = jnp.tril(jnp.ones((CHUNK, CHUNK), dtype=bool))
    eye = jnp.eye(CHUNK, dtype=bool)
    return tril, eye


def rg_lru_branch(xb, gate_pre, conv_w, conv_b, w_a, b_a, w_x, b_x, lam):
    b, s, w = xb.shape
    u = causal_dwconv(xb, conv_w) + conv_b
    ub = u.reshape(b, s, LRU_BLOCKS, LRU_BLOCK)
    r = jax.nn.sigmoid(jnp.einsum('bsnk,nkj->bsnj', ub, w_a).reshape(b, s, w) + b_a)
    i = jax.nn.sigmoid(jnp.einsum('bsnk,nkj->bsnj', ub, w_x).reshape(b, s, w) + b_x)
    log_a = (-LRU_C * r.astype(jnp.float32)) * jax.nn.softplus(-lam.astype(jnp.float32))
    a = jnp.exp(log_a)
    xin = jnp.sqrt(-jnp.expm1(2.0 * log_a)) * (i * u).astype(jnp.float32)

    def combine(lhs, rhs):
        a1, h1 = lhs
        a2, h2 = rhs
        return a1 * a2, a2 * h1 + h2

    _, h = lax.associative_scan(combine, (a, xin), axis=1)
    return h.astype(xb.dtype) * jax.nn.gelu(gate_pre)


def gla_branch(q, k, v, g_out, decay_low, w_decay, b_decay, norm_g):
    dt = v.dtype
    b = v.shape[0]
    tril, _ = causal_masks()
    log_alpha = jax.nn.log_sigmoid((decay_low @ w_decay + b_decay).astype(jnp.float32)) / GLA_TAU
    qc = to_chunks(q.astype(jnp.float32), GLA_HEADS) * (GLA_DK ** -0.5)
    kc = to_chunks(k.astype(jnp.float32), GLA_HEADS)
    vc = to_chunks(v.astype(jnp.float32), GLA_HEADS)
    bcum = jnp.cumsum(to_chunks(log_alpha, GLA_HEADS), axis=3)
    q_dec = qc * jnp.exp(bcum)
    attn = jnp.einsum('bhncd,bhnjd->bhncj', q_dec, kc * jnp.exp(-bcum))
    attn = jnp.where(tril, attn, 0.0)
    o_intra = jnp.einsum('bhncj,bhnjv->bhncv', attn, vc)
    b_last = bcum[:, :, :, -1]
    k_tail = kc * jnp.exp(b_last[:, :, :, None] - bcum)
    kv = jnp.einsum('bhncd,bhncv->bhndv', k_tail, vc)

    def step(state, inp):
        dec, upd = inp
        return state * dec[..., None] + upd, state

    s0 = jnp.zeros((b, GLA_HEADS, GLA_DK, GLA_DV), jnp.float32)
    _, s_prev = lax.scan(step, s0, (jnp.moveaxis(jnp.exp(b_last), 2, 0), jnp.moveaxis(kv, 2, 0)))
    s_prev = jnp.moveaxis(s_prev, 0, 2)
    o = o_intra + jnp.einsum('bhncd,bhndv->bhncv', q_dec, s_prev)
    o = rmsnorm(o, norm_g)
    return from_chunks(o).astype(dt) * jax.nn.silu(g_out)


def gdn_branch(q, k, v, g_out, beta_pre, alpha_pre, conv_w, a_log, dt_bias, norm_g):
    dt = v.dtype
    b = v.shape[0]
    tril, eye = causal_masks()
    strict = tril & ~eye
    qkv = jax.nn.silu(causal_dwconv(jnp.concatenate([q, k, v], axis=-1), conv_w))
    q, k, v = split_cols(qkv, (BRANCH_WIDTH, BRANCH_WIDTH, BRANCH_WIDTH))
    qc = l2norm(to_chunks(q.astype(jnp.float32), GDN_HEADS)) * (GDN_HEAD_DIM ** -0.5)
    kc = l2norm(to_chunks(k.astype(jnp.float32), GDN_HEADS))
    vc = to_chunks(v.astype(jnp.float32), GDN_HEADS)
    beta = heads_to_chunks(jax.nn.sigmoid(beta_pre.astype(jnp.float32)))
    g = -jnp.exp(a_log) * jax.nn.softplus(alpha_pre.astype(jnp.float32) + dt_bias)
    gcum = jnp.cumsum(heads_to_chunks(g), axis=-1)
    diff = gcum[..., :, None] - gcum[..., None, :]
    decay = jnp.where(tril, jnp.exp(jnp.where(tril, diff, 0.0)), 0.0)
    kk = jnp.einsum('bhncd,bhnjd->bhncj', kc, kc)
    lower = jnp.where(strict, beta[..., None] * kk * decay, 0.0)
    tmat = lower + eye.astype(jnp.float32)
    rhs = jnp.concatenate([vc * beta[..., None], kc * (beta * jnp.exp(gcum))[..., None]], axis=-1)
    sol = lax.linalg.triangular_solve(tmat, rhs, left_side=True, lower=True, unit_diagonal=True)
    u_c, w_c = sol[..., :GDN_HEAD_DIM], sol[..., GDN_HEAD_DIM:]
    qk = jnp.einsum('bhncd,bhnjd->bhncj', qc, kc) * decay
    q_dec = qc * jnp.exp(gcum)[..., None]
    g_last = gcum[..., -1]
    k_tail = kc * jnp.exp(g_last[..., None] - gcum)[..., None]

    def step(state, inp):
        u_n, w_n, qd_n, qk_n, kt_n, gl_n = inp
        v_new = u_n - jnp.einsum('bhcd,bhdv->bhcv', w_n, state)
        o_n = jnp.einsum('bhcd,bhdv->bhcv', qd_n, state) + jnp.einsum('bhcj,bhjv->bhcv', qk_n, v_new)
        state = state * jnp.exp(gl_n)[..., None, None] + jnp.einsum('bhcd,bhcv->bhdv', kt_n, v_new)
        return state, o_n

    xs = tuple(jnp.moveaxis(t, 2, 0) for t in (u_c, w_c, q_dec, qk, k_tail, g_last))
    s0 = jnp.zeros((b, GDN_HEADS, GDN_HEAD_DIM, GDN_HEAD_DIM), jnp.float32)
    _, o = lax.scan(step, s0, xs)
    o = rmsnorm(jnp.moveaxis(o, 0, 2), norm_g)
    return from_chunks(o).astype(dt) * jax.nn.silu(g_out)


def mlstm_branch(q, k, v, o_pre, i_pre, f_pre, b_i, b_f, norm_g):
    dt = v.dtype
    b = v.shape[0]
    tril, _ = causal_masks()
    qc = to_chunks(q.astype(jnp.float32), MLSTM_HEADS)
    kc = to_chunks(k.astype(jnp.float32), MLSTM_HEADS) * (MLSTM_DQK ** -0.5)
    vc = to_chunks(v.astype(jnp.float32), MLSTM_HEADS)
    ig = heads_to_chunks(i_pre.astype(jnp.float32) + b_i)
    bcum = jnp.cumsum(heads_to_chunks(jax.nn.log_sigmoid(f_pre.astype(jnp.float32) + b_f)), axis=-1)
    logd = bcum[..., :, None] - bcum[..., None, :] + ig[..., None, :]
    logd = jnp.where(tril, logd, -jnp.inf)
    m_intra = jnp.max(logd, axis=-1)
    tail = bcum[..., -1:] - bcum + ig
    a_max = jnp.max(tail, axis=-1)
    k_w = kc * jnp.exp(tail - a_max[..., None])[..., None]
    kv = jnp.einsum('bhncd,bhncv->bhndv', k_w, vc)
    ksum = jnp.sum(k_w, axis=3)
    b_last = bcum[..., -1]

    def step(carry, inp):
        c_st, n_st, m_st = carry
        bl, am, kv_n, ks_n = inp
        m_new = jnp.maximum(bl + m_st, am)
        s_old = jnp.exp(bl + m_st - m_new)
        s_new = jnp.exp(am - m_new)
        c_new = c_st * s_old[..., None, None] + kv_n * s_new[..., None, None]
        n_new = n_st * s_old[..., None] + ks_n * s_new[..., None]
        return (c_new, n_new, m_new), (c_st, n_st, m_st)

    init = (jnp.zeros((b, MLSTM_HEADS, MLSTM_DQK, MLSTM_DV), jnp.float32),
            jnp.zeros((b, MLSTM_HEADS, MLSTM_DQK), jnp.float32),
            jnp.zeros((b, MLSTM_HEADS), jnp.float32))
    xs = tuple(jnp.moveaxis(t, 2, 0) for t in (b_last, a_max, kv, ksum))
    _, (c_prev, n_prev, m_prev) = lax.scan(step, init, xs)
    c_prev, n_prev, m_prev = (jnp.moveaxis(t, 0, 2) for t in (c_prev, n_prev, m_prev))
    m_inter = bcum + m_prev[..., None]
    m_i = jnp.maximum(m_inter, m_intra)
    s_inter = jnp.exp(m_inter - m_i)
    sc = jnp.einsum('bhncd,bhnjd->bhncj', qc, kc) * jnp.exp(logd - m_i[..., None])
    num = jnp.einsum('bhncj,bhnjv->bhncv', sc, vc) + s_inter[..., None] * jnp.einsum('bhncd,bhndv->bhncv', qc, c_prev)
    den = jnp.sum(sc, axis=-1) + s_inter * jnp.einsum('bhncd,bhnd->bhnc', qc, n_prev)
    h = num / jnp.maximum(jnp.abs(den), jnp.exp(-m_i))[..., None]
    h = rmsnorm(h, norm_g)
    return from_chunks(h).astype(dt) * jax.nn.sigmoid(o_pre)


def hier_moe(h, w_grp, b_grp, w_exp_r, b_exp_r, w_gate, w_up, w_down):
    b, s, d = h.shape
    t = h.reshape(b * s, d)
    n_tok = t.shape[0]
    grp_logits = (t @ w_grp).astype(jnp.float32) + b_grp
    grp = jnp.argmax(grp_logits, axis=-1)
    p_grp = jnp.take_along_axis(jax.nn.softmax(grp_logits, axis=-1), grp[:, None], axis=-1)
    exp_logits = ((t @ w_exp_r).astype(jnp.float32) + b_exp_r).reshape(n_tok, N_GROUPS, EXPERTS_PER_GROUP)
    in_grp = jnp.take_along_axis(exp_logits, grp[:, None, None], axis=1)[:, 0]
    top_val, top_idx = lax.top_k(in_grp, TOP_K)
    w_top = jax.nn.softmax(top_val, axis=-1) * p_grp
    expert_id = grp[:, None] * EXPERTS_PER_GROUP + top_idx
    comb = jnp.sum(jax.nn.one_hot(expert_id, N_EXPERTS, dtype=jnp.float32) * w_top[..., None], axis=1)
    comb = comb.reshape(n_tok, N_GROUPS, EXPERTS_PER_GROUP).astype(h.dtype)
    y = None
    for g in range(N_GROUPS):
        sl = slice(g * EXPERTS_PER_GROUP, (g + 1) * EXPERTS_PER_GROUP)
        a = jnp.einsum('td,edf->tef', t, w_gate[sl])
        u = jnp.einsum('td,edf->tef', t, w_up[sl])
        act = jax.nn.silu(a) * u * comb[:, g, :, None]
        term = jnp.einsum('tef,efd->td', act, w_down[sl])
        y = term if y is None else y + term
    return y.reshape(b, s, d)


def setup_inputs(seed: int = 0) -> dict:
    key = jax.random.key(seed)
    keys = jax.random.split(key, 48)
    ks = iter([keys[i] for i in range(48)])
    f32 = jnp.float32
    L, D, BW = DEPTH, D_MODEL, BRANCH_WIDTH

    def normal(shape, fan_in):
        return jax.random.normal(next(ks), shape, f32) * (fan_in ** -0.5)

    def gain(shape):
        return 1.0 + 0.02 * jax.random.normal(next(ks), shape, f32)

    def bias(shape, scale=0.02):
        return scale * jax.random.normal(next(ks), shape, f32)

    x = jax.random.normal(next(ks), (BATCH, SEQ, D), f32)
    mix_norm = gain((L, D))
    w_in = normal((L, D, D_IN), D)
    lru_conv_w = normal((L, CONV_WIDTH, BW), CONV_WIDTH)
    lru_conv_b = bias((L, BW))
    lru_w_a = normal((L, LRU_BLOCKS, LRU_BLOCK, LRU_BLOCK), LRU_BLOCK)
    lru_b_a = bias((L, BW))
    lru_w_x = normal((L, LRU_BLOCKS, LRU_BLOCK, LRU_BLOCK), LRU_BLOCK)
    lru_b_x = bias((L, BW))
    a_pow = jax.random.uniform(next(ks), (L, BW), f32, 0.9, 0.999)
    s_lam = a_pow ** (1.0 / LRU_C)
    lru_lambda = jnp.log(s_lam) - jnp.log1p(-s_lam)
    gla_w_decay = normal((L, GLA_RANK, GLA_HEADS * GLA_DK), GLA_RANK)
    gla_b_decay = bias((L, GLA_HEADS * GLA_DK))
    gla_norm = gain((L, GLA_DV))
    gdn_conv_w = normal((L, CONV_WIDTH, 3 * BW), CONV_WIDTH)
    gdn_a_log = jnp.log(jax.random.uniform(next(ks), (L, GDN_HEADS), f32, 1.0, 16.0))
    dt0 = jnp.exp(jax.random.uniform(next(ks), (L, GDN_HEADS), f32, np.log(1e-3), np.log(1e-1)))
    gdn_dt_bias = dt0 + jnp.log(-jnp.expm1(-dt0))
    gdn_norm = gain((L, GDN_HEAD_DIM))
    mlstm_b_i = bias((L, MLSTM_HEADS), 0.1)
    mlstm_b_f = 3.0 + bias((L, MLSTM_HEADS), 0.1)
    mlstm_norm = gain((L, MLSTM_DV))
    w_branch = normal((L, N_BRANCHES, BW, D), BW)
    w_merge_gate = normal((L, N_BRANCHES, D, D), D)
    b_merge_gate = bias((L, N_BRANCHES, D))
    w_out = normal((L, D, D), D)
    ffn_norm = gain((L, D))
    w_group_router = normal((L, D, N_GROUPS), D)
    b_group_router = bias((L, N_GROUPS), 0.01)
    w_expert_router = normal((L, D, N_EXPERTS), D)
    b_expert_router = bias((L, N_EXPERTS), 0.01)
    w_exp_gate = normal((L, N_EXPERTS, D, D_EXPERT), D)
    w_exp_up = normal((L, N_EXPERTS, D, D_EXPERT), D)
    w_exp_down = normal((L, N_EXPERTS, D_EXPERT, D), D_EXPERT)
    final_norm = gain((D,))
    return {"x": x, "mix_norm": mix_norm, "w_in": w_in,
            "lru_conv_w": lru_conv_w, "lru_conv_b": lru_conv_b, "lru_w_a": lru_w_a, "lru_b_a": lru_b_a,
            "lru_w_x": lru_w_x, "lru_b_x": lru_b_x, "lru_lambda": lru_lambda,
            "gla_w_decay": gla_w_decay, "gla_b_decay": gla_b_decay, "gla_norm": gla_norm,
            "gdn_conv_w": gdn_conv_w, "gdn_a_log": gdn_a_log, "gdn_dt_bias": gdn_dt_bias, "gdn_norm": gdn_norm,
            "mlstm_b_i": mlstm_b_i, "mlstm_b_f": mlstm_b_f, "mlstm_norm": mlstm_norm,
            "w_branch": w_branch, "w_merge_gate": w_merge_gate, "b_merge_gate": b_merge_gate, "w_out": w_out,
            "ffn_norm": ffn_norm, "w_group_router": w_group_router, "b_group_router": b_group_router,
            "w_expert_router": w_expert_router, "b_expert_router": b_expert_router,
            "w_exp_gate": w_exp_gate, "w_exp_up": w_exp_up, "w_exp_down": w_exp_down,
            "final_norm": final_norm}


def reference(x, mix_norm, w_in, lru_conv_w, lru_conv_b, lru_w_a, lru_b_a, lru_w_x, lru_b_x, lru_lambda,
              gla_w_decay, gla_b_decay, gla_norm, gdn_conv_w, gdn_a_log, gdn_dt_bias, gdn_norm,
              mlstm_b_i, mlstm_b_f, mlstm_norm, w_branch, w_merge_gate, b_merge_gate, w_out,
              ffn_norm, w_group_router, b_group_router, w_expert_router, b_expert_router,
              w_exp_gate, w_exp_up, w_exp_down, final_norm):
    for l in range(DEPTH):
        xn = rmsnorm(x, mix_norm[l])
        (a_x, a_gate, b_q, b_k, b_v, b_g, b_decay, c_q, c_k, c_v, c_g, c_beta, c_alpha,
         d_q, d_k, d_v, d_o, d_i, d_f) = split_cols(xn @ w_in[l], IN_SPLITS)
        y_a = rg_lru_branch(a_x, a_gate, lru_conv_w[l], lru_conv_b[l], lru_w_a[l], lru_b_a[l],
                            lru_w_x[l], lru_b_x[l], lru_lambda[l])
        y_b = gla_branch(b_q, b_k, b_v, b_g, b_decay, gla_w_decay[l], gla_b_decay[l], gla_norm[l])
        y_c = gdn_branch(c_q, c_k, c_v, c_g, c_beta, c_alpha, gdn_conv_w[l], gdn_a_log[l],
                         gdn_dt_bias[l], gdn_norm[l])
        y_d = mlstm_branch(d_q, d_k, d_v, d_o, d_i, d_f, mlstm_b_i[l], mlstm_b_f[l], mlstm_norm[l])
        branches = (y_a, y_b, y_c, y_d)
        merged = None
        for n in range(N_BRANCHES):
            gate = jax.nn.sigmoid(xn @ w_merge_gate[l, n] + b_merge_gate[l, n])
            term = gate * (branches[n] @ w_branch[l, n])
            merged = term if merged is None else merged + term
        x = x + merged @ w_out[l]
        x = x + hier_moe(rmsnorm(x, ffn_norm[l]), w_group_router[l], b_group_router[l],
                         w_expert_router[l], b_expert_router[l], w_exp_gate[l], w_exp_up[l], w_exp_down[l])
    return rmsnorm(x, final_norm)
```

```python
import functools

import jax
import jax.numpy as jnp
from jax import lax
from jax.experimental import pallas as pl
from jax.experimental.pallas import tpu as pltpu

F32 = jnp.float32
BF16 = jnp.bfloat16
HI = lax.Precision.HIGHEST

D_MODEL = 4096
BW = D_MODEL // 4
CONV_WIDTH = 4
CHUNK = 64
NORM_EPS = 1e-6
LRU_BLOCKS = 8
LRU_BLOCK = BW // LRU_BLOCKS
LRU_C = 8.0
GLA_HEADS = 4
GLA_DV = BW // GLA_HEADS
GLA_DK = GLA_DV // 2
GLA_RANK = 16
GLA_TAU = 16.0
GDN_HEAD_DIM = 128
GDN_HEADS = BW // GDN_HEAD_DIM
MLSTM_HEADS = 4
MLSTM_DV = BW // MLSTM_HEADS
MLSTM_DQK = MLSTM_DV // 2
N_GROUPS = 4
EXPERTS_PER_GROUP = 8
N_EXPERTS = N_GROUPS * EXPERTS_PER_GROUP
D_EXPERT = D_MODEL // 8

LANES = 128
SUBLANES = 8
VMEM_LIMIT = 56 * 1024 * 1024

PROJ_BIG = 12 * BW
OFF_AX, OFF_AG = 0, BW
OFF_BQ, OFF_BK, OFF_BV, OFF_BG = 2 * BW, 2 * BW + 512, 3 * BW, 4 * BW
OFF_CQ, OFF_CK, OFF_CV, OFF_CG = 5 * BW, 6 * BW, 7 * BW, 8 * BW
OFF_DQ, OFF_DK, OFF_DV, OFF_DO = 9 * BW, 9 * BW + 512, 10 * BW, 11 * BW
LANE_DECAY, LANE_BETA, LANE_ALPHA, LANE_I, LANE_F = 0, 16, 24, 32, 36

SEQ_BLOCK = 256
EXPERT_TILE = 256
ROW_TILE = 256


def _cparams(sem):
    return pltpu.CompilerParams(dimension_semantics=sem, vmem_limit_bytes=VMEM_LIMIT)


def _softplus(x):
    return jnp.maximum(x, 0.0) + jnp.log1p(jnp.exp(-jnp.abs(x)))


def _log_sigmoid(x):
    return -_softplus(-x)


def _sigmoid(x):
    return 1.0 / (1.0 + jnp.exp(-x))


def _silu(x):
    return x * _sigmoid(x)


def _gelu_tanh(x):
    return 0.5 * x * (1.0 + jnp.tanh(0.7978845608028654 * (x + 0.044715 * (x * x * x))))


def _dot(a, b, precision=None):
    return jnp.dot(a, b, preferred_element_type=F32, precision=precision)


def _dot_nt(a, b, precision=None):
    return lax.dot_general(a, b, (((1,), (1,)), ((), ())), preferred_element_type=F32,
                           precision=precision)


def _dot_tn(a, b, precision=None):
    return lax.dot_general(a, b, (((0,), (0,)), ((), ())), preferred_element_type=F32,
                           precision=precision)


def _iota(shape, dim):
    return lax.broadcasted_iota(jnp.int32, shape, dim)


def _pick_lane(x, lane_idx):
    return jnp.sum(jnp.where(_iota(x.shape, 1) == lane_idx, x, 0.0), axis=1, keepdims=True)


def _shift_rows(x, tail, s):
    r = pltpu.roll(x, s, 0)
    rt = pltpu.roll(tail, s, 0)
    head = jnp.where(_iota(tail.shape, 0) < s, rt, r[0:SUBLANES])
    return jnp.concatenate([head, r[SUBLANES:]], axis=0)


def _causal_conv(x, tail, w):
    y = x * w[CONV_WIDTH - 1:CONV_WIDTH, :]
    for s in range(1, CONV_WIDTH):
        y = y + _shift_rows(x, tail, s) * w[CONV_WIDTH - 1 - s:CONV_WIDTH - s, :]
    return y


def _rmsnorm_kernel(x_ref, g_ref, o_ref):
    x = x_ref[...]
    ms = jnp.mean(x * x, axis=-1, keepdims=True)
    o_ref[...] = (x * lax.rsqrt(ms + NORM_EPS) * g_ref[...]).astype(o_ref.dtype)


def _rmsnorm(x2d, g, out_dtype, tm=512):
    t, d = x2d.shape
    return pl.pallas_call(
        _rmsnorm_kernel,
        grid=(t // tm,),
        in_specs=[pl.BlockSpec((tm, d), lambda i: (i, 0)),
                  pl.BlockSpec((1, d), lambda i: (0, 0))],
        out_specs=pl.BlockSpec((tm, d), lambda i: (i, 0)),
        out_shape=jax.ShapeDtypeStruct((t, d), out_dtype),
        compiler_params=_cparams(("parallel",)),
        name="rmsnorm",
    )(x2d, g.reshape(1, d))


def _mm_kernel(a_ref, w_ref, o_ref):
    o_ref[...] = _dot(a_ref[...], w_ref[...]).astype(o_ref.dtype)


def _mm_res_kernel(a_ref, w_ref, r_ref, o_ref):
    o_ref[...] = r_ref[...] + _dot(a_ref[...], w_ref[...])


def _matmul(a, w, out_dtype=F32, residual=None, tm=512, tn=512, name="matmul"):
    m, k = a.shape
    n = w.shape[1]
    tm, tn = min(tm, m), min(tn, n)
    in_specs = [pl.BlockSpec((tm, k), lambda i, j: (i, 0)),
                pl.BlockSpec((k, tn), lambda i, j: (0, j))]
    args = [a, w]
    body = _mm_kernel
    if residual is not None:
        in_specs.append(pl.BlockSpec((tm, tn), lambda i, j: (i, j)))
        args.append(residual)
        body = _mm_res_kernel
    return pl.pallas_call(
        body,
        grid=(m // tm, n // tn),
        in_specs=in_specs,
        out_specs=pl.BlockSpec((tm, tn), lambda i, j: (i, j)),
        out_shape=jax.ShapeDtypeStruct((m, n), out_dtype),
        compiler_params=_cparams(("parallel", "parallel")),
        name=name,
    )(*args)


def _lru_kernel(x_ref, gate_ref, cw_ref, cb_ref, wa_ref, ba_ref, wx_ref, bx_ref, lam_ref,
                o_ref, tail_ref, h_ref):
    @pl.when(pl.program_id(1) == 0)
    def _():
        tail_ref[...] = jnp.zeros_like(tail_ref)
        h_ref[...] = jnp.zeros_like(h_ref)

    x = x_ref[...]
    n = x.shape[0]
    u = _causal_conv(x, tail_ref[...], cw_ref[...]) + cb_ref[...]
    tail_ref[...] = x[n - SUBLANES:, :]
    r_parts, i_parts = [], []
    for b in range(LRU_BLOCKS):
        ub = u[:, b * LRU_BLOCK:(b + 1) * LRU_BLOCK]
        r_parts.append(_dot(ub, wa_ref[b], HI))
        i_parts.append(_dot(ub, wx_ref[b], HI))
    r = _sigmoid(jnp.concatenate(r_parts, axis=1) + ba_ref[...])
    i = _sigmoid(jnp.concatenate(i_parts, axis=1) + bx_ref[...])
    log_a = (-LRU_C * r) * _softplus(-lam_ref[...])
    a = jnp.exp(log_a)
    xin = jnp.sqrt(-jnp.tanh(log_a) * (a * a + 1.0)) * (i * u)
    row = _iota(a.shape, 0)
    s = 1
    while s < n:
        keep = row >= s
        a_sh = jnp.where(keep, pltpu.roll(a, s, 0), 1.0)
        x_sh = jnp.where(keep, pltpu.roll(xin, s, 0), 0.0)
        xin = a * x_sh + xin
        a = a * a_sh
        s *= 2
    h = xin + a * h_ref[0:1, :]
    h_ref[...] = jnp.broadcast_to(h[n - 1:n, :], h_ref.shape)
    o_ref[...] = (h * _gelu_tanh(gate_ref[...])).astype(o_ref.dtype)


def _lru(proj, batch, seq, cw, cb, wa, ba, wx, bx, lam):
    t = proj.shape[0]
    nsb = seq // SEQ_BLOCK
    row = lambda b, s: b * nsb + s
    vec = lambda a: a.reshape(1, BW)
    const2 = lambda b, s: (0, 0)
    const3 = lambda b, s: (0, 0, 0)
    return pl.pallas_call(
        _lru_kernel,
        grid=(batch, nsb),
        in_specs=[pl.BlockSpec((SEQ_BLOCK, BW), lambda b, s: (row(b, s), OFF_AX // BW)),
                  pl.BlockSpec((SEQ_BLOCK, BW), lambda b, s: (row(b, s), OFF_AG // BW)),
                  pl.BlockSpec((CONV_WIDTH, BW), const2),
                  pl.BlockSpec((1, BW), const2),
                  pl.BlockSpec((LRU_BLOCKS, LRU_BLOCK, LRU_BLOCK), const3),
                  pl.BlockSpec((1, BW), const2),
                  pl.BlockSpec((LRU_BLOCKS, LRU_BLOCK, LRU_BLOCK), const3),
                  pl.BlockSpec((1, BW), const2),
                  pl.BlockSpec((1, BW), const2)],
        out_specs=pl.BlockSpec((SEQ_BLOCK, BW), lambda b, s: (row(b, s), 0)),
        out_shape=jax.ShapeDtypeStruct((t, BW), BF16),
        scratch_shapes=[pltpu.VMEM((SUBLANES, BW), F32), pltpu.VMEM((SUBLANES, BW), F32)],
        compiler_params=_cparams(("parallel", "arbitrary")),
        name="rg_lru",
    )(proj, proj, cw, vec(cb), wa, vec(ba), wx, vec(bx), vec(lam))


def _gla_kernel(q_ref, k_ref, v_ref, g_ref, sm_ref, wd_ref, bd_ref, ng_ref, o_ref, st_ref):
    @pl.when(pl.program_id(2) == 0)
    def _():
        st_ref[...] = jnp.zeros_like(st_ref)

    tril = _iota((CHUNK, CHUNK), 0) >= _iota((CHUNK, CHUNK), 1)
    tril_f = tril.astype(F32)
    for c in range(SEQ_BLOCK // CHUNK):
        sl = slice(c * CHUNK, (c + 1) * CHUNK)
        qc = q_ref[sl, :] * (GLA_DK ** -0.5)
        kc = k_ref[sl, :]
        vc = v_ref[sl, :]
        log_alpha = _log_sigmoid(_dot(sm_ref[sl, :], wd_ref[...], HI) + bd_ref[...]) / GLA_TAU
        bcum = _dot(tril_f, log_alpha, HI)
        q_dec = qc * jnp.exp(bcum)
        attn = jnp.where(tril, _dot_nt(q_dec, kc * jnp.exp(-bcum), HI), 0.0)
        b_last = bcum[CHUNK - 1:CHUNK, :]
        k_tail = kc * jnp.exp(b_last - bcum)
        st = st_ref[...]
        o = _dot(attn, vc, HI) + _dot_nt(q_dec, st, HI)
        st_ref[...] = st * jnp.exp(b_last) + _dot_tn(vc, k_tail, HI)
        o = o * lax.rsqrt(jnp.mean(o * o, axis=-1, keepdims=True) + NORM_EPS) * ng_ref[...]
        o_ref[sl, :] = (o * _silu(g_ref[sl, :])).astype(o_ref.dtype)


def _gla(proj, small, batch, seq, w_decay, b_decay, norm_g):
    t = proj.shape[0]
    nsb = seq // SEQ_BLOCK
    row = lambda b, s: b * nsb + s
    wd = jnp.zeros((LANES, GLA_HEADS * GLA_DK), F32).at[LANE_DECAY:LANE_DECAY + GLA_RANK].set(w_decay)
    return pl.pallas_call(
        _gla_kernel,
        grid=(batch, GLA_HEADS, nsb),
        in_specs=[pl.BlockSpec((SEQ_BLOCK, GLA_DK), lambda b, h, s: (row(b, s), OFF_BQ // GLA_DK + h)),
                  pl.BlockSpec((SEQ_BLOCK, GLA_DK), lambda b, h, s: (row(b, s), OFF_BK // GLA_DK + h)),
                  pl.BlockSpec((SEQ_BLOCK, GLA_DV), lambda b, h, s: (row(b, s), OFF_BV // GLA_DV + h)),
                  pl.BlockSpec((SEQ_BLOCK, GLA_DV), lambda b, h, s: (row(b, s), OFF_BG // GLA_DV + h)),
                  pl.BlockSpec((SEQ_BLOCK, LANES), lambda b, h, s: (row(b, s), 0)),
                  pl.BlockSpec((LANES, GLA_DK), lambda b, h, s: (0, h)),
                  pl.BlockSpec((1, GLA_DK), lambda b, h, s: (0, h)),
                  pl.BlockSpec((1, GLA_DV), lambda b, h, s: (0, 0))],
        out_specs=pl.BlockSpec((SEQ_BLOCK, GLA_DV), lambda b, h, s: (row(b, s), h)),
        out_shape=jax.ShapeDtypeStruct((t, BW), BF16),
        scratch_shapes=[pltpu.VMEM((GLA_DV, GLA_DK), F32)],
        compiler_params=_cparams(("parallel", "parallel", "arbitrary")),
        name="gla",
    )(proj, proj, proj, proj, small, wd, b_decay.reshape(1, -1), norm_g.reshape(1, -1))


def _unit_lower_inverse(low):
    ri = _iota((CHUNK, CHUNK), 0)
    ci = _iota((CHUNK, CHUNK), 1)
    eye = (ri == ci).astype(F32)
    same16 = (ri // 16) == (ci // 16)
    same32 = (ri // 32) == (ci // 32)
    d1 = jnp.where(same16, low, 0.0)
    d2 = _dot(d1, d1, HI)
    d4 = _dot(d2, d2, HI)
    d8 = _dot(d4, d4, HI)
    m = eye - d1
    m = m + _dot(m, d2, HI)
    m = m + _dot(m, d4, HI)
    m = m + _dot(m, d8, HI)
    c1 = jnp.where(same32 & ~same16, low, 0.0)
    m = m - _dot(_dot(m, c1, HI), m, HI)
    c2 = jnp.where(same32, 0.0, low)
    m = m - _dot(_dot(m, c2, HI), m, HI)
    return m


def _gdn_kernel(q_ref, k_ref, v_ref, g_ref, sm_ref, cwq_ref, cwk_ref, cwv_ref, alog_ref, dtb_ref,
                ng_ref, o_ref, st_ref, tq_ref, tk_ref, tv_ref):
    head = pl.program_id(1)

    @pl.when(pl.program_id(2) == 0)
    def _():
        st_ref[...] = jnp.zeros_like(st_ref)
        tq_ref[...] = jnp.zeros_like(tq_ref)
        tk_ref[...] = jnp.zeros_like(tk_ref)
        tv_ref[...] = jnp.zeros_like(tv_ref)

    n = SEQ_BLOCK
    q_raw, k_raw, v_raw = q_ref[...], k_ref[...], v_ref[...]
    q = _silu(_causal_conv(q_raw, tq_ref[...], cwq_ref[...]))
    k = _silu(_causal_conv(k_raw, tk_ref[...], cwk_ref[...]))
    v = _silu(_causal_conv(v_raw, tv_ref[...], cwv_ref[...]))
    tq_ref[...] = q_raw[n - SUBLANES:, :]
    tk_ref[...] = k_raw[n - SUBLANES:, :]
    tv_ref[...] = v_raw[n - SUBLANES:, :]
    q = q * lax.rsqrt(jnp.sum(q * q, axis=-1, keepdims=True) + NORM_EPS) * (GDN_HEAD_DIM ** -0.5)
    k = k * lax.rsqrt(jnp.sum(k * k, axis=-1, keepdims=True) + NORM_EPS)
    sm = sm_ref[...]
    beta_all = _pick_lane(_sigmoid(sm), LANE_BETA + head)
    g_lanes = -jnp.exp(alog_ref[...]) * _softplus(sm + dtb_ref[...])

    ri = _iota((CHUNK, CHUNK), 0)
    ci = _iota((CHUNK, CHUNK), 1)
    tril = ri >= ci
    strict = ri > ci
    tril_f = tril.astype(F32)
    sel = (_iota((CHUNK, LANES), 1) == LANE_ALPHA + head).astype(F32)
    for c in range(SEQ_BLOCK // CHUNK):
        sl = slice(c * CHUNK, (c + 1) * CHUNK)
        qc, kc, vc = q[sl, :], k[sl, :], v[sl, :]
        beta = beta_all[sl, :]
        gcum_lanes = _dot(tril_f, g_lanes[sl, :], HI)
        gcum = _pick_lane(gcum_lanes, LANE_ALPHA + head)
        gcum_row = _dot_nt(sel, gcum_lanes, HI)
        diff = gcum - gcum_row
        decay = jnp.where(tril, jnp.exp(jnp.where(tril, diff, 0.0)), 0.0)
        kk = _dot_nt(kc, kc, HI)
        low = jnp.where(strict, beta * kk * decay, 0.0)
        minv = _unit_lower_inverse(low)
        u_c = _dot(minv, vc * beta, HI)
        w_c = _dot(minv, kc * (beta * jnp.exp(gcum)), HI)
        qk = _dot_nt(qc, kc, HI) * decay
        q_dec = qc * jnp.exp(gcum)
        g_last = gcum[CHUNK - 1:CHUNK, :]
        k_tail = kc * jnp.exp(g_last - gcum)
        st = st_ref[...]
        v_new = u_c - _dot(w_c, st, HI)
        o = _dot(q_dec, st, HI) + _dot(qk, v_new, HI)
        st_ref[...] = st * jnp.exp(g_last) + _dot_tn(k_tail, v_new, HI)
        o = o * lax.rsqrt(jnp.mean(o * o, axis=-1, keepdims=True) + NORM_EPS) * ng_ref[...]
        o_ref[sl, :] = (o * _silu(g_ref[sl, :])).astype(o_ref.dtype)


def _gdn(proj, small, batch, seq, conv_w, a_log, dt_bias, norm_g):
    t = proj.shape[0]
    nsb = seq // SEQ_BLOCK
    hd = GDN_HEAD_DIM
    row = lambda b, s: b * nsb + s
    lane_row = lambda a: jnp.zeros((1, LANES), F32).at[0, LANE_ALPHA:LANE_ALPHA + GDN_HEADS].set(a)
    return pl.pallas_call(
        _gdn_kernel,
        grid=(batch, GDN_HEADS, nsb),
        in_specs=[pl.BlockSpec((SEQ_BLOCK, hd), lambda b, h, s: (row(b, s), OFF_CQ // hd + h)),
                  pl.BlockSpec((SEQ_BLOCK, hd), lambda b, h, s: (row(b, s), OFF_CK // hd + h)),
                  pl.BlockSpec((SEQ_BLOCK, hd), lambda b, h, s: (row(b, s), OFF_CV // hd + h)),
                  pl.BlockSpec((SEQ_BLOCK, hd), lambda b, h, s: (row(b, s), OFF_CG // hd + h)),
                  pl.BlockSpec((SEQ_BLOCK, LANES), lambda b, h, s: (row(b, s), 0)),
                  pl.BlockSpec((CONV_WIDTH, hd), lambda b, h, s: (0, h)),
                  pl.BlockSpec((CONV_WIDTH, hd), lambda b, h, s: (0, GDN_HEADS + h)),
                  pl.BlockSpec((CONV_WIDTH, hd), lambda b, h, s: (0, 2 * GDN_HEADS + h)),
                  pl.BlockSpec((1, LANES), lambda b, h, s: (0, 0)),
                  pl.BlockSpec((1, LANES), lambda b, h, s: (0, 0)),
                  pl.BlockSpec((1, hd), lambda b, h, s: (0, 0))],
        out_specs=pl.BlockSpec((SEQ_BLOCK, hd), lambda b, h, s: (row(b, s), h)),
        out_shape=jax.ShapeDtypeStruct((t, BW), BF16),
        scratch_shapes=[pltpu.VMEM((hd, hd), F32)] + [pltpu.VMEM((SUBLANES, hd), F32)] * 3,
        compiler_params=_cparams(("parallel", "parallel", "arbitrary")),
        name="gdn",
    )(proj, proj, proj, proj, small, conv_w, conv_w, conv_w, lane_row(a_log), lane_row(dt_bias),
      norm_g.reshape(1, -1))


def _mlstm_kernel(q_ref, k_ref, v_ref, og_ref, sm_ref, bi_ref, bf_ref, ng_ref, o_ref,
                  c_ref, n_ref, m_ref):
    head = pl.program_id(1)

    @pl.when(pl.program_id(2) == 0)
    def _():
        c_ref[...] = jnp.zeros_like(c_ref)
        n_ref[...] = jnp.zeros_like(n_ref)
        m_ref[...] = jnp.zeros_like(m_ref)

    sm = sm_ref[...]
    ig_all = _pick_lane(sm + bi_ref[...], LANE_I + head)
    logf_lanes = _log_sigmoid(sm + bf_ref[...])
    tril = _iota((CHUNK, CHUNK), 0) >= _iota((CHUNK, CHUNK), 1)
    tril_f = tril.astype(F32)
    lane0 = (_iota((CHUNK, LANES), 1) == 0).astype(F32)
    for c in range(SEQ_BLOCK // CHUNK):
        sl = slice(c * CHUNK, (c + 1) * CHUNK)
        qc = q_ref[sl, :]
        kc = k_ref[sl, :] * (MLSTM_DQK ** -0.5)
        vc = v_ref[sl, :]
        ig = ig_all[sl, :]
        bcum = _pick_lane(_dot(tril_f, logf_lanes[sl, :], HI), LANE_F + head)
        src = jnp.broadcast_to(ig - bcum, (CHUNK, LANES))
        logd = bcum + _dot_nt(lane0, src, HI)
        logd = jnp.where(tril, logd, -jnp.inf)
        m_intra = jnp.max(logd, axis=1, keepdims=True)
        b_last = bcum[CHUNK - 1:CHUNK, :]
        tail = b_last - bcum + ig
        a_max = jnp.max(tail, axis=0, keepdims=True)
        k_w = kc * jnp.exp(tail - a_max)
        kv = _dot_tn(k_w, vc, HI)
        ksum = jnp.sum(k_w, axis=0, keepdims=True)
        c_st, n_st, m_st = c_ref[...], n_ref[0:1, :], m_ref[0:1, 0:1]
        m_inter = bcum + m_st
        m_i = jnp.maximum(m_inter, m_intra)
        s_inter = jnp.exp(m_inter - m_i)
        sc = _dot_nt(qc, kc, HI) * jnp.exp(logd - m_i)
        num = _dot(sc, vc, HI) + s_inter * _dot(qc, c_st, HI)
        den = jnp.sum(sc, axis=1, keepdims=True) + s_inter * jnp.sum(qc * n_st, axis=1, keepdims=True)
        h = num / jnp.maximum(jnp.abs(den), jnp.exp(-m_i))
        m_new = jnp.maximum(b_last + m_st, a_max)
        s_old = jnp.exp(b_last + m_st - m_new)
        s_new = jnp.exp(a_max - m_new)
        c_ref[...] = c_st * s_old + kv * s_new
        n_ref[...] = jnp.broadcast_to(n_st * s_old + ksum * s_new, n_ref.shape)
        m_ref[...] = jnp.broadcast_to(m_new, m_ref.shape)
        h = h * lax.rsqrt(jnp.mean(h * h, axis=-1, keepdims=True) + NORM_EPS) * ng_ref[...]
        o_ref[sl, :] = (h * _sigmoid(og_ref[sl, :])).astype(o_ref.dtype)


def _mlstm(proj, small, batch, seq, b_i, b_f, norm_g):
    t = proj.shape[0]
    nsb = seq // SEQ_BLOCK
    dqk, dv = MLSTM_DQK, MLSTM_DV
    row = lambda b, s: b * nsb + s
    lane_row = lambda a, off: jnp.zeros((1, LANES), F32).at[0, off:off + MLSTM_HEADS].set(a)
    return pl.pallas_call(
        _mlstm_kernel,
        grid=(batch, MLSTM_HEADS, nsb),
        in_specs=[pl.BlockSpec((SEQ_BLOCK, dqk), lambda b, h, s: (row(b, s), OFF_DQ // dqk + h)),
                  pl.BlockSpec((SEQ_BLOCK, dqk), lambda b, h, s: (row(b, s), OFF_DK // dqk + h)),
                  pl.BlockSpec((SEQ_BLOCK, dv), lambda b, h, s: (row(b, s), OFF_DV // dv + h)),
                  pl.BlockSpec((SEQ_BLOCK, dv), lambda b, h, s: (row(b, s), OFF_DO // dv + h)),
                  pl.BlockSpec((SEQ_BLOCK, LANES), lambda b, h, s: (row(b, s), 0)),
                  pl.BlockSpec((1, LANES), lambda b, h, s: (0, 0)),
                  pl.BlockSpec((1, LANES), lambda b, h, s: (0, 0)),
                  pl.BlockSpec((1, dv), lambda b, h, s: (0, 0))],
        out_specs=pl.BlockSpec((SEQ_BLOCK, dv), lambda b, h, s: (row(b, s), h)),
        out_shape=jax.ShapeDtypeStruct((t, BW), BF16),
        scratch_shapes=[pltpu.VMEM((dqk, dv), F32), pltpu.VMEM((SUBLANES, dqk), F32),
                        pltpu.VMEM((SUBLANES, LANES), F32)],
        compiler_params=_cparams(("parallel", "parallel", "arbitrary")),
        name="mlstm",
    )(proj, proj, proj, proj, small, lane_row(b_i, LANE_I), lane_row(b_f, LANE_F),
      norm_g.reshape(1, -1))


def _merge_kernel(xn_ref, ya_ref, yb_ref, yc_ref, yd_ref, wg_ref, bg_ref, wb_ref, o_ref):
    xn = xn_ref[...]
    acc = None
    for n, y_ref in enumerate((ya_ref, yb_ref, yc_ref, yd_ref)):
        gate = _sigmoid(_dot(xn, wg_ref[n]) + bg_ref[n])
        term = gate * _dot(y_ref[...], wb_ref[n])
        acc = term if acc is None else acc + term
    o_ref[...] = acc.astype(o_ref.dtype)


def _merge(xn, ys, wg, bg, wb, tm=512, tn=256):
    t, d = xn.shape
    tm = min(tm, t)
    nb = wg.shape[0]
    yspec = pl.BlockSpec((tm, BW), lambda i, j: (i, 0))
    return pl.pallas_call(
        _merge_kernel,
        grid=(t // tm, d // tn),
        in_specs=[pl.BlockSpec((tm, d), lambda i, j: (i, 0)), yspec, yspec, yspec, yspec,
                  pl.BlockSpec((nb, d, tn), lambda i, j: (0, 0, j)),
                  pl.BlockSpec((nb, 1, tn), lambda i, j: (0, 0, j)),
                  pl.BlockSpec((nb, BW, tn), lambda i, j: (0, 0, j))],
        out_specs=pl.BlockSpec((tm, tn), lambda i, j: (i, j)),
        out_shape=jax.ShapeDtypeStruct((t, d), BF16),
        compiler_params=_cparams(("parallel", "parallel")),
        name="gated_merge",
    )(xn, *ys, wg, bg.reshape(nb, 1, d), wb)


def _router_kernel(x_ref, g_ref, wr_ref, br_ref, h_ref, info_ref, idx_ref, cnt_ref, carry_ref):
    @pl.when(pl.program_id(0) == 0)
    def _():
        carry_ref[...] = jnp.zeros_like(carry_ref)

    x = x_ref[...]
    tm = x.shape[0]
    h = x * lax.rsqrt(jnp.mean(x * x, axis=-1, keepdims=True) + NORM_EPS) * g_ref[...]
    h_ref[...] = h
    logits = _dot(h, wr_ref[...], HI) + br_ref[...]
    lane = _iota((tm, LANES), 1).astype(F32)
    big = float(LANES)
    gl = jnp.where(lane < N_GROUPS, logits, -jnp.inf)
    gmax = jnp.max(gl, axis=1, keepdims=True)
    grp = jnp.min(jnp.where(gl == gmax, lane, big), axis=1, keepdims=True)
    p_grp = 1.0 / jnp.sum(jnp.exp(gl - gmax), axis=1, keepdims=True)
    lo = N_GROUPS + grp * EXPERTS_PER_GROUP
    el = jnp.where((lane >= lo) & (lane < lo + EXPERTS_PER_GROUP), logits, -jnp.inf)
    v0 = jnp.max(el, axis=1, keepdims=True)
    i0 = jnp.min(jnp.where(el == v0, lane, big), axis=1, keepdims=True)
    el = jnp.where(lane == i0, -jnp.inf, el)
    v1 = jnp.max(el, axis=1, keepdims=True)
    i1 = jnp.min(jnp.where(el == v1, lane, big), axis=1, keepdims=True)
    e = jnp.exp(v1 - v0)
    w0 = p_grp / (1.0 + e)
    w1 = p_grp * e / (1.0 + e)
    e0 = i0 - N_GROUPS
    e1 = i1 - N_GROUPS
    oh0 = (lane == e0).astype(F32)
    oh1 = (lane == e1).astype(F32)
    both = oh0 + oh1
    strict = (_iota((tm, tm), 0) > _iota((tm, tm), 1)).astype(BF16)
    before = carry_ref[0:1, :] + _dot(strict, both.astype(BF16))
    r0 = jnp.sum(oh0 * before, axis=1, keepdims=True)
    r1 = jnp.sum(oh1 * before, axis=1, keepdims=True)
    total = carry_ref[0:1, :] + jnp.sum(both, axis=0, keepdims=True)
    carry_ref[...] = jnp.broadcast_to(total, carry_ref.shape)
    cnt_ref[...] = jnp.broadcast_to(total, cnt_ref.shape)
    info_ref[...] = jnp.where(lane == 0.0, w0, jnp.where(lane == 1.0, w1, 0.0))
    idx = jnp.where(lane == 0.0, e0, jnp.where(lane == 1.0, e1,
                                               jnp.where(lane == 2.0, r0, jnp.where(lane == 3.0, r1, 0.0))))
    idx_ref[...] = idx.astype(jnp.int32)


def _router(x2d, g, w_grp, b_grp, w_exp, b_exp, tm=256):
    t, d = x2d.shape
    tm = min(tm, t)
    wr = jnp.zeros((d, LANES), F32).at[:, :N_GROUPS].set(w_grp).at[:, N_GROUPS:N_GROUPS + N_EXPERTS].set(w_exp)
    br = jnp.zeros((1, LANES), F32).at[0, :N_GROUPS].set(b_grp).at[0, N_GROUPS:N_GROUPS + N_EXPERTS].set(b_exp)
    return pl.pallas_call(
        _router_kernel,
        grid=(t // tm,),
        in_specs=[pl.BlockSpec((tm, d), lambda i: (i, 0)),
                  pl.BlockSpec((1, d), lambda i: (0, 0)),
                  pl.BlockSpec((d, LANES), lambda i: (0, 0)),
                  pl.BlockSpec((1, LANES), lambda i: (0, 0))],
        out_specs=[pl.BlockSpec((tm, d), lambda i: (i, 0)),
                   pl.BlockSpec((tm, LANES), lambda i: (i, 0)),
                   pl.BlockSpec((tm, LANES), lambda i: (i, 0)),
                   pl.BlockSpec((SUBLANES, LANES), lambda i: (0, 0))],
        out_shape=[jax.ShapeDtypeStruct((t, d), F32),
                   jax.ShapeDtypeStruct((t, LANES), F32),
                   jax.ShapeDtypeStruct((t, LANES), jnp.int32),
                   jax.ShapeDtypeStruct((SUBLANES, LANES), F32)],
        scratch_shapes=[pltpu.VMEM((SUBLANES, LANES), F32)],
        compiler_params=_cparams(("arbitrary",)),
        name="router",
    )(x2d, g.reshape(1, d), wr, br)


def _dispatch_kernel(pos_ref, h_ref, zero_ref, xs_ref, sem):
    del zero_ref
    i = pl.program_id(0)
    tm = h_ref.shape[0]

    def issue(r, carry):
        t = i * tm + r
        for k in range(2):
            pltpu.make_async_copy(h_ref.at[pl.ds(r, 1)], xs_ref.at[pl.ds(pos_ref[2 * t + k], 1)],
                                  sem).start()
        return carry

    lax.fori_loop(0, tm, issue, 0)

    def drain(r, carry):
        for k in range(2):
            pltpu.make_async_copy(h_ref.at[pl.ds(r, 1)], xs_ref.at[pl.ds(0, 1)], sem).wait()
        return carry

    lax.fori_loop(0, tm, drain, 0)


def _dispatch(pos, h, n_rows, tm=ROW_TILE):
    t, d = h.shape
    tm = min(tm, t)
    zeros = jnp.zeros((n_rows, d), F32)
    return pl.pallas_call(
        _dispatch_kernel,
        grid_spec=pltpu.PrefetchScalarGridSpec(
            num_scalar_prefetch=1,
            grid=(t // tm,),
            in_specs=[pl.BlockSpec((tm, d), lambda i, pos: (i, 0)),
                      pl.BlockSpec(memory_space=pl.ANY)],
            out_specs=pl.BlockSpec(memory_space=pl.ANY),
            scratch_shapes=[pltpu.SemaphoreType.DMA(())],
        ),
        out_shape=jax.ShapeDtypeStruct((n_rows, d), F32),
        input_output_aliases={2: 0},
        compiler_params=_cparams(("arbitrary",)),
        name="dispatch",
    )(pos, h, zeros)


def _combine_kernel(pos_ref, x_ref, info_ref, ys_ref, o_ref, buf0_ref, buf1_ref, sem):
    i = pl.program_id(0)
    tm = x_ref.shape[0]
    bufs = (buf0_ref, buf1_ref)

    def issue(r, carry):
        t = i * tm + r
        for k in range(2):
            pltpu.make_async_copy(ys_ref.at[pl.ds(pos_ref[2 * t + k], 1)], bufs[k].at[pl.ds(r, 1)],
                                  sem).start()
        return carry

    lax.fori_loop(0, tm, issue, 0)

    def drain(r, carry):
        for k in range(2):
            pltpu.make_async_copy(ys_ref.at[pl.ds(0, 1)], bufs[k].at[pl.ds(r, 1)], sem).wait()
        return carry

    lax.fori_loop(0, tm, drain, 0)
    info = info_ref[...]
    o_ref[...] = x_ref[...] + info[:, 0:1] * buf0_ref[...] + info[:, 1:2] * buf1_ref[...]


def _combine(pos, x2d, info, ys, tm=ROW_TILE):
    t, d = x2d.shape
    tm = min(tm, t)
    return pl.pallas_call(
        _combine_kernel,
        grid_spec=pltpu.PrefetchScalarGridSpec(
            num_scalar_prefetch=1,
            grid=(t // tm,),
            in_specs=[pl.BlockSpec((tm, d), lambda i, pos: (i, 0)),
                      pl.BlockSpec((tm, LANES), lambda i, pos: (i, 0)),
                      pl.BlockSpec(memory_space=pl.ANY)],
            out_specs=pl.BlockSpec((tm, d), lambda i, pos: (i, 0)),
            scratch_shapes=[pltpu.VMEM((tm, d), F32), pltpu.VMEM((tm, d), F32),
                            pltpu.SemaphoreType.DMA(())],
        ),
        out_shape=jax.ShapeDtypeStruct((t, d), F32),
        compiler_params=_cparams(("arbitrary",)),
        name="combine",
    )(pos, x2d, info, ys)


def _expert_kernel(te_ref, nu_ref, xs_ref, wg_ref, wu_ref, wd_ref, ys_ref):
    del te_ref
    i = pl.program_id(0)

    @pl.when(i < nu_ref[0])
    def _():
        x = xs_ref[...].astype(BF16)
        a = _dot(x, wg_ref[0])
        u = _dot(x, wu_ref[0])
        act = (_silu(a) * u).astype(BF16)
        ys_ref[...] = _dot(act, wd_ref[0])

    @pl.when(i >= nu_ref[0])
    def _():
        ys_ref[...] = jnp.zeros_like(ys_ref)


def _experts(tile_expert, n_used, xs, wg, wu, wd):
    p, d = xs.shape
    f = wg.shape[2]
    tm = EXPERT_TILE
    src_tile = lambda i, te, nu: (jnp.minimum(i, nu[0] - 1), 0)
    return pl.pallas_call(
        _expert_kernel,
        grid_spec=pltpu.PrefetchScalarGridSpec(
            num_scalar_prefetch=2,
            grid=(p // tm,),
            in_specs=[pl.BlockSpec((tm, d), src_tile),
                      pl.BlockSpec((1, d, f), lambda i, te, nu: (te[i], 0, 0)),
                      pl.BlockSpec((1, d, f), lambda i, te, nu: (te[i], 0, 0)),
                      pl.BlockSpec((1, f, d), lambda i, te, nu: (te[i], 0, 0))],
            out_specs=pl.BlockSpec((tm, d), lambda i, te, nu: (i, 0)),
        ),
        out_shape=jax.ShapeDtypeStruct((p, d), F32),
        compiler_params=_cparams(("arbitrary",)),
        name="experts",
    )(tile_expert, n_used, xs, wg, wu, wd)


def _hier_moe(x2d, ffn_g, w_grp, b_grp, w_exp, b_exp, wg, wu, wd):
    t, d = x2d.shape
    tm = EXPERT_TILE
    n_rows = 2 * t + N_EXPERTS * tm
    h, info, idx, cnt = _router(x2d, ffn_g, w_grp, b_grp, w_exp, b_exp)
    counts = cnt[0, :N_EXPERTS].astype(jnp.int32)
    padded = ((counts + tm - 1) // tm) * tm
    ends = jnp.cumsum(padded)
    offs = ends - padded
    pos = (offs[idx[:, 0:2]] + idx[:, 2:4]).reshape(2 * t)
    n_used = (ends[-1] // tm).reshape(1)
    tiles = jnp.minimum(jnp.arange(n_rows // tm, dtype=jnp.int32), n_used[0] - 1)
    tile_expert = jnp.minimum(jnp.searchsorted(ends // tm, tiles, side="right"),
                              N_EXPERTS - 1).astype(jnp.int32)
    xs = _dispatch(pos, h, n_rows)
    ys = _experts(tile_expert, n_used.astype(jnp.int32), xs, wg, wu, wd)
    return _combine(pos, x2d, info, ys)


def _split_w_in(w):
    c0 = 5 * BW
    c1 = c0 + GLA_RANK
    c2 = c1 + 4 * BW
    c3 = c2 + 2 * GDN_HEADS
    c4 = c3 + 3 * BW
    c5 = c4 + 2 * MLSTM_HEADS
    big = jnp.concatenate([w[:, :c0], w[:, c1:c2], w[:, c3:c4]], axis=1)
    n_small = GLA_RANK + 2 * GDN_HEADS + 2 * MLSTM_HEADS
    small = jnp.concatenate([w[:, c0:c1], w[:, c2:c3], w[:, c4:c5],
                             jnp.zeros((w.shape[0], LANES - n_small), w.dtype)], axis=1)
    return big.astype(BF16), small.astype(BF16)


def kernel(x, mix_norm, w_in, lru_conv_w, lru_conv_b, lru_w_a, lru_b_a, lru_w_x, lru_b_x, lru_lambda,
           gla_w_decay, gla_b_decay, gla_norm, gdn_conv_w, gdn_a_log, gdn_dt_bias, gdn_norm,
           mlstm_b_i, mlstm_b_f, mlstm_norm, w_branch, w_merge_gate, b_merge_gate, w_out,
           ffn_norm, w_group_router, b_group_router, w_expert_router, b_expert_router,
           w_exp_gate, w_exp_up, w_exp_down, final_norm):
    batch, seq, d = x.shape
    depth = w_in.shape[0]
    x2d = x.reshape(batch * seq, d)
    for l in range(depth):
        w_big, w_small = _split_w_in(w_in[l])
        xn = _rmsnorm(x2d, mix_norm[l], BF16)
        proj = _matmul(xn, w_big, name="in_proj")
        small = _matmul(xn, w_small, name="in_proj_small")
        y_a = _lru(proj, batch, seq, lru_conv_w[l], lru_conv_b[l], lru_w_a[l], lru_b_a[l],
                   lru_w_x[l], lru_b_x[l], lru_lambda[l])
        y_b = _gla(proj, small, batch, seq, gla_w_decay[l], gla_b_decay[l], gla_norm[l])
        y_c = _gdn(proj, small, batch, seq, gdn_conv_w[l], gdn_a_log[l], gdn_dt_bias[l], gdn_norm[l])
        y_d = _mlstm(proj, small, batch, seq, mlstm_b_i[l], mlstm_b_f[l], mlstm_norm[l])
        merged = _merge(xn, (y_a, y_b, y_c, y_d), w_merge_gate[l].astype(BF16), b_merge_gate[l],
                        w_branch[l].astype(BF16))
        x2d = _matmul(merged, w_out[l].astype(BF16), residual=x2d, name="out_proj")
        x2d = _hier_moe(x2d, ffn_norm[l], w_group_router[l], b_group_router[l],
                        w_expert_router[l], b_expert_router[l], w_exp_gate[l].astype(BF16),
                        w_exp_up[l].astype(BF16), w_exp_down[l].astype(BF16))
    return _rmsnorm(x2d, final_norm, F32).reshape(batch, seq, d)
```

```python
import jax
import jax.numpy as jnp
from jax import lax
from jax.experimental import pallas as pl
from jax.experimental.pallas import tpu as pltpu

F32 = jnp.float32
BF16 = jnp.bfloat16
HI = lax.Precision.HIGHEST

D_MODEL = 4096
BW = D_MODEL // 4
CONV_WIDTH = 4
CHUNK = 64
NORM_EPS = 1e-6
LRU_BLOCKS = 8
LRU_BLOCK = BW // LRU_BLOCKS
LRU_C = 8.0
GLA_HEADS = 4
GLA_DV = BW // GLA_HEADS
GLA_DK = GLA_DV // 2
GLA_RANK = 16
GLA_TAU = 16.0
GDN_HEAD_DIM = 128
GDN_HEADS = BW // GDN_HEAD_DIM
MLSTM_HEADS = 4
MLSTM_DV = BW // MLSTM_HEADS
MLSTM_DQK = MLSTM_DV // 2
N_GROUPS = 4
EXPERTS_PER_GROUP = 8
N_EXPERTS = N_GROUPS * EXPERTS_PER_GROUP
D_EXPERT = D_MODEL // 8

LANES = 128
SUBLANES = 8
VMEM_LIMIT = 56 * 1024 * 1024

PROJ_BIG = 12 * BW
OFF_AX, OFF_AG = 0, BW
OFF_BQ, OFF_BK, OFF_BV, OFF_BG = 2 * BW, 2 * BW + 512, 3 * BW, 4 * BW
OFF_CQ, OFF_CK, OFF_CV, OFF_CG = 5 * BW, 6 * BW, 7 * BW, 8 * BW
OFF_DQ, OFF_DK, OFF_DV, OFF_DO = 9 * BW, 9 * BW + 512, 10 * BW, 11 * BW
LANE_DECAY, LANE_BETA, LANE_ALPHA, LANE_I, LANE_F = 0, 16, 24, 32, 36

SEQ_BLOCK = 256
N_CHUNKS = SEQ_BLOCK // CHUNK
EXPERT_TILE = 256
ROW_TILE = 256


def _cparams(sem):
    return pltpu.CompilerParams(dimension_semantics=sem, vmem_limit_bytes=VMEM_LIMIT)


def _softplus(x):
    return jnp.maximum(x, 0.0) + jnp.log1p(jnp.exp(-jnp.abs(x)))


def _log_sigmoid(x):
    return -_softplus(-x)


def _sigmoid(x):
    return 1.0 / (1.0 + jnp.exp(-x))


def _silu(x):
    return x * _sigmoid(x)


def _gelu_tanh(x):
    return 0.5 * x * (1.0 + jnp.tanh(0.7978845608028654 * (x + 0.044715 * (x * x * x))))


def _dot(a, b, precision=None):
    return jnp.dot(a, b, preferred_element_type=F32, precision=precision)


def _dot_nt(a, b, precision=None):
    return lax.dot_general(a, b, (((1,), (1,)), ((), ())), preferred_element_type=F32,
                           precision=precision)


def _dot_tn(a, b, precision=None):
    return lax.dot_general(a, b, (((0,), (0,)), ((), ())), preferred_element_type=F32,
                           precision=precision)


def _iota(shape, dim):
    return lax.broadcasted_iota(jnp.int32, shape, dim)


def _shift_rows(x, tail, s):
    r = pltpu.roll(x, s, 0)
    rt = pltpu.roll(tail, s, 0)
    head = jnp.where(_iota(tail.shape, 0) < s, rt, r[0:SUBLANES])
    return jnp.concatenate([head, r[SUBLANES:]], axis=0)


def _causal_conv(x, tail, w):
    y = x * w[CONV_WIDTH - 1:CONV_WIDTH, :]
    for s in range(1, CONV_WIDTH):
        y = y + _shift_rows(x, tail, s) * w[CONV_WIDTH - 1 - s:CONV_WIDTH - s, :]
    return y


def _chunk_cumsum(x):
    pos = _iota(x.shape, 0) & (CHUNK - 1)
    s = 1
    while s < CHUNK:
        x = x + jnp.where(pos >= s, pltpu.roll(x, s, 0), 0.0)
        s *= 2
    return x


def _chunk_last(x):
    return jnp.concatenate(
        [jnp.broadcast_to(x[(c + 1) * CHUNK - 1:(c + 1) * CHUNK], (CHUNK, x.shape[1]))
         for c in range(x.shape[0] // CHUNK)], axis=0)


def _row_form(col):
    n = col.shape[0]
    return jnp.broadcast_to(col, (n, n)).T


def _block_masks(n):
    ri = _iota((n, n), 0)
    ci = _iota((n, n), 1)
    same = (ri // CHUNK) == (ci // CHUNK)
    return ri, ci, same & (ri >= ci), same & (ri > ci)


def _heads(x, width, count):
    return [x[:, h * width:(h + 1) * width] for h in range(count)]


def _rmsnorm_kernel(x_ref, g_ref, o_ref):
    x = x_ref[...]
    ms = jnp.mean(x * x, axis=-1, keepdims=True)
    o_ref[...] = (x * lax.rsqrt(ms + NORM_EPS) * g_ref[...]).astype(o_ref.dtype)


def _rmsnorm(x2d, g, out_dtype, tm=512):
    t, d = x2d.shape
    return pl.pallas_call(
        _rmsnorm_kernel,
        grid=(t // tm,),
        in_specs=[pl.BlockSpec((tm, d), lambda i: (i, 0)),
                  pl.BlockSpec((1, d), lambda i: (0, 0))],
        out_specs=pl.BlockSpec((tm, d), lambda i: (i, 0)),
        out_shape=jax.ShapeDtypeStruct((t, d), out_dtype),
        compiler_params=_cparams(("parallel",)),
        name="rmsnorm",
    )(x2d, g.reshape(1, d))


def _mm_kernel(a_ref, w_ref, o_ref):
    o_ref[...] = _dot(a_ref[...], w_ref[...]).astype(o_ref.dtype)


def _mm_res_kernel(a_ref, w_ref, r_ref, o_ref):
    o_ref[...] = r_ref[...] + _dot(a_ref[...], w_ref[...])


def _matmul(a, w, layer, out_dtype=F32, residual=None, tm=512, tn=512, name="matmul"):
    m, k = a.shape
    n = w.shape[2]
    tm, tn = min(tm, m), min(tn, n)
    in_specs = [pl.BlockSpec((tm, k), lambda i, j: (i, 0)),
                pl.BlockSpec((None, k, tn), lambda i, j: (layer, 0, j))]
    args = [a, w]
    body = _mm_kernel
    if residual is not None:
        in_specs.append(pl.BlockSpec((tm, tn), lambda i, j: (i, j)))
        args.append(residual)
        body = _mm_res_kernel
    return pl.pallas_call(
        body,
        grid=(m // tm, n // tn),
        in_specs=in_specs,
        out_specs=pl.BlockSpec((tm, tn), lambda i, j: (i, j)),
        out_shape=jax.ShapeDtypeStruct((m, n), out_dtype),
        compiler_params=_cparams(("parallel", "parallel")),
        name=name,
    )(*args)


def _lru_kernel(x_ref, gate_ref, cw_ref, cb_ref, wa_ref, ba_ref, wx_ref, bx_ref, lam_ref,
                o_ref, tail_ref, h_ref):
    @pl.when(pl.program_id(1) == 0)
    def _():
        tail_ref[...] = jnp.zeros_like(tail_ref)
        h_ref[...] = jnp.zeros_like(h_ref)

    x = x_ref[...]
    n = x.shape[0]
    u = _causal_conv(x, tail_ref[...], cw_ref[...]) + cb_ref[...]
    tail_ref[...] = x[n - SUBLANES:, :]
    ubs = _heads(u, LRU_BLOCK, LRU_BLOCKS)
    r = jnp.concatenate([_dot(ub, wa_ref[b]) for b, ub in enumerate(ubs)], axis=1)
    i = jnp.concatenate([_dot(ub, wx_ref[b]) for b, ub in enumerate(ubs)], axis=1)
    r = _sigmoid(r + ba_ref[...])
    i = _sigmoid(i + bx_ref[...])
    log_a = (-LRU_C * r) * _softplus(-lam_ref[...])
    a = jnp.exp(log_a)
    xin = jnp.sqrt(-jnp.tanh(log_a) * (a * a + 1.0)) * (i * u)
    row = _iota(a.shape, 0)
    s = 1
    while s < n:
        keep = row >= s
        a_sh = jnp.where(keep, pltpu.roll(a, s, 0), 1.0)
        x_sh = jnp.where(keep, pltpu.roll(xin, s, 0), 0.0)
        xin = a * x_sh + xin
        a = a * a_sh
        s *= 2
    h = xin + a * h_ref[0:1, :]
    h_ref[...] = jnp.broadcast_to(h[n - 1:n, :], h_ref.shape)
    o_ref[...] = (h * _gelu_tanh(gate_ref[...])).astype(o_ref.dtype)


def _lru(proj, batch, seq, layer, cw, cb, wa, ba, wx, bx, lam):
    t = proj.shape[0]
    nsb = seq // SEQ_BLOCK
    row = lambda b, s: b * nsb + s
    vec = lambda a: a.reshape(a.shape[0], 1, BW)
    vspec = pl.BlockSpec((None, 1, BW), lambda b, s: (layer, 0, 0))
    wspec = pl.BlockSpec((None, LRU_BLOCKS, LRU_BLOCK, LRU_BLOCK), lambda b, s: (layer, 0, 0, 0))
    return pl.pallas_call(
        _lru_kernel,
        grid=(batch, nsb),
        in_specs=[pl.BlockSpec((SEQ_BLOCK, BW), lambda b, s: (row(b, s), OFF_AX // BW)),
                  pl.BlockSpec((SEQ_BLOCK, BW), lambda b, s: (row(b, s), OFF_AG // BW)),
                  pl.BlockSpec((None, CONV_WIDTH, BW), lambda b, s: (layer, 0, 0)),
                  vspec, wspec, vspec, wspec, vspec, vspec],
        out_specs=pl.BlockSpec((SEQ_BLOCK, BW), lambda b, s: (row(b, s), 0)),
        out_shape=jax.ShapeDtypeStruct((t, BW), BF16),
        scratch_shapes=[pltpu.VMEM((SUBLANES, BW), F32), pltpu.VMEM((SUBLANES, BW), F32)],
        compiler_params=_cparams(("parallel", "arbitrary")),
        name="rg_lru",
    )(proj, proj, cw, vec(cb), wa, vec(ba), wx, vec(bx), vec(lam))


def _gla_kernel(q_ref, k_ref, v_ref, g_ref, sm_ref, wd_ref, bd_ref, ng_ref, o_ref, st_ref):
    @pl.when(pl.program_id(1) == 0)
    def _():
        st_ref[...] = jnp.zeros_like(st_ref)

    nh, dk, dv = GLA_HEADS, GLA_DK, GLA_DV
    _, _, tril, _ = _block_masks(SEQ_BLOCK)
    log_alpha = _log_sigmoid(_dot(sm_ref[...], wd_ref[...], HI) + bd_ref[...]) / GLA_TAU
    bcum = _chunk_cumsum(log_alpha)
    b_last = _chunk_last(bcum)
    k = k_ref[...]
    q_decs = _heads(q_ref[...] * (GLA_DK ** -0.5) * jnp.exp(bcum), dk, nh)
    k_decs = _heads(k * jnp.exp(-bcum), dk, nh)
    k_tails = _heads(k * jnp.exp(b_last - bcum), dk, nh)
    b_lasts = _heads(b_last, dk, nh)
    vs = _heads(v_ref[...], dv, nh)
    attns = [jnp.where(tril, _dot_nt(qd, kd), 0.0) for qd, kd in zip(q_decs, k_decs)]
    o_intras = [_dot(a, v) for a, v in zip(attns, vs)]
    sts = [st_ref[h] for h in range(nh)]
    for c in range(N_CHUNKS):
        sl = slice(c * CHUNK, (c + 1) * CHUNK)
        outs = [oi[sl] + _dot_nt(qd[sl], st) for oi, qd, st in zip(o_intras, q_decs, sts)]
        kvs = [_dot_tn(v[sl], kt[sl]) for v, kt in zip(vs, k_tails)]
        sts = [st * jnp.exp(bl[c * CHUNK:c * CHUNK + 1, :]) + kv for st, bl, kv in zip(sts, b_lasts, kvs)]
        for h, o in enumerate(outs):
            cols = slice(h * dv, (h + 1) * dv)
            o = o * lax.rsqrt(jnp.mean(o * o, axis=-1, keepdims=True) + NORM_EPS) * ng_ref[...]
            o_ref[sl, cols] = (o * _silu(g_ref[sl, cols])).astype(o_ref.dtype)
    for h in range(nh):
        st_ref[h] = sts[h]


def _gla(proj, small, batch, seq, layer, w_decay, b_decay, norm_g):
    t = proj.shape[0]
    nsb = seq // SEQ_BLOCK
    wq = GLA_HEADS * GLA_DK
    row = lambda b, s: b * nsb + s
    depth = w_decay.shape[0]
    wd = jnp.zeros((depth, LANES, wq), F32).at[:, LANE_DECAY:LANE_DECAY + GLA_RANK].set(w_decay)
    return pl.pallas_call(
        _gla_kernel,
        grid=(batch, nsb),
        in_specs=[pl.BlockSpec((SEQ_BLOCK, wq), lambda b, s: (row(b, s), OFF_BQ // wq)),
                  pl.BlockSpec((SEQ_BLOCK, wq), lambda b, s: (row(b, s), OFF_BK // wq)),
                  pl.BlockSpec((SEQ_BLOCK, BW), lambda b, s: (row(b, s), OFF_BV // BW)),
                  pl.BlockSpec((SEQ_BLOCK, BW), lambda b, s: (row(b, s), OFF_BG // BW)),
                  pl.BlockSpec((SEQ_BLOCK, LANES), lambda b, s: (row(b, s), 0)),
                  pl.BlockSpec((None, LANES, wq), lambda b, s: (layer, 0, 0)),
                  pl.BlockSpec((None, 1, wq), lambda b, s: (layer, 0, 0)),
                  pl.BlockSpec((None, 1, GLA_DV), lambda b, s: (layer, 0, 0))],
        out_specs=pl.BlockSpec((SEQ_BLOCK, BW), lambda b, s: (row(b, s), 0)),
        out_shape=jax.ShapeDtypeStruct((t, BW), BF16),
        scratch_shapes=[pltpu.VMEM((GLA_HEADS, GLA_DV, GLA_DK), F32)],
        compiler_params=_cparams(("parallel", "arbitrary")),
        name="gla",
    )(proj, proj, proj, proj, small, wd, b_decay.reshape(depth, 1, wq), norm_g.reshape(depth, 1, GLA_DV))


def _unit_lower_inverse(lows, ri, ci):
    eye = (ri == ci).astype(F32)
    same16 = (ri // 16) == (ci // 16)
    same32 = (ri // 32) == (ci // 32)
    d1 = [jnp.where(same16, low, 0.0) for low in lows]
    d2 = [_dot(a, a) for a in d1]
    m = [eye - a for a in d1]
    d4 = [_dot(a, a) for a in d2]
    m = [a + _dot(a, b) for a, b in zip(m, d2)]
    d8 = [_dot(a, a) for a in d4]
    m = [a + _dot(a, b) for a, b in zip(m, d4)]
    m = [a + _dot(a, b) for a, b in zip(m, d8)]
    c1 = [jnp.where(same32 & ~same16, low, 0.0) for low in lows]
    t = [_dot(a, b) for a, b in zip(m, c1)]
    m = [a - _dot(b, a) for a, b in zip(m, t)]
    c2 = [jnp.where(same32, 0.0, low) for low in lows]
    t = [_dot(a, b) for a, b in zip(m, c2)]
    m = [a - _dot(b, a) for a, b in zip(m, t)]
    return m


def _gdn_kernel(q_ref, k_ref, v_ref, g_ref, sm_ref, cwq_ref, cwk_ref, cwv_ref, alog_ref, dtb_ref,
                ng_ref, o_ref, st_ref, tq_ref, tk_ref, tv_ref):
    @pl.when(pl.program_id(1) == 0)
    def _():
        st_ref[...] = jnp.zeros_like(st_ref)
        tq_ref[...] = jnp.zeros_like(tq_ref)
        tk_ref[...] = jnp.zeros_like(tk_ref)
        tv_ref[...] = jnp.zeros_like(tv_ref)

    n = SEQ_BLOCK
    nh, hd = GDN_HEADS, GDN_HEAD_DIM
    q_raw, k_raw, v_raw = q_ref[...], k_ref[...], v_ref[...]
    qs = _heads(_silu(_causal_conv(q_raw, tq_ref[...], cwq_ref[...])), hd, nh)
    ks = _heads(_silu(_causal_conv(k_raw, tk_ref[...], cwk_ref[...])), hd, nh)
    vs = _heads(_silu(_causal_conv(v_raw, tv_ref[...], cwv_ref[...])), hd, nh)
    tq_ref[...] = q_raw[n - SUBLANES:, :]
    tk_ref[...] = k_raw[n - SUBLANES:, :]
    tv_ref[...] = v_raw[n - SUBLANES:, :]
    sm = sm_ref[...]
    beta_lanes = _sigmoid(sm)
    gcum_lanes = _chunk_cumsum(-jnp.exp(alog_ref[...]) * _softplus(sm + dtb_ref[...]))
    ri, ci, tril, strict = _block_masks(n)

    qs = [q * lax.rsqrt(jnp.sum(q * q, axis=-1, keepdims=True) + NORM_EPS) * (hd ** -0.5) for q in qs]
    ks = [k * lax.rsqrt(jnp.sum(k * k, axis=-1, keepdims=True) + NORM_EPS) for k in ks]
    betas = [beta_lanes[:, LANE_BETA + h:LANE_BETA + h + 1] for h in range(nh)]
    gcums = [gcum_lanes[:, LANE_ALPHA + h:LANE_ALPHA + h + 1] for h in range(nh)]
    g_lasts = [_chunk_last(g) for g in gcums]
    gbs = [jnp.broadcast_to(g, (n, n)) for g in gcums]
    decays = [jnp.where(tril, jnp.exp(jnp.where(tril, gb - gb.T, 0.0)), 0.0) for gb in gbs]
    kks = [_dot_nt(k, k) for k in ks]
    lows = [jnp.where(strict, b * kk * d, 0.0) for b, kk, d in zip(betas, kks, decays)]
    qks = [_dot_nt(q, k) * d for q, k, d in zip(qs, ks, decays)]
    minvs = _unit_lower_inverse(lows, ri, ci)
    e_gs = [jnp.exp(g) for g in gcums]
    uws = [_dot(mi, jnp.concatenate([v * b, k * (b * e)], axis=1))
           for mi, v, k, b, e in zip(minvs, vs, ks, betas, e_gs)]
    ps = [_dot(qk, uw) for qk, uw in zip(qks, uws)]
    q_effs = [q * e - p[:, hd:] for q, e, p in zip(qs, e_gs, ps)]
    k_tails = [k * jnp.exp(gl - g) for k, gl, g in zip(ks, g_lasts, gcums)]
    sts = [st_ref[h] for h in range(nh)]
    for c in range(N_CHUNKS):
        sl = slice(c * CHUNK, (c + 1) * CHUNK)
        outs = [p[sl, :hd] + _dot(qe[sl], st) for p, qe, st in zip(ps, q_effs, sts)]
        abs_ = [_dot_tn(kt[sl], uw[sl]) for kt, uw in zip(k_tails, uws)]
        sts = [st * jnp.exp(gl[c * CHUNK:c * CHUNK + 1, :]) + ab[:, :hd] - _dot(ab[:, hd:], st)
               for gl, st, ab in zip(g_lasts, sts, abs_)]
        for h, o in enumerate(outs):
            cols = slice(h * hd, (h + 1) * hd)
            o = o * lax.rsqrt(jnp.mean(o * o, axis=-1, keepdims=True) + NORM_EPS) * ng_ref[...]
            o_ref[sl, cols] = (o * _silu(g_ref[sl, cols])).astype(o_ref.dtype)
    for h in range(nh):
        st_ref[h] = sts[h]


def _gdn(proj, small, batch, seq, layer, conv_w, a_log, dt_bias, norm_g):
    t = proj.shape[0]
    nsb = seq // SEQ_BLOCK
    hd = GDN_HEAD_DIM
    row = lambda b, s: b * nsb + s
    depth = a_log.shape[0]
    lane_row = lambda a: jnp.zeros((depth, 1, LANES), F32).at[:, 0, LANE_ALPHA:LANE_ALPHA + GDN_HEADS].set(a)
    lspec = pl.BlockSpec((None, 1, LANES), lambda b, s: (layer, 0, 0))
    return pl.pallas_call(
        _gdn_kernel,
        grid=(batch, nsb),
        in_specs=[pl.BlockSpec((SEQ_BLOCK, BW), lambda b, s: (row(b, s), OFF_CQ // BW)),
                  pl.BlockSpec((SEQ_BLOCK, BW), lambda b, s: (row(b, s), OFF_CK // BW)),
                  pl.BlockSpec((SEQ_BLOCK, BW), lambda b, s: (row(b, s), OFF_CV // BW)),
                  pl.BlockSpec((SEQ_BLOCK, BW), lambda b, s: (row(b, s), OFF_CG // BW)),
                  pl.BlockSpec((SEQ_BLOCK, LANES), lambda b, s: (row(b, s), 0)),
                  pl.BlockSpec((None, CONV_WIDTH, BW), lambda b, s: (layer, 0, 0)),
                  pl.BlockSpec((None, CONV_WIDTH, BW), lambda b, s: (layer, 0, 1)),
                  pl.BlockSpec((None, CONV_WIDTH, BW), lambda b, s: (layer, 0, 2)),
                  lspec, lspec,
                  pl.BlockSpec((None, 1, hd), lambda b, s: (layer, 0, 0))],
        out_specs=pl.BlockSpec((SEQ_BLOCK, BW), lambda b, s: (row(b, s), 0)),
        out_shape=jax.ShapeDtypeStruct((t, BW), BF16),
        scratch_shapes=[pltpu.VMEM((GDN_HEADS, hd, hd), F32)] + [pltpu.VMEM((SUBLANES, BW), F32)] * 3,
        compiler_params=_cparams(("parallel", "arbitrary")),
        name="gdn",
    )(proj, proj, proj, proj, small, conv_w, conv_w, conv_w, lane_row(a_log), lane_row(dt_bias),
      norm_g.reshape(depth, 1, hd))


def _mlstm_kernel(q_ref, k_ref, v_ref, og_ref, sm_ref, bi_ref, bf_ref, ng_ref, o_ref,
                  c_ref, n_ref, m_ref):
    @pl.when(pl.program_id(1) == 0)
    def _():
        c_ref[...] = jnp.zeros_like(c_ref)
        n_ref[...] = jnp.zeros_like(n_ref)
        m_ref[...] = jnp.zeros_like(m_ref)

    n = SEQ_BLOCK
    nh, dqk, dv = MLSTM_HEADS, MLSTM_DQK, MLSTM_DV
    hr = range(nh)
    _, _, tril, _ = _block_masks(n)
    qs = _heads(q_ref[...], dqk, nh)
    ks = _heads(k_ref[...] * (MLSTM_DQK ** -0.5), dqk, nh)
    vs = _heads(v_ref[...], dv, nh)
    sm = sm_ref[...]
    ig_lanes = sm + bi_ref[...]
    bcum_lanes = _chunk_cumsum(_log_sigmoid(sm + bf_ref[...]))
    igs = [ig_lanes[:, LANE_I + h:LANE_I + h + 1] for h in hr]
    bcums = [bcum_lanes[:, LANE_F + h:LANE_F + h + 1] for h in hr]
    b_lasts = [_chunk_last(b) for b in bcums]
    logds = [jnp.where(tril, b + _row_form(i - b), -jnp.inf) for b, i in zip(bcums, igs)]
    m_intras = [jnp.max(ld, axis=1, keepdims=True) for ld in logds]
    tails = [bl - b + i for bl, b, i in zip(b_lasts, bcums, igs)]
    spread = lambda xs: jnp.concatenate([jnp.broadcast_to(x, (CHUNK, 1)) for x in xs], axis=0)
    m_prev_rows, a_max_rows, s_olds, s_news = [], [], [], []
    for h in hr:
        m_st = m_ref[h, 0:1, 0:1]
        m_prev, a_max, s_old, s_new = [], [], [], []
        for c in range(N_CHUNKS):
            a_c = jnp.max(tails[h][c * CHUNK:(c + 1) * CHUNK], axis=0, keepdims=True)
            bl_c = bcums[h][(c + 1) * CHUNK - 1:(c + 1) * CHUNK, :]
            m_new = jnp.maximum(bl_c + m_st, a_c)
            m_prev.append(m_st)
            a_max.append(a_c)
            s_old.append(jnp.exp(bl_c + m_st - m_new))
            s_new.append(jnp.exp(a_c - m_new))
            m_st = m_new
        m_ref[h] = jnp.broadcast_to(m_st, m_ref.shape[1:])
        m_prev_rows.append(spread(m_prev))
        a_max_rows.append(spread(a_max))
        s_olds.append(s_old)
        s_news.append(s_new)
    k_ws = [k * jnp.exp(t - a) for k, t, a in zip(ks, tails, a_max_rows)]
    m_inters = [b + mp for b, mp in zip(bcums, m_prev_rows)]
    m_is = [jnp.maximum(mi, mx) for mi, mx in zip(m_inters, m_intras)]
    s_inters = [jnp.exp(mi - m) for mi, m in zip(m_inters, m_is)]
    scs = [_dot_nt(q, k) * jnp.exp(ld - m) for q, k, ld, m in zip(qs, ks, logds, m_is)]
    num_intras = [_dot(sc, v) for sc, v in zip(scs, vs)]
    den_intras = [jnp.sum(sc, axis=1, keepdims=True) for sc in scs]
    floors = [jnp.exp(-m) for m in m_is]
    c_sts = [c_ref[h] for h in hr]
    n_sts = [n_ref[h, 0:1, :] for h in hr]
    for c in range(N_CHUNKS):
        sl = slice(c * CHUNK, (c + 1) * CHUNK)
        nums = [ni[sl] + si[sl] * _dot(q[sl], cs) for ni, si, q, cs in zip(num_intras, s_inters, qs, c_sts)]
        dens = [di[sl] + si[sl] * jnp.sum(q[sl] * ns, axis=1, keepdims=True)
                for di, si, q, ns in zip(den_intras, s_inters, qs, n_sts)]
        kvs = [_dot_tn(kw[sl], v[sl]) for kw, v in zip(k_ws, vs)]
        c_sts = [cs * so[c] + kv * sn[c] for cs, so, kv, sn in zip(c_sts, s_olds, kvs, s_news)]
        n_sts = [ns * so[c] + jnp.sum(kw[sl], axis=0, keepdims=True) * sn[c]
                 for ns, so, kw, sn in zip(n_sts, s_olds, k_ws, s_news)]
        for h in hr:
            cols = slice(h * dv, (h + 1) * dv)
            hh = nums[h] / jnp.maximum(jnp.abs(dens[h]), floors[h][sl])
            hh = hh * lax.rsqrt(jnp.mean(hh * hh, axis=-1, keepdims=True) + NORM_EPS) * ng_ref[...]
            o_ref[sl, cols] = (hh * _sigmoid(og_ref[sl, cols])).astype(o_ref.dtype)
    for h in hr:
        c_ref[h] = c_sts[h]
        n_ref[h] = jnp.broadcast_to(n_sts[h], n_ref.shape[1:])


def _mlstm(proj, small, batch, seq, layer, b_i, b_f, norm_g):
    t = proj.shape[0]
    nsb = seq // SEQ_BLOCK
    nh, dqk, dv = MLSTM_HEADS, MLSTM_DQK, MLSTM_DV
    wq = nh * dqk
    row = lambda b, s: b * nsb + s
    depth = b_i.shape[0]
    lane_row = lambda a, off: jnp.zeros((depth, 1, LANES), F32).at[:, 0, off:off + nh].set(a)
    lspec = pl.BlockSpec((None, 1, LANES), lambda b, s: (layer, 0, 0))
    return pl.pallas_call(
        _mlstm_kernel,
        grid=(batch, nsb),
        in_specs=[pl.BlockSpec((SEQ_BLOCK, wq), lambda b, s: (row(b, s), OFF_DQ // wq)),
                  pl.BlockSpec((SEQ_BLOCK, wq), lambda b, s: (row(b, s), OFF_DK // wq)),
                  pl.BlockSpec((SEQ_BLOCK, BW), lambda b, s: (row(b, s), OFF_DV // BW)),
                  pl.BlockSpec((SEQ_BLOCK, BW), lambda b, s: (row(b, s), OFF_DO // BW)),
                  pl.BlockSpec((SEQ_BLOCK, LANES), lambda b, s: (row(b, s), 0)),
                  lspec, lspec,
                  pl.BlockSpec((None, 1, dv), lambda b, s: (layer, 0, 0))],
        out_specs=pl.BlockSpec((SEQ_BLOCK, BW), lambda b, s: (row(b, s), 0)),
        out_shape=jax.ShapeDtypeStruct((t, BW), BF16),
        scratch_shapes=[pltpu.VMEM((nh, dqk, dv), F32), pltpu.VMEM((nh, SUBLANES, dqk), F32),
                        pltpu.VMEM((nh, SUBLANES, LANES), F32)],
        compiler_params=_cparams(("parallel", "arbitrary")),
        name="mlstm",
    )(proj, proj, proj, proj, small, lane_row(b_i, LANE_I), lane_row(b_f, LANE_F),
      norm_g.reshape(depth, 1, dv))


def _merge_kernel(xn_ref, ya_ref, yb_ref, yc_ref, yd_ref, wg_ref, bg_ref, wb_ref, o_ref):
    xn = xn_ref[...]
    acc = None
    for n, y_ref in enumerate((ya_ref, yb_ref, yc_ref, yd_ref)):
        gate = _sigmoid(_dot(xn, wg_ref[n]) + bg_ref[n])
        term = gate * _dot(y_ref[...], wb_ref[n])
        acc = term if acc is None else acc + term
    o_ref[...] = acc.astype(o_ref.dtype)


def _merge(xn, ys, layer, wg, bg, wb, tm=512, tn=256):
    t, d = xn.shape
    tm = min(tm, t)
    depth, nb = wg.shape[:2]
    yspec = pl.BlockSpec((tm, BW), lambda i, j: (i, 0))
    return pl.pallas_call(
        _merge_kernel,
        grid=(t // tm, d // tn),
        in_specs=[pl.BlockSpec((tm, d), lambda i, j: (i, 0)), yspec, yspec, yspec, yspec,
                  pl.BlockSpec((None, nb, d, tn), lambda i, j: (layer, 0, 0, j)),
                  pl.BlockSpec((None, nb, 1, tn), lambda i, j: (layer, 0, 0, j)),
                  pl.BlockSpec((None, nb, BW, tn), lambda i, j: (layer, 0, 0, j))],
        out_specs=pl.BlockSpec((tm, tn), lambda i, j: (i, j)),
        out_shape=jax.ShapeDtypeStruct((t, d), BF16),
        compiler_params=_cparams(("parallel", "parallel")),
        name="gated_merge",
    )(xn, *ys, wg, bg.reshape(depth, nb, 1, d), wb)


def _router_kernel(x_ref, g_ref, wr_ref, br_ref, h_ref, info_ref, idx_ref, cnt_ref, carry_ref):
    @pl.when(pl.program_id(0) == 0)
    def _():
        carry_ref[...] = jnp.zeros_like(carry_ref)

    x = x_ref[...]
    tm = x.shape[0]
    h = x * lax.rsqrt(jnp.mean(x * x, axis=-1, keepdims=True) + NORM_EPS) * g_ref[...]
    h_ref[...] = h
    logits = _dot(h, wr_ref[...], HI) + br_ref[...]
    lane = _iota((tm, LANES), 1).astype(F32)
    big = float(LANES)
    gl = jnp.where(lane < N_GROUPS, logits, -jnp.inf)
    gmax = jnp.max(gl, axis=1, keepdims=True)
    grp = jnp.min(jnp.where(gl == gmax, lane, big), axis=1, keepdims=True)
    p_grp = 1.0 / jnp.sum(jnp.exp(gl - gmax), axis=1, keepdims=True)
    lo = N_GROUPS + grp * EXPERTS_PER_GROUP
    el = jnp.where((lane >= lo) & (lane < lo + EXPERTS_PER_GROUP), logits, -jnp.inf)
    v0 = jnp.max(el, axis=1, keepdims=True)
    i0 = jnp.min(jnp.where(el == v0, lane, big), axis=1, keepdims=True)
    el = jnp.where(lane == i0, -jnp.inf, el)
    v1 = jnp.max(el, axis=1, keepdims=True)
    i1 = jnp.min(jnp.where(el == v1, lane, big), axis=1, keepdims=True)
    e = jnp.exp(v1 - v0)
    w0 = p_grp / (1.0 + e)
    w1 = p_grp * e / (1.0 + e)
    e0 = i0 - N_GROUPS
    e1 = i1 - N_GROUPS
    oh0 = (lane == e0).astype(F32)
    oh1 = (lane == e1).astype(F32)
    both = oh0 + oh1
    strict = (_iota((tm, tm), 0) > _iota((tm, tm), 1)).astype(BF16)
    before = carry_ref[0:1, :] + _dot(strict, both.astype(BF16))
    r0 = jnp.sum(oh0 * before, axis=1, keepdims=True)
    r1 = jnp.sum(oh1 * before, axis=1, keepdims=True)
    total = carry_ref[0:1, :] + jnp.sum(both, axis=0, keepdims=True)
    carry_ref[...] = jnp.broadcast_to(total, carry_ref.shape)
    cnt_ref[...] = jnp.broadcast_to(total, cnt_ref.shape)
    info_ref[...] = jnp.where(lane == 0.0, w0, jnp.where(lane == 1.0, w1, 0.0))
    idx = jnp.where(lane == 0.0, e0, jnp.where(lane == 1.0, e1,
                                               jnp.where(lane == 2.0, r0, jnp.where(lane == 3.0, r1, 0.0))))
    idx_ref[...] = idx.astype(jnp.int32)


def _router(x2d, layer, g, w_grp, b_grp, w_exp, b_exp, tm=256):
    t, d = x2d.shape
    tm = min(tm, t)
    depth = g.shape[0]
    wr = (jnp.zeros((depth, d, LANES), F32).at[:, :, :N_GROUPS].set(w_grp)
          .at[:, :, N_GROUPS:N_GROUPS + N_EXPERTS].set(w_exp))
    br = (jnp.zeros((depth, 1, LANES), F32).at[:, 0, :N_GROUPS].set(b_grp)
          .at[:, 0, N_GROUPS:N_GROUPS + N_EXPERTS].set(b_exp))
    return pl.pallas_call(
        _router_kernel,
        grid=(t // tm,),
        in_specs=[pl.BlockSpec((tm, d), lambda i: (i, 0)),
                  pl.BlockSpec((None, 1, d), lambda i: (layer, 0, 0)),
                  pl.BlockSpec((None, d, LANES), lambda i: (layer, 0, 0)),
                  pl.BlockSpec((None, 1, LANES), lambda i: (layer, 0, 0))],
        out_specs=[pl.BlockSpec((tm, d), lambda i: (i, 0)),
                   pl.BlockSpec((tm, LANES), lambda i: (i, 0)),
                   pl.BlockSpec((tm, LANES), lambda i: (i, 0)),
                   pl.BlockSpec((SUBLANES, LANES), lambda i: (0, 0))],
        out_shape=[jax.ShapeDtypeStruct((t, d), F32),
                   jax.ShapeDtypeStruct((t, LANES), F32),
                   jax.ShapeDtypeStruct((t, LANES), jnp.int32),
                   jax.ShapeDtypeStruct((SUBLANES, LANES), F32)],
        scratch_shapes=[pltpu.VMEM((SUBLANES, LANES), F32)],
        compiler_params=_cparams(("arbitrary",)),
        name="router",
    )(x2d, g.reshape(depth, 1, d), wr, br)


def _dispatch_kernel(pos_ref, zrow_ref, h_ref, xs_ref, zero_ref, sem, zsem):
    i = pl.program_id(0)
    tm = h_ref.shape[0]
    zt = zero_ref.shape[0]

    @pl.when(i == 0)
    def _():
        zero_ref[...] = jnp.zeros_like(zero_ref)

        def zero_copy(e):
            start = pl.multiple_of(jnp.maximum(zrow_ref[e], 0), zt)
            return pltpu.make_async_copy(zero_ref, xs_ref.at[pl.ds(start, zt)], zsem)

        def zissue(e, carry):
            @pl.when(zrow_ref[e] >= 0)
            def _():
                zero_copy(e).start()
            return carry

        def zdrain(e, carry):
            @pl.when(zrow_ref[e] >= 0)
            def _():
                zero_copy(e).wait()
            return carry

        lax.fori_loop(0, zrow_ref.shape[0], zissue, 0)
        lax.fori_loop(0, zrow_ref.shape[0], zdrain, 0)

    def row_copy(r, p):
        return pltpu.make_async_copy(h_ref.at[pl.ds(r, 1)], xs_ref.at[pl.ds(p, 1)], sem)

    def issue(r, carry):
        t = i * tm + r
        for k in range(2):
            row_copy(r, pos_ref[2 * t + k]).start()
        return carry

    def drain(r, carry):
        for k in range(2):
            row_copy(r, 0).wait()
        return carry

    lax.fori_loop(0, tm, issue, 0)
    lax.fori_loop(0, tm, drain, 0)


def _dispatch(pos, zrow, h, n_rows, tm=ROW_TILE):
    t, d = h.shape
    tm = min(tm, t)
    return pl.pallas_call(
        _dispatch_kernel,
        grid_spec=pltpu.PrefetchScalarGridSpec(
            num_scalar_prefetch=2,
            grid=(t // tm,),
            in_specs=[pl.BlockSpec((tm, d), lambda i, pos, zrow: (i, 0))],
            out_specs=pl.BlockSpec(memory_space=pl.ANY),
            scratch_shapes=[pltpu.VMEM((EXPERT_TILE, d), F32), pltpu.SemaphoreType.DMA(()),
                            pltpu.SemaphoreType.DMA(())],
        ),
        out_shape=jax.ShapeDtypeStruct((n_rows, d), F32),
        compiler_params=_cparams(("arbitrary",)),
        name="dispatch",
    )(pos, zrow, h)


def _combine_kernel(pos_ref, x_ref, info_ref, ys_ref, o_ref, buf0_ref, buf1_ref, sem):
    i = pl.program_id(0)
    tm = x_ref.shape[0]
    bufs = (buf0_ref, buf1_ref)

    def row_copy(r, k, p):
        return pltpu.make_async_copy(ys_ref.at[pl.ds(p, 1)], bufs[k].at[pl.ds(r, 1)], sem)

    def issue(r, carry):
        t = i * tm + r
        for k in range(2):
            row_copy(r, k, pos_ref[2 * t + k]).start()
        return carry

    def drain(r, carry):
        for k in range(2):
            row_copy(r, k, 0).wait()
        return carry

    lax.fori_loop(0, tm, issue, 0)
    lax.fori_loop(0, tm, drain, 0)
    info = info_ref[...]
    o_ref[...] = x_ref[...] + info[:, 0:1] * buf0_ref[...] + info[:, 1:2] * buf1_ref[...]


def _combine(pos, x2d, info, ys, tm=ROW_TILE):
    t, d = x2d.shape
    tm = min(tm, t)
    return pl.pallas_call(
        _combine_kernel,
        grid_spec=pltpu.PrefetchScalarGridSpec(
            num_scalar_prefetch=1,
            grid=(t // tm,),
            in_specs=[pl.BlockSpec((tm, d), lambda i, pos: (i, 0)),
                      pl.BlockSpec((tm, LANES), lambda i, pos: (i, 0)),
                      pl.BlockSpec(memory_space=pl.ANY)],
            out_specs=pl.BlockSpec((tm, d), lambda i, pos: (i, 0)),
            scratch_shapes=[pltpu.VMEM((tm, d), F32), pltpu.VMEM((tm, d), F32),
                            pltpu.SemaphoreType.DMA(())],
        ),
        out_shape=jax.ShapeDtypeStruct((t, d), F32),
        compiler_params=_cparams(("arbitrary",)),
        name="combine",
    )(pos, x2d, info, ys)


def _expert_kernel(te_ref, nu_ref, xs_ref, wg_ref, wu_ref, wd_ref, ys_ref):
    del te_ref
    i = pl.program_id(0)

    @pl.when(i < nu_ref[0])
    def _():
        x = xs_ref[...].astype(BF16)
        a = _dot(x, wg_ref[...])
        u = _dot(x, wu_ref[...])
        act = (_silu(a) * u).astype(BF16)
        ys_ref[...] = _dot(act, wd_ref[...])

    @pl.when(i >= nu_ref[0])
    def _():
        ys_ref[...] = jnp.zeros_like(ys_ref)


def _experts(tile_expert, n_used, xs, wg, wu, wd):
    p, d = xs.shape
    f = wg.shape[2]
    tm = EXPERT_TILE
    src_tile = lambda i, te, nu: (jnp.minimum(i, nu[0] - 1), 0)
    return pl.pallas_call(
        _expert_kernel,
        grid_spec=pltpu.PrefetchScalarGridSpec(
            num_scalar_prefetch=2,
            grid=(p // tm,),
            in_specs=[pl.BlockSpec((tm, d), src_tile),
                      pl.BlockSpec((None, d, f), lambda i, te, nu: (te[i], 0, 0)),
                      pl.BlockSpec((None, d, f), lambda i, te, nu: (te[i], 0, 0)),
                      pl.BlockSpec((None, f, d), lambda i, te, nu: (te[i], 0, 0))],
            out_specs=pl.BlockSpec((tm, d), lambda i, te, nu: (i, 0)),
        ),
        out_shape=jax.ShapeDtypeStruct((p, d), F32),
        compiler_params=_cparams(("arbitrary",)),
        name="experts",
    )(tile_expert, n_used, xs, wg, wu, wd)


def _hier_moe(x2d, layer, ffn_g, w_grp, b_grp, w_exp, b_exp, wg, wu, wd):
    t, d = x2d.shape
    tm = EXPERT_TILE
    n_rows = 2 * t + N_EXPERTS * tm
    h, info, idx, cnt = _router(x2d, layer, ffn_g, w_grp, b_grp, w_exp, b_exp)
    counts = cnt[0, :N_EXPERTS].astype(jnp.int32)
    padded = ((counts + tm - 1) // tm) * tm
    ends = jnp.cumsum(padded)
    offs = ends - padded
    pos = (offs[idx[:, 0:2]] + idx[:, 2:4]).reshape(2 * t)
    n_used = (ends[-1] // tm).reshape(1).astype(jnp.int32)
    tail = n_used[0] + jnp.arange(N_EXPERTS, dtype=jnp.int32)
    zrow = jnp.concatenate([jnp.where(padded > 0, ends - tm, -1),
                            jnp.where(tail < n_rows // tm, tail * tm, -1)]).astype(jnp.int32)
    tiles = jnp.minimum(jnp.arange(n_rows // tm, dtype=jnp.int32), n_used[0] - 1)
    tile_expert = jnp.sum(((ends // tm)[None, :] <= tiles[:, None]).astype(jnp.int32), axis=1)
    tile_expert = jnp.minimum(tile_expert, N_EXPERTS - 1) + layer * N_EXPERTS
    xs = _dispatch(pos, zrow, h, n_rows)
    ys = _experts(tile_expert, n_used, xs, wg, wu, wd)
    return _combine(pos, x2d, info, ys)


def _split_w_in(w):
    c0 = 5 * BW
    c1 = c0 + GLA_RANK
    c2 = c1 + 4 * BW
    c3 = c2 + 2 * GDN_HEADS
    c4 = c3 + 3 * BW
    c5 = c4 + 2 * MLSTM_HEADS
    big = jnp.concatenate([w[..., :c0], w[..., c1:c2], w[..., c3:c4]], axis=-1)
    n_small = GLA_RANK + 2 * GDN_HEADS + 2 * MLSTM_HEADS
    small = jnp.concatenate([w[..., c0:c1], w[..., c2:c3], w[..., c4:c5],
                             jnp.zeros(w.shape[:-1] + (LANES - n_small,), w.dtype)], axis=-1)
    return big.astype(BF16), small.astype(BF16)


def kernel(x, mix_norm, w_in, lru_conv_w, lru_conv_b, lru_w_a, lru_b_a, lru_w_x, lru_b_x, lru_lambda,
           gla_w_decay, gla_b_decay, gla_norm, gdn_conv_w, gdn_a_log, gdn_dt_bias, gdn_norm,
           mlstm_b_i, mlstm_b_f, mlstm_norm, w_branch, w_merge_gate, b_merge_gate, w_out,
           ffn_norm, w_group_router, b_group_router, w_expert_router, b_expert_router,
           w_exp_gate, w_exp_up, w_exp_down, final_norm):
    batch, seq, d = x.shape
    depth = w_in.shape[0]
    x2d = x.reshape(batch * seq, d)
    w_big, w_small = _split_w_in(w_in)
    wg_merge = w_merge_gate.astype(BF16)
    wb_merge = w_branch.astype(BF16)
    w_o = w_out.astype(BF16)
    f = w_exp_gate.shape[-1]
    we_gate = w_exp_gate.astype(BF16).reshape(depth * N_EXPERTS, d, f)
    we_up = w_exp_up.astype(BF16).reshape(depth * N_EXPERTS, d, f)
    we_down = w_exp_down.astype(BF16).reshape(depth * N_EXPERTS, f, d)
    for l in range(depth):
        xn = _rmsnorm(x2d, mix_norm[l], BF16)
        proj = _matmul(xn, w_big, l, name="in_proj")
        small = _matmul(xn, w_small, l, name="in_proj_small")
        y_a = _lru(proj, batch, seq, l, lru_conv_w, lru_conv_b, lru_w_a, lru_b_a, lru_w_x, lru_b_x,
                   lru_lambda)
        y_b = _gla(proj, small, batch, seq, l, gla_w_decay, gla_b_decay, gla_norm)
        y_c = _gdn(proj, small, batch, seq, l, gdn_conv_w, gdn_a_log, gdn_dt_bias, gdn_norm)
        y_d = _mlstm(proj, small, batch, seq, l, mlstm_b_i, mlstm_b_f, mlstm_norm)
        merged = _merge(xn, (y_a, y_b, y_c, y_d), l, wg_merge, b_merge_gate, wb_merge)
        x2d = _matmul(merged, w_o, l, residual=x2d, name="out_proj")
        x2d = _hier_moe(x2d, l, ffn_norm, w_group_router, b_group_router, w_expert_router,
                        b_expert_router, we_gate, we_up, we_down)
    return _rmsnorm(x2d, final_norm, F32).reshape(batch, seq, d)
```

```python
import functools

import jax
import jax.numpy as jnp
from jax import lax
from jax.experimental import pallas as pl
from jax.experimental.pallas import tpu as pltpu

F32 = jnp.float32
BF16 = jnp.bfloat16
HI = lax.Precision.HIGHEST

D_MODEL = 4096
BW = D_MODEL // 4
CONV_WIDTH = 4
CHUNK = 64
NORM_EPS = 1e-6
LRU_BLOCKS = 8
LRU_BLOCK = BW // LRU_BLOCKS
LRU_C = 8.0
GLA_HEADS = 4
GLA_DV = BW // GLA_HEADS
GLA_DK = GLA_DV // 2
GLA_RANK = 16
GLA_TAU = 16.0
GDN_HEAD_DIM = 128
GDN_HEADS = BW // GDN_HEAD_DIM
MLSTM_HEADS = 4
MLSTM_DV = BW // MLSTM_HEADS
MLSTM_DQK = MLSTM_DV // 2
N_GROUPS = 4
EXPERTS_PER_GROUP = 8
N_EXPERTS = N_GROUPS * EXPERTS_PER_GROUP
D_EXPERT = D_MODEL // 8

LANES = 128
SUBLANES = 8
VMEM_LIMIT = 56 * 1024 * 1024

PROJ_BIG = 12 * BW
OFF_AX, OFF_AG = 0, BW
OFF_BQ, OFF_BK, OFF_BV, OFF_BG = 2 * BW, 2 * BW + 512, 3 * BW, 4 * BW
OFF_CQ, OFF_CK, OFF_CV, OFF_CG = 5 * BW, 6 * BW, 7 * BW, 8 * BW
OFF_DQ, OFF_DK, OFF_DV, OFF_DO = 9 * BW, 9 * BW + 512, 10 * BW, 11 * BW
LANE_DECAY, LANE_BETA, LANE_ALPHA, LANE_I, LANE_F = 0, 16, 24, 32, 36

SEQ_BLOCK = 256
N_CHUNKS = SEQ_BLOCK // CHUNK
EXPERT_TILE = 256
ROW_TILE = 256


def _cparams(sem):
    return pltpu.CompilerParams(dimension_semantics=sem, vmem_limit_bytes=VMEM_LIMIT)


def _softplus(x):
    return jnp.maximum(x, 0.0) + jnp.log1p(jnp.exp(-jnp.abs(x)))


def _log_sigmoid(x):
    return -_softplus(-x)


def _sigmoid(x):
    return 1.0 / (1.0 + jnp.exp(-x))


def _silu(x):
    return x * _sigmoid(x)


def _gelu_tanh(x):
    return 0.5 * x * (1.0 + jnp.tanh(0.7978845608028654 * (x + 0.044715 * (x * x * x))))


def _dot(a, b, precision=None):
    return jnp.dot(a, b, preferred_element_type=F32, precision=precision)


def _dot_nt(a, b, precision=None):
    return lax.dot_general(a, b, (((1,), (1,)), ((), ())), preferred_element_type=F32,
                           precision=precision)


def _dot_tn(a, b, precision=None):
    return lax.dot_general(a, b, (((0,), (0,)), ((), ())), preferred_element_type=F32,
                           precision=precision)


def _iota(shape, dim):
    return lax.broadcasted_iota(jnp.int32, shape, dim)


def _shift_rows(x, tail, s):
    r = pltpu.roll(x, s, 0)
    rt = pltpu.roll(tail, s, 0)
    head = jnp.where(_iota(tail.shape, 0) < s, rt, r[0:SUBLANES])
    return jnp.concatenate([head, r[SUBLANES:]], axis=0)


def _causal_conv(x, tail, w):
    y = x * w[CONV_WIDTH - 1:CONV_WIDTH, :]
    for s in range(1, CONV_WIDTH):
        y = y + _shift_rows(x, tail, s) * w[CONV_WIDTH - 1 - s:CONV_WIDTH - s, :]
    return y


def _chunk_cumsum(x):
    pos = _iota(x.shape, 0) & (CHUNK - 1)
    s = 1
    while s < CHUNK:
        x = x + jnp.where(pos >= s, pltpu.roll(x, s, 0), 0.0)
        s *= 2
    return x


def _chunk_last(x):
    return jnp.concatenate(
        [jnp.broadcast_to(x[(c + 1) * CHUNK - 1:(c + 1) * CHUNK], (CHUNK, x.shape[1]))
         for c in range(x.shape[0] // CHUNK)], axis=0)


def _row_form(col):
    n = col.shape[0]
    return jnp.broadcast_to(col, (n, n)).T


def _block_masks(n):
    ri = _iota((n, n), 0)
    ci = _iota((n, n), 1)
    same = (ri // CHUNK) == (ci // CHUNK)
    return ri, ci, same & (ri >= ci), same & (ri > ci)


def _pack_pairs(x):
    c = x.shape[1] // 2
    bits = lambda v: lax.bitcast_convert_type(v.astype(BF16).astype(F32), jnp.uint32)
    return (bits(x[:, :c]) >> 16) | bits(x[:, c:])


def _unpack_pairs(p):
    lo = lax.bitcast_convert_type(p << 16, F32)
    hi = lax.bitcast_convert_type(p & jnp.uint32(0xFFFF0000), F32)
    return lo, hi


def _heads(x, width, count):
    return [x[:, h * width:(h + 1) * width] for h in range(count)]


def _rmsnorm_kernel(x_ref, g_ref, o_ref):
    x = x_ref[...]
    ms = jnp.mean(x * x, axis=-1, keepdims=True)
    o_ref[...] = (x * lax.rsqrt(ms + NORM_EPS) * g_ref[...]).astype(o_ref.dtype)


def _rmsnorm(x2d, g, out_dtype, tm=512):
    t, d = x2d.shape
    return pl.pallas_call(
        _rmsnorm_kernel,
        grid=(t // tm,),
        in_specs=[pl.BlockSpec((tm, d), lambda i: (i, 0)),
                  pl.BlockSpec((1, d), lambda i: (0, 0))],
        out_specs=pl.BlockSpec((tm, d), lambda i: (i, 0)),
        out_shape=jax.ShapeDtypeStruct((t, d), out_dtype),
        compiler_params=_cparams(("parallel",)),
        name="rmsnorm",
    )(x2d, g.reshape(1, d))


def _mm_kernel(a_ref, w_ref, o_ref):
    o_ref[...] = _dot(a_ref[...], w_ref[...]).astype(o_ref.dtype)


def _mm_res_kernel(a_ref, w_ref, r_ref, o_ref):
    o_ref[...] = r_ref[...] + _dot(a_ref[...], w_ref[...])


def _matmul(a, w, layer, out_dtype=F32, residual=None, tm=512, tn=512, name="matmul"):
    m, k = a.shape
    n = w.shape[2]
    tm, tn = min(tm, m), min(tn, n)
    in_specs = [pl.BlockSpec((tm, k), lambda i, j: (i, 0)),
                pl.BlockSpec((None, k, tn), lambda i, j: (layer, 0, j))]
    args = [a, w]
    body = _mm_kernel
    if residual is not None:
        in_specs.append(pl.BlockSpec((tm, tn), lambda i, j: (i, j)))
        args.append(residual)
        body = _mm_res_kernel
    return pl.pallas_call(
        body,
        grid=(m // tm, n // tn),
        in_specs=in_specs,
        out_specs=pl.BlockSpec((tm, tn), lambda i, j: (i, j)),
        out_shape=jax.ShapeDtypeStruct((m, n), out_dtype),
        compiler_params=_cparams(("parallel", "parallel")),
        name=name,
    )(*args)


def _lru_kernel(x_ref, gate_ref, cw_ref, cb_ref, wa_ref, ba_ref, wx_ref, bx_ref, lam_ref,
                o_ref, tail_ref, h_ref):
    @pl.when(pl.program_id(1) == 0)
    def _():
        tail_ref[...] = jnp.zeros_like(tail_ref)
        h_ref[...] = jnp.zeros_like(h_ref)

    x = x_ref[...]
    n = x.shape[0]
    u = _causal_conv(x, tail_ref[...], cw_ref[...]) + cb_ref[...]
    tail_ref[...] = x[n - SUBLANES:, :]
    ubs = _heads(u, LRU_BLOCK, LRU_BLOCKS)
    r = jnp.concatenate([_dot(ub, wa_ref[b]) for b, ub in enumerate(ubs)], axis=1)
    i = jnp.concatenate([_dot(ub, wx_ref[b]) for b, ub in enumerate(ubs)], axis=1)
    r = _sigmoid(r + ba_ref[...])
    i = _sigmoid(i + bx_ref[...])
    log_a = (-LRU_C * r) * _softplus(-lam_ref[...])
    a = jnp.exp(log_a)
    xin = jnp.sqrt(-jnp.tanh(log_a) * (a * a + 1.0)) * (i * u)
    row = _iota(a.shape, 0)
    s = 1
    while s < n:
        keep = row >= s
        a_sh = jnp.where(keep, pltpu.roll(a, s, 0), 1.0)
        x_sh = jnp.where(keep, pltpu.roll(xin, s, 0), 0.0)
        xin = a * x_sh + xin
        a = a * a_sh
        s *= 2
    h = xin + a * h_ref[0:1, :]
    h_ref[...] = jnp.broadcast_to(h[n - 1:n, :], h_ref.shape)
    o_ref[...] = (h * _gelu_tanh(gate_ref[...])).astype(o_ref.dtype)


def _lru(proj, batch, seq, layer, cw, cb, wa, ba, wx, bx, lam):
    t = proj.shape[0]
    nsb = seq // SEQ_BLOCK
    row = lambda b, s: b * nsb + s
    vec = lambda a: a.reshape(a.shape[0], 1, BW)
    vspec = pl.BlockSpec((None, 1, BW), lambda b, s: (layer, 0, 0))
    wspec = pl.BlockSpec((None, LRU_BLOCKS, LRU_BLOCK, LRU_BLOCK), lambda b, s: (layer, 0, 0, 0))
    return pl.pallas_call(
        _lru_kernel,
        grid=(batch, nsb),
        in_specs=[pl.BlockSpec((SEQ_BLOCK, BW), lambda b, s: (row(b, s), OFF_AX // BW)),
                  pl.BlockSpec((SEQ_BLOCK, BW), lambda b, s: (row(b, s), OFF_AG // BW)),
                  pl.BlockSpec((None, CONV_WIDTH, BW), lambda b, s: (layer, 0, 0)),
                  vspec, wspec, vspec, wspec, vspec, vspec],
        out_specs=pl.BlockSpec((SEQ_BLOCK, BW), lambda b, s: (row(b, s), 0)),
        out_shape=jax.ShapeDtypeStruct((t, BW), BF16),
        scratch_shapes=[pltpu.VMEM((SUBLANES, BW), F32), pltpu.VMEM((SUBLANES, BW), F32)],
        compiler_params=_cparams(("parallel", "arbitrary")),
        name="rg_lru",
    )(proj, proj, cw, vec(cb), wa, vec(ba), wx, vec(bx), vec(lam))


def _gla_kernel(q_ref, k_ref, v_ref, g_ref, sm_ref, wd_ref, bd_ref, ng_ref, o_ref, st_ref):
    @pl.when(pl.program_id(1) == 0)
    def _():
        st_ref[...] = jnp.zeros_like(st_ref)

    nh, dk, dv = GLA_HEADS, GLA_DK, GLA_DV
    _, _, tril, _ = _block_masks(SEQ_BLOCK)
    log_alpha = _log_sigmoid(_dot(sm_ref[...], wd_ref[...], HI) + bd_ref[...]) / GLA_TAU
    bcum = _chunk_cumsum(log_alpha)
    b_last = _chunk_last(bcum)
    k = k_ref[...]
    q_decs = _heads(q_ref[...] * (GLA_DK ** -0.5) * jnp.exp(bcum), dk, nh)
    k_decs = _heads(k * jnp.exp(-bcum), dk, nh)
    k_tails = _heads(k * jnp.exp(b_last - bcum), dk, nh)
    b_lasts = _heads(b_last, dk, nh)
    vs = _heads(v_ref[...], dv, nh)
    attns = [jnp.where(tril, _dot_nt(qd, kd), 0.0) for qd, kd in zip(q_decs, k_decs)]
    o_intras = [_dot(a, v) for a, v in zip(attns, vs)]
    sts = [st_ref[h] for h in range(nh)]
    for c in range(N_CHUNKS):
        sl = slice(c * CHUNK, (c + 1) * CHUNK)
        outs = [oi[sl] + _dot_nt(qd[sl], st) for oi, qd, st in zip(o_intras, q_decs, sts)]
        kvs = [_dot_tn(v[sl], kt[sl]) for v, kt in zip(vs, k_tails)]
        sts = [st * jnp.exp(bl[c * CHUNK:c * CHUNK + 1, :]) + kv for st, bl, kv in zip(sts, b_lasts, kvs)]
        for h, o in enumerate(outs):
            cols = slice(h * dv, (h + 1) * dv)
            o = o * lax.rsqrt(jnp.mean(o * o, axis=-1, keepdims=True) + NORM_EPS) * ng_ref[...]
            o_ref[sl, cols] = (o * _silu(g_ref[sl, cols])).astype(o_ref.dtype)
    for h in range(nh):
        st_ref[h] = sts[h]


def _gla(proj, small, batch, seq, layer, w_decay, b_decay, norm_g):
    t = proj.shape[0]
    nsb = seq // SEQ_BLOCK
    wq = GLA_HEADS * GLA_DK
    row = lambda b, s: b * nsb + s
    depth = w_decay.shape[0]
    wd = jnp.zeros((depth, LANES, wq), F32).at[:, LANE_DECAY:LANE_DECAY + GLA_RANK].set(w_decay)
    return pl.pallas_call(
        _gla_kernel,
        grid=(batch, nsb),
        in_specs=[pl.BlockSpec((SEQ_BLOCK, wq), lambda b, s: (row(b, s), OFF_BQ // wq)),
                  pl.BlockSpec((SEQ_BLOCK, wq), lambda b, s: (row(b, s), OFF_BK // wq)),
                  pl.BlockSpec((SEQ_BLOCK, BW), lambda b, s: (row(b, s), OFF_BV // BW)),
                  pl.BlockSpec((SEQ_BLOCK, BW), lambda b, s: (row(b, s), OFF_BG // BW)),
                  pl.BlockSpec((SEQ_BLOCK, LANES), lambda b, s: (row(b, s), 0)),
                  pl.BlockSpec((None, LANES, wq), lambda b, s: (layer, 0, 0)),
                  pl.BlockSpec((None, 1, wq), lambda b, s: (layer, 0, 0)),
                  pl.BlockSpec((None, 1, GLA_DV), lambda b, s: (layer, 0, 0))],
        out_specs=pl.BlockSpec((SEQ_BLOCK, BW), lambda b, s: (row(b, s), 0)),
        out_shape=jax.ShapeDtypeStruct((t, BW), BF16),
        scratch_shapes=[pltpu.VMEM((GLA_HEADS, GLA_DV, GLA_DK), F32)],
        compiler_params=_cparams(("parallel", "arbitrary")),
        name="gla",
    )(proj, proj, proj, proj, small, wd, b_decay.reshape(depth, 1, wq), norm_g.reshape(depth, 1, GLA_DV))


def _unit_lower_inverse(lows, ri, ci):
    eye = (ri == ci).astype(F32)
    same16 = (ri // 16) == (ci // 16)
    same32 = (ri // 32) == (ci // 32)
    d1 = [jnp.where(same16, low, 0.0) for low in lows]
    d2 = [_dot(a, a) for a in d1]
    m = [eye - a for a in d1]
    d4 = [_dot(a, a) for a in d2]
    m = [a + _dot(a, b) for a, b in zip(m, d2)]
    d8 = [_dot(a, a) for a in d4]
    m = [a + _dot(a, b) for a, b in zip(m, d4)]
    m = [a + _dot(a, b) for a, b in zip(m, d8)]
    c1 = [jnp.where(same32 & ~same16, low, 0.0) for low in lows]
    t = [_dot(a, b) for a, b in zip(m, c1)]
    m = [a - _dot(b, a) for a, b in zip(m, t)]
    c2 = [jnp.where(same32, 0.0, low) for low in lows]
    t = [_dot(a, b) for a, b in zip(m, c2)]
    m = [a - _dot(b, a) for a, b in zip(m, t)]
    return m


def _gdn_kernel(q_ref, k_ref, v_ref, g_ref, sm_ref, cwq_ref, cwk_ref, cwv_ref, alog_ref, dtb_ref,
                ng_ref, o_ref, st_ref, tq_ref, tk_ref, tv_ref):
    @pl.when(pl.program_id(1) == 0)
    def _():
        st_ref[...] = jnp.zeros_like(st_ref)
        tq_ref[...] = jnp.zeros_like(tq_ref)
        tk_ref[...] = jnp.zeros_like(tk_ref)
        tv_ref[...] = jnp.zeros_like(tv_ref)

    n = SEQ_BLOCK
    nh, hd = GDN_HEADS, GDN_HEAD_DIM
    q_raw, k_raw, v_raw = q_ref[...], k_ref[...], v_ref[...]
    qs = _heads(_silu(_causal_conv(q_raw, tq_ref[...], cwq_ref[...])), hd, nh)
    ks = _heads(_silu(_causal_conv(k_raw, tk_ref[...], cwk_ref[...])), hd, nh)
    vs = _heads(_silu(_causal_conv(v_raw, tv_ref[...], cwv_ref[...])), hd, nh)
    tq_ref[...] = q_raw[n - SUBLANES:, :]
    tk_ref[...] = k_raw[n - SUBLANES:, :]
    tv_ref[...] = v_raw[n - SUBLANES:, :]
    sm = sm_ref[...]
    beta_lanes = _sigmoid(sm)
    gcum_lanes = _chunk_cumsum(-jnp.exp(alog_ref[...]) * _softplus(sm + dtb_ref[...]))
    ri, ci, tril, strict = _block_masks(n)

    qs = [q * lax.rsqrt(jnp.sum(q * q, axis=-1, keepdims=True) + NORM_EPS) * (hd ** -0.5) for q in qs]
    ks = [k * lax.rsqrt(jnp.sum(k * k, axis=-1, keepdims=True) + NORM_EPS) for k in ks]
    betas = [beta_lanes[:, LANE_BETA + h:LANE_BETA + h + 1] for h in range(nh)]
    gcums = [gcum_lanes[:, LANE_ALPHA + h:LANE_ALPHA + h + 1] for h in range(nh)]
    g_lasts = [_chunk_last(g) for g in gcums]
    gbs = [jnp.broadcast_to(g, (n, n)) for g in gcums]
    decays = [jnp.where(tril, jnp.exp(jnp.where(tril, gb - gb.T, 0.0)), 0.0) for gb in gbs]
    kks = [_dot_nt(k, k) for k in ks]
    lows = [jnp.where(strict, b * kk * d, 0.0) for b, kk, d in zip(betas, kks, decays)]
    qks = [_dot_nt(q, k) * d for q, k, d in zip(qs, ks, decays)]
    minvs = _unit_lower_inverse(lows, ri, ci)
    e_gs = [jnp.exp(g) for g in gcums]
    uws = [_dot(mi, jnp.concatenate([v * b, k * (b * e)], axis=1))
           for mi, v, k, b, e in zip(minvs, vs, ks, betas, e_gs)]
    ps = [_dot(qk, uw) for qk, uw in zip(qks, uws)]
    q_effs = [q * e - p[:, hd:] for q, e, p in zip(qs, e_gs, ps)]
    k_tails = [k * jnp.exp(gl - g) for k, gl, g in zip(ks, g_lasts, gcums)]
    sts = [st_ref[h] for h in range(nh)]
    for c in range(N_CHUNKS):
        sl = slice(c * CHUNK, (c + 1) * CHUNK)
        outs = [p[sl, :hd] + _dot(qe[sl], st) for p, qe, st in zip(ps, q_effs, sts)]
        abs_ = [_dot_tn(kt[sl], uw[sl]) for kt, uw in zip(k_tails, uws)]
        sts = [st * jnp.exp(gl[c * CHUNK:c * CHUNK + 1, :]) + ab[:, :hd] - _dot(ab[:, hd:], st)
               for gl, st, ab in zip(g_lasts, sts, abs_)]
        for h, o in enumerate(outs):
            cols = slice(h * hd, (h + 1) * hd)
            o = o * lax.rsqrt(jnp.mean(o * o, axis=-1, keepdims=True) + NORM_EPS) * ng_ref[...]
            o_ref[sl, cols] = (o * _silu(g_ref[sl, cols])).astype(o_ref.dtype)
    for h in range(nh):
        st_ref[h] = sts[h]


def _gdn(proj, small, batch, seq, layer, conv_w, a_log, dt_bias, norm_g):
    t = proj.shape[0]
    nsb = seq // SEQ_BLOCK
    hd = GDN_HEAD_DIM
    row = lambda b, s: b * nsb + s
    depth = a_log.shape[0]
    lane_row = lambda a: jnp.zeros((depth, 1, LANES), F32).at[:, 0, LANE_ALPHA:LANE_ALPHA + GDN_HEADS].set(a)
    lspec = pl.BlockSpec((None, 1, LANES), lambda b, s: (layer, 0, 0))
    return pl.pallas_call(
        _gdn_kernel,
        grid=(batch, nsb),
        in_specs=[pl.BlockSpec((SEQ_BLOCK, BW), lambda b, s: (row(b, s), OFF_CQ // BW)),
                  pl.BlockSpec((SEQ_BLOCK, BW), lambda b, s: (row(b, s), OFF_CK // BW)),
                  pl.BlockSpec((SEQ_BLOCK, BW), lambda b, s: (row(b, s), OFF_CV // BW)),
                  pl.BlockSpec((SEQ_BLOCK, BW), lambda b, s: (row(b, s), OFF_CG // BW)),
                  pl.BlockSpec((SEQ_BLOCK, LANES), lambda b, s: (row(b, s), 0)),
                  pl.BlockSpec((None, CONV_WIDTH, BW), lambda b, s: (layer, 0, 0)),
                  pl.BlockSpec((None, CONV_WIDTH, BW), lambda b, s: (layer, 0, 1)),
                  pl.BlockSpec((None, CONV_WIDTH, BW), lambda b, s: (layer, 0, 2)),
                  lspec, lspec,
                  pl.BlockSpec((None, 1, hd), lambda b, s: (layer, 0, 0))],
        out_specs=pl.BlockSpec((SEQ_BLOCK, BW), lambda b, s: (row(b, s), 0)),
        out_shape=jax.ShapeDtypeStruct((t, BW), BF16),
        scratch_shapes=[pltpu.VMEM((GDN_HEADS, hd, hd), F32)] + [pltpu.VMEM((SUBLANES, BW), F32)] * 3,
        compiler_params=_cparams(("parallel", "arbitrary")),
        name="gdn",
    )(proj, proj, proj, proj, small, conv_w, conv_w, conv_w, lane_row(a_log), lane_row(dt_bias),
      norm_g.reshape(depth, 1, hd))


def _mlstm_kernel(q_ref, k_ref, v_ref, og_ref, sm_ref, bi_ref, bf_ref, ng_ref, o_ref,
                  c_ref, n_ref, m_ref):
    @pl.when(pl.program_id(1) == 0)
    def _():
        c_ref[...] = jnp.zeros_like(c_ref)
        n_ref[...] = jnp.zeros_like(n_ref)
        m_ref[...] = jnp.zeros_like(m_ref)

    n = SEQ_BLOCK
    nh, dqk, dv = MLSTM_HEADS, MLSTM_DQK, MLSTM_DV
    hr = range(nh)
    _, _, tril, _ = _block_masks(n)
    qs = _heads(q_ref[...], dqk, nh)
    ks = _heads(k_ref[...] * (MLSTM_DQK ** -0.5), dqk, nh)
    vs = _heads(v_ref[...], dv, nh)
    sm = sm_ref[...]
    ig_lanes = sm + bi_ref[...]
    bcum_lanes = _chunk_cumsum(_log_sigmoid(sm + bf_ref[...]))
    igs = [ig_lanes[:, LANE_I + h:LANE_I + h + 1] for h in hr]
    bcums = [bcum_lanes[:, LANE_F + h:LANE_F + h + 1] for h in hr]
    b_lasts = [_chunk_last(b) for b in bcums]
    logds = [jnp.where(tril, b + _row_form(i - b), -jnp.inf) for b, i in zip(bcums, igs)]
    m_intras = [jnp.max(ld, axis=1, keepdims=True) for ld in logds]
    tails = [bl - b + i for bl, b, i in zip(b_lasts, bcums, igs)]
    spread = lambda xs: jnp.concatenate([jnp.broadcast_to(x, (CHUNK, 1)) for x in xs], axis=0)
    m_prev_rows, a_max_rows, s_olds, s_news = [], [], [], []
    for h in hr:
        m_st = m_ref[h, 0:1, 0:1]
        m_prev, a_max, s_old, s_new = [], [], [], []
        for c in range(N_CHUNKS):
            a_c = jnp.max(tails[h][c * CHUNK:(c + 1) * CHUNK], axis=0, keepdims=True)
            bl_c = bcums[h][(c + 1) * CHUNK - 1:(c + 1) * CHUNK, :]
            m_new = jnp.maximum(bl_c + m_st, a_c)
            m_prev.append(m_st)
            a_max.append(a_c)
            s_old.append(jnp.exp(bl_c + m_st - m_new))
            s_new.append(jnp.exp(a_c - m_new))
            m_st = m_new
        m_ref[h] = jnp.broadcast_to(m_st, m_ref.shape[1:])
        m_prev_rows.append(spread(m_prev))
        a_max_rows.append(spread(a_max))
        s_olds.append(s_old)
        s_news.append(s_new)
    k_ws = [k * jnp.exp(t - a) for k, t, a in zip(ks, tails, a_max_rows)]
    m_inters = [b + mp for b, mp in zip(bcums, m_prev_rows)]
    m_is = [jnp.maximum(mi, mx) for mi, mx in zip(m_inters, m_intras)]
    s_inters = [jnp.exp(mi - m) for mi, m in zip(m_inters, m_is)]
    scs = [_dot_nt(q, k) * jnp.exp(ld - m) for q, k, ld, m in zip(qs, ks, logds, m_is)]
    num_intras = [_dot(sc, v) for sc, v in zip(scs, vs)]
    den_intras = [jnp.sum(sc, axis=1, keepdims=True) for sc in scs]
    floors = [jnp.exp(-m) for m in m_is]
    c_sts = [c_ref[h] for h in hr]
    n_sts = [n_ref[h, 0:1, :] for h in hr]
    for c in range(N_CHUNKS):
        sl = slice(c * CHUNK, (c + 1) * CHUNK)
        nums = [ni[sl] + si[sl] * _dot(q[sl], cs) for ni, si, q, cs in zip(num_intras, s_inters, qs, c_sts)]
        dens = [di[sl] + si[sl] * jnp.sum(q[sl] * ns, axis=1, keepdims=True)
                for di, si, q, ns in zip(den_intras, s_inters, qs, n_sts)]
        kvs = [_dot_tn(kw[sl], v[sl]) for kw, v in zip(k_ws, vs)]
        c_sts = [cs * so[c] + kv * sn[c] for cs, so, kv, sn in zip(c_sts, s_olds, kvs, s_news)]
        n_sts = [ns * so[c] + jnp.sum(kw[sl], axis=0, keepdims=True) * sn[c]
                 for ns, so, kw, sn in zip(n_sts, s_olds, k_ws, s_news)]
        for h in hr:
            cols = slice(h * dv, (h + 1) * dv)
            hh = nums[h] / jnp.maximum(jnp.abs(dens[h]), floors[h][sl])
            hh = hh * lax.rsqrt(jnp.mean(hh * hh, axis=-1, keepdims=True) + NORM_EPS) * ng_ref[...]
            o_ref[sl, cols] = (hh * _sigmoid(og_ref[sl, cols])).astype(o_ref.dtype)
    for h in hr:
        c_ref[h] = c_sts[h]
        n_ref[h] = jnp.broadcast_to(n_sts[h], n_ref.shape[1:])


def _mlstm(proj, small, batch, seq, layer, b_i, b_f, norm_g):
    t = proj.shape[0]
    nsb = seq // SEQ_BLOCK
    nh, dqk, dv = MLSTM_HEADS, MLSTM_DQK, MLSTM_DV
    wq = nh * dqk
    row = lambda b, s: b * nsb + s
    depth = b_i.shape[0]
    lane_row = lambda a, off: jnp.zeros((depth, 1, LANES), F32).at[:, 0, off:off + nh].set(a)
    lspec = pl.BlockSpec((None, 1, LANES), lambda b, s: (layer, 0, 0))
    return pl.pallas_call(
        _mlstm_kernel,
        grid=(batch, nsb),
        in_specs=[pl.BlockSpec((SEQ_BLOCK, wq), lambda b, s: (row(b, s), OFF_DQ // wq)),
                  pl.BlockSpec((SEQ_BLOCK, wq), lambda b, s: (row(b, s), OFF_DK // wq)),
                  pl.BlockSpec((SEQ_BLOCK, BW), lambda b, s: (row(b, s), OFF_DV // BW)),
                  pl.BlockSpec((SEQ_BLOCK, BW), lambda b, s: (row(b, s), OFF_DO // BW)),
                  pl.BlockSpec((SEQ_BLOCK, LANES), lambda b, s: (row(b, s), 0)),
                  lspec, lspec,
                  pl.BlockSpec((None, 1, dv), lambda b, s: (layer, 0, 0))],
        out_specs=pl.BlockSpec((SEQ_BLOCK, BW), lambda b, s: (row(b, s), 0)),
        out_shape=jax.ShapeDtypeStruct((t, BW), BF16),
        scratch_shapes=[pltpu.VMEM((nh, dqk, dv), F32), pltpu.VMEM((nh, SUBLANES, dqk), F32),
                        pltpu.VMEM((nh, SUBLANES, LANES), F32)],
        compiler_params=_cparams(("parallel", "arbitrary")),
        name="mlstm",
    )(proj, proj, proj, proj, small, lane_row(b_i, LANE_I), lane_row(b_f, LANE_F),
      norm_g.reshape(depth, 1, dv))


def _merge_kernel(xn_ref, ya_ref, yb_ref, yc_ref, yd_ref, wg_ref, bg_ref, wb_ref, o_ref):
    xn = xn_ref[...]
    acc = None
    for n, y_ref in enumerate((ya_ref, yb_ref, yc_ref, yd_ref)):
        gate = _sigmoid(_dot(xn, wg_ref[n]) + bg_ref[n])
        term = gate * _dot(y_ref[...], wb_ref[n])
        acc = term if acc is None else acc + term
    o_ref[...] = acc.astype(o_ref.dtype)


def _merge(xn, ys, layer, wg, bg, wb, tm=512, tn=256):
    t, d = xn.shape
    tm = min(tm, t)
    depth, nb = wg.shape[:2]
    yspec = pl.BlockSpec((tm, BW), lambda i, j: (i, 0))
    return pl.pallas_call(
        _merge_kernel,
        grid=(t // tm, d // tn),
        in_specs=[pl.BlockSpec((tm, d), lambda i, j: (i, 0)), yspec, yspec, yspec, yspec,
                  pl.BlockSpec((None, nb, d, tn), lambda i, j: (layer, 0, 0, j)),
                  pl.BlockSpec((None, nb, 1, tn), lambda i, j: (layer, 0, 0, j)),
                  pl.BlockSpec((None, nb, BW, tn), lambda i, j: (layer, 0, 0, j))],
        out_specs=pl.BlockSpec((tm, tn), lambda i, j: (i, j)),
        out_shape=jax.ShapeDtypeStruct((t, d), BF16),
        compiler_params=_cparams(("parallel", "parallel")),
        name="gated_merge",
    )(xn, *ys, wg, bg.reshape(depth, nb, 1, d), wb)


def _router_kernel(x_ref, g_ref, wr_ref, br_ref, h_ref, info_ref, idx_ref, cnt_ref, carry_ref):
    @pl.when(pl.program_id(0) == 0)
    def _():
        carry_ref[...] = jnp.zeros_like(carry_ref)

    x = x_ref[...]
    tm = x.shape[0]
    h = x * lax.rsqrt(jnp.mean(x * x, axis=-1, keepdims=True) + NORM_EPS) * g_ref[...]
    h_ref[...] = _pack_pairs(h)
    logits = _dot(h, wr_ref[...], HI) + br_ref[...]
    lane = _iota((tm, LANES), 1).astype(F32)
    big = float(LANES)
    gl = jnp.where(lane < N_GROUPS, logits, -jnp.inf)
    gmax = jnp.max(gl, axis=1, keepdims=True)
    grp = jnp.min(jnp.where(gl == gmax, lane, big), axis=1, keepdims=True)
    p_grp = 1.0 / jnp.sum(jnp.exp(gl - gmax), axis=1, keepdims=True)
    lo = N_GROUPS + grp * EXPERTS_PER_GROUP
    el = jnp.where((lane >= lo) & (lane < lo + EXPERTS_PER_GROUP), logits, -jnp.inf)
    v0 = jnp.max(el, axis=1, keepdims=True)
    i0 = jnp.min(jnp.where(el == v0, lane, big), axis=1, keepdims=True)
    el = jnp.where(lane == i0, -jnp.inf, el)
    v1 = jnp.max(el, axis=1, keepdims=True)
    i1 = jnp.min(jnp.where(el == v1, lane, big), axis=1, keepdims=True)
    e = jnp.exp(v1 - v0)
    w0 = p_grp / (1.0 + e)
    w1 = p_grp * e / (1.0 + e)
    e0 = i0 - N_GROUPS
    e1 = i1 - N_GROUPS
    oh0 = (lane == e0).astype(F32)
    oh1 = (lane == e1).astype(F32)
    both = oh0 + oh1
    strict = (_iota((tm, tm), 0) > _iota((tm, tm), 1)).astype(BF16)
    before = carry_ref[0:1, :] + _dot(strict, both.astype(BF16))
    r0 = jnp.sum(oh0 * before, axis=1, keepdims=True)
    r1 = jnp.sum(oh1 * before, axis=1, keepdims=True)
    total = carry_ref[0:1, :] + jnp.sum(both, axis=0, keepdims=True)
    carry_ref[...] = jnp.broadcast_to(total, carry_ref.shape)
    cnt_ref[...] = jnp.broadcast_to(total, cnt_ref.shape)
    info_ref[...] = jnp.where(lane == 0.0, w0, jnp.where(lane == 1.0, w1, 0.0))
    idx = jnp.where(lane == 0.0, e0, jnp.where(lane == 1.0, e1,
                                               jnp.where(lane == 2.0, r0, jnp.where(lane == 3.0, r1, 0.0))))
    idx_ref[...] = idx.astype(jnp.int32)


def _router(x2d, layer, g, w_grp, b_grp, w_exp, b_exp, tm=256):
    t, d = x2d.shape
    tm = min(tm, t)
    depth = g.shape[0]
    wr = (jnp.zeros((depth, d, LANES), F32).at[:, :, :N_GROUPS].set(w_grp)
          .at[:, :, N_GROUPS:N_GROUPS + N_EXPERTS].set(w_exp))
    br = (jnp.zeros((depth, 1, LANES), F32).at[:, 0, :N_GROUPS].set(b_grp)
          .at[:, 0, N_GROUPS:N_GROUPS + N_EXPERTS].set(b_exp))
    return pl.pallas_call(
        _router_kernel,
        grid=(t // tm,),
        in_specs=[pl.BlockSpec((tm, d), lambda i: (i, 0)),
                  pl.BlockSpec((None, 1, d), lambda i: (layer, 0, 0)),
                  pl.BlockSpec((None, d, LANES), lambda i: (layer, 0, 0)),
                  pl.BlockSpec((None, 1, LANES), lambda i: (layer, 0, 0))],
        out_specs=[pl.BlockSpec((tm, d // 2), lambda i: (i, 0)),
                   pl.BlockSpec((tm, LANES), lambda i: (i, 0)),
                   pl.BlockSpec((tm, LANES), lambda i: (i, 0)),
                   pl.BlockSpec((SUBLANES, LANES), lambda i: (0, 0))],
        out_shape=[jax.ShapeDtypeStruct((t, d // 2), jnp.uint32),
                   jax.ShapeDtypeStruct((t, LANES), F32),
                   jax.ShapeDtypeStruct((t, LANES), jnp.int32),
                   jax.ShapeDtypeStruct((SUBLANES, LANES), F32)],
        scratch_shapes=[pltpu.VMEM((SUBLANES, LANES), F32)],
        compiler_params=_cparams(("arbitrary",)),
        name="router",
    )(x2d, g.reshape(depth, 1, d), wr, br)


def _dispatch_kernel(pos_ref, zrow_ref, h_ref, xs_ref, zero_ref, sem, zsem):
    i = pl.program_id(0)
    tm = h_ref.shape[0]
    zt = zero_ref.shape[0]

    @pl.when(i == 0)
    def _():
        zero_ref[...] = jnp.zeros_like(zero_ref)

        def zero_copy(e):
            start = pl.multiple_of(jnp.maximum(zrow_ref[e], 0), zt)
            return pltpu.make_async_copy(zero_ref, xs_ref.at[pl.ds(start, zt)], zsem)

        def zissue(e, carry):
            @pl.when(zrow_ref[e] >= 0)
            def _():
                zero_copy(e).start()
            return carry

        def zdrain(e, carry):
            @pl.when(zrow_ref[e] >= 0)
            def _():
                zero_copy(e).wait()
            return carry

        lax.fori_loop(0, zrow_ref.shape[0], zissue, 0)
        lax.fori_loop(0, zrow_ref.shape[0], zdrain, 0)

    def row_copy(r, p):
        return pltpu.make_async_copy(h_ref.at[pl.ds(r, 1)], xs_ref.at[pl.ds(p, 1)], sem)

    def issue(r, carry):
        t = i * tm + r
        for k in range(2):
            row_copy(r, pos_ref[2 * t + k]).start()
        return carry

    def drain(r, carry):
        for k in range(2):
            row_copy(r, 0).wait()
        return carry

    lax.fori_loop(0, tm, issue, 0)
    lax.fori_loop(0, tm, drain, 0)


def _dispatch(pos, zrow, h, n_rows, tm=ROW_TILE):
    t, d = h.shape
    tm = min(tm, t)
    return pl.pallas_call(
        _dispatch_kernel,
        grid_spec=pltpu.PrefetchScalarGridSpec(
            num_scalar_prefetch=2,
            grid=(t // tm,),
            in_specs=[pl.BlockSpec((tm, d), lambda i, pos, zrow: (i, 0))],
            out_specs=pl.BlockSpec(memory_space=pl.ANY),
            scratch_shapes=[pltpu.VMEM((EXPERT_TILE, d), h.dtype), pltpu.SemaphoreType.DMA(()),
                            pltpu.SemaphoreType.DMA(())],
        ),
        out_shape=jax.ShapeDtypeStruct((n_rows, d), h.dtype),
        compiler_params=_cparams(("arbitrary",)),
        name="dispatch",
    )(pos, zrow, h)


def _combine_kernel(keep_x, pos_ref, x_ref, info_ref, g_ref, ys_ref, *rest):
    outs, (buf_ref, sem) = rest[:-2], rest[-2:]
    i = pl.program_id(0)
    n = pl.num_programs(0)
    tm = x_ref.shape[0]
    half = x_ref.shape[1] // 2

    def row_copy(slot, r, k, p):
        return pltpu.make_async_copy(ys_ref.at[pl.ds(p, 1)], buf_ref.at[slot, k, pl.ds(r, 1)],
                                     sem.at[slot])

    def gather_tile(tile, slot):
        def issue(r, carry):
            t = tile * tm + r
            for k in range(2):
                row_copy(slot, r, k, pos_ref[2 * t + k]).start()
            return carry
        lax.fori_loop(0, tm, issue, 0)

    @pl.when(i == 0)
    def _():
        gather_tile(0, 0)

    @pl.when(i + 1 < n)
    def _():
        gather_tile(i + 1, (i + 1) % 2)

    slot = i % 2

    def drain(r, carry):
        for k in range(2):
            row_copy(slot, r, k, 0).wait()
        return carry

    lax.fori_loop(0, tm, drain, 0)
    info = info_ref[...]
    w0, w1 = info[:, 0:1], info[:, 1:2]
    lo0, hi0 = _unpack_pairs(buf_ref[slot, 0])
    lo1, hi1 = _unpack_pairs(buf_ref[slot, 1])
    x = x_ref[...]
    x_lo = x[:, :half] + w0 * lo0 + w1 * lo1
    x_hi = x[:, half:] + w0 * hi0 + w1 * hi1
    ms = (jnp.sum(x_lo * x_lo, axis=-1, keepdims=True)
          + jnp.sum(x_hi * x_hi, axis=-1, keepdims=True)) / x.shape[1]
    inv = lax.rsqrt(ms + NORM_EPS)
    g = g_ref[...]
    if keep_x:
        outs[0][:, :half] = x_lo
        outs[0][:, half:] = x_hi
    outs[-1][:, :half] = (x_lo * inv * g[:, :half]).astype(outs[-1].dtype)
    outs[-1][:, half:] = (x_hi * inv * g[:, half:]).astype(outs[-1].dtype)


def _combine(pos, x2d, info, ys, g, keep_x, norm_dtype, tm=ROW_TILE):
    t, d = x2d.shape
    tm = min(tm, t)
    row_spec = pl.BlockSpec((tm, d), lambda i, pos: (i, 0))
    out_specs = [row_spec] * (2 if keep_x else 1)
    out_shape = ([jax.ShapeDtypeStruct((t, d), F32)] if keep_x else []) + [jax.ShapeDtypeStruct((t, d), norm_dtype)]
    return pl.pallas_call(
        functools.partial(_combine_kernel, keep_x),
        grid_spec=pltpu.PrefetchScalarGridSpec(
            num_scalar_prefetch=1,
            grid=(t // tm,),
            in_specs=[row_spec,
                      pl.BlockSpec((tm, LANES), lambda i, pos: (i, 0)),
                      pl.BlockSpec((1, d), lambda i, pos: (0, 0)),
                      pl.BlockSpec(memory_space=pl.ANY)],
            out_specs=out_specs,
            scratch_shapes=[pltpu.VMEM((2, 2, tm, d // 2), ys.dtype),
                            pltpu.SemaphoreType.DMA((2,))],
        ),
        out_shape=out_shape,
        compiler_params=_cparams(("arbitrary",)),
        name="combine",
    )(pos, x2d, info, g.reshape(1, d), ys)


def _expert_kernel(te_ref, nu_ref, xs_ref, wg_ref, wu_ref, wd_ref, ys_ref):
    del te_ref
    i = pl.program_id(0)

    @pl.when(i < nu_ref[0])
    def _():
        lo, hi = _unpack_pairs(xs_ref[...])
        x = jnp.concatenate([lo.astype(BF16), hi.astype(BF16)], axis=1)
        a = _dot(x, wg_ref[...])
        u = _dot(x, wu_ref[...])
        act = (_silu(a) * u).astype(BF16)
        ys_ref[...] = _pack_pairs(_dot(act, wd_ref[...]))

    @pl.when(i >= nu_ref[0])
    def _():
        ys_ref[...] = jnp.zeros_like(ys_ref)


def _experts(tile_expert, n_used, xs, wg, wu, wd):
    p, dh = xs.shape
    d, f = wg.shape[1:]
    tm = EXPERT_TILE
    src_tile = lambda i, te, nu: (jnp.maximum(jnp.minimum(i, nu[0] - 1), 0), 0)
    return pl.pallas_call(
        _expert_kernel,
        grid_spec=pltpu.PrefetchScalarGridSpec(
            num_scalar_prefetch=2,
            grid=(p // tm,),
            in_specs=[pl.BlockSpec((tm, dh), src_tile),
                      pl.BlockSpec((None, d, f), lambda i, te, nu: (te[i], 0, 0)),
                      pl.BlockSpec((None, d, f), lambda i, te, nu: (te[i], 0, 0)),
                      pl.BlockSpec((None, f, d), lambda i, te, nu: (te[i], 0, 0))],
            out_specs=pl.BlockSpec((tm, dh), lambda i, te, nu: (i, 0)),
        ),
        out_shape=jax.ShapeDtypeStruct((p, dh), xs.dtype),
        compiler_params=_cparams(("arbitrary",)),
        name="experts",
    )(tile_expert, n_used, xs, wg, wu, wd)


def _hier_moe(x2d, layer, ffn_g, w_grp, b_grp, w_exp, b_exp, wg, wu, wd, next_g, keep_x, norm_dtype):
    t, d = x2d.shape
    tm = EXPERT_TILE
    n_rows = 2 * t + N_EXPERTS * tm
    h, info, idx, cnt = _router(x2d, layer, ffn_g, w_grp, b_grp, w_exp, b_exp)
    counts = cnt[0, :N_EXPERTS].astype(jnp.int32)
    padded = ((counts + tm - 1) // tm) * tm
    ends = jnp.cumsum(padded)
    offs = ends - padded
    pos = (offs[idx[:, 0:2]] + idx[:, 2:4]).reshape(2 * t)
    n_used = (ends[-1] // tm).reshape(1).astype(jnp.int32)
    tail = n_used[0] + jnp.arange(N_EXPERTS, dtype=jnp.int32)
    zrow = jnp.concatenate([jnp.where(padded > 0, ends - tm, -1),
                            jnp.where(tail < n_rows // tm, tail * tm, -1)]).astype(jnp.int32)
    tiles = jnp.minimum(jnp.arange(n_rows // tm, dtype=jnp.int32), n_used[0] - 1)
    tile_expert = jnp.sum(((ends // tm)[None, :] <= tiles[:, None]).astype(jnp.int32), axis=1)
    tile_expert = jnp.minimum(tile_expert, N_EXPERTS - 1) + layer * N_EXPERTS
    xs = _dispatch(pos, zrow, h, n_rows)
    ys = _experts(tile_expert, n_used, xs, wg, wu, wd)
    return _combine(pos, x2d, info, ys, next_g, keep_x, norm_dtype)


def _split_w_in(w):
    c0 = 5 * BW
    c1 = c0 + GLA_RANK
    c2 = c1 + 4 * BW
    c3 = c2 + 2 * GDN_HEADS
    c4 = c3 + 3 * BW
    c5 = c4 + 2 * MLSTM_HEADS
    big = jnp.concatenate([w[..., :c0], w[..., c1:c2], w[..., c3:c4]], axis=-1)
    n_small = GLA_RANK + 2 * GDN_HEADS + 2 * MLSTM_HEADS
    small = jnp.concatenate([w[..., c0:c1], w[..., c2:c3], w[..., c4:c5],
                             jnp.zeros(w.shape[:-1] + (LANES - n_small,), w.dtype)], axis=-1)
    return big.astype(BF16), small.astype(BF16)


def kernel(x, mix_norm, w_in, lru_conv_w, lru_conv_b, lru_w_a, lru_b_a, lru_w_x, lru_b_x, lru_lambda,
           gla_w_decay, gla_b_decay, gla_norm, gdn_conv_w, gdn_a_log, gdn_dt_bias, gdn_norm,
           mlstm_b_i, mlstm_b_f, mlstm_norm, w_branch, w_merge_gate, b_merge_gate, w_out,
           ffn_norm, w_group_router, b_group_router, w_expert_router, b_expert_router,
           w_exp_gate, w_exp_up, w_exp_down, final_norm):
    batch, seq, d = x.shape
    depth = w_in.shape[0]
    x2d = x.reshape(batch * seq, d)
    w_big, w_small = _split_w_in(w_in)
    wg_merge = w_merge_gate.astype(BF16)
    wb_merge = w_branch.astype(BF16)
    w_o = w_out.astype(BF16)
    f = w_exp_gate.shape[-1]
    we_gate = w_exp_gate.astype(BF16).reshape(depth * N_EXPERTS, d, f)
    we_up = w_exp_up.astype(BF16).reshape(depth * N_EXPERTS, d, f)
    we_down = w_exp_down.astype(BF16).reshape(depth * N_EXPERTS, f, d)
    xn = _rmsnorm(x2d, mix_norm[0], BF16)
    for l in range(depth):
        last = l == depth - 1
        proj = _matmul(xn, w_big, l, tm=1024, name="in_proj")
        small = _matmul(xn, w_small, l, name="in_proj_small")
        y_a = _lru(proj, batch, seq, l, lru_conv_w, lru_conv_b, lru_w_a, lru_b_a, lru_w_x, lru_b_x,
                   lru_lambda)
        y_b = _gla(proj, small, batch, seq, l, gla_w_decay, gla_b_decay, gla_norm)
        y_c = _gdn(proj, small, batch, seq, l, gdn_conv_w, gdn_a_log, gdn_dt_bias, gdn_norm)
        y_d = _mlstm(proj, small, batch, seq, l, mlstm_b_i, mlstm_b_f, mlstm_norm)
        merged = _merge(xn, (y_a, y_b, y_c, y_d), l, wg_merge, b_merge_gate, wb_merge)
        x2d = _matmul(merged, w_o, l, residual=x2d, tm=1024, name="out_proj")
        outs = _hier_moe(x2d, l, ffn_norm, w_group_router, b_group_router, w_expert_router,
                         b_expert_router, we_gate, we_up, we_down,
                         final_norm if last else mix_norm[l + 1], not last, F32 if last else BF16)
        if last:
            (y,) = outs
        else:
            x2d, xn = outs
    return y.reshape(batch, seq, d)
```

```python
import functools

import jax
import jax.numpy as jnp
from jax import lax
from jax.experimental import pallas as pl
from jax.experimental.pallas import tpu as pltpu

F32 = jnp.float32
BF16 = jnp.bfloat16
HI = lax.Precision.HIGHEST

D_MODEL = 4096
BW = D_MODEL // 4
CONV_WIDTH = 4
CHUNK = 64
NORM_EPS = 1e-6
LRU_BLOCKS = 8
LRU_BLOCK = BW // LRU_BLOCKS
LRU_C = 8.0
GLA_HEADS = 4
GLA_DV = BW // GLA_HEADS
GLA_DK = GLA_DV // 2
GLA_RANK = 16
GLA_TAU = 16.0
GDN_HEAD_DIM = 128
GDN_HEADS = BW // GDN_HEAD_DIM
MLSTM_HEADS = 4
MLSTM_DV = BW // MLSTM_HEADS
MLSTM_DQK = MLSTM_DV // 2
N_GROUPS = 4
EXPERTS_PER_GROUP = 8
N_EXPERTS = N_GROUPS * EXPERTS_PER_GROUP
D_EXPERT = D_MODEL // 8

LANES = 128
SUBLANES = 8
VMEM_LIMIT = 56 * 1024 * 1024

PROJ_BIG = 12 * BW
OFF_AX, OFF_AG = 0, BW
OFF_BQ, OFF_BK, OFF_BV, OFF_BG = 2 * BW, 2 * BW + 512, 3 * BW, 4 * BW
OFF_CQ, OFF_CK, OFF_CV, OFF_CG = 5 * BW, 6 * BW, 7 * BW, 8 * BW
OFF_DQ, OFF_DK, OFF_DV, OFF_DO = 9 * BW, 9 * BW + 512, 10 * BW, 11 * BW
LANE_DECAY, LANE_BETA, LANE_ALPHA, LANE_I, LANE_F = 0, 16, 24, 32, 36

SEQ_BLOCK = 256
N_CHUNKS = SEQ_BLOCK // CHUNK
EXPERT_TILE = 256
ROW_TILE = 256


def _cparams(sem):
    return pltpu.CompilerParams(dimension_semantics=sem, vmem_limit_bytes=VMEM_LIMIT)


def _softplus(x):
    return jnp.maximum(x, 0.0) + jnp.log1p(jnp.exp(-jnp.abs(x)))


def _log_sigmoid(x):
    return -_softplus(-x)


def _sigmoid(x):
    return 1.0 / (1.0 + jnp.exp(-x))


def _silu(x):
    return x * _sigmoid(x)


def _gelu_tanh(x):
    return 0.5 * x * (1.0 + jnp.tanh(0.7978845608028654 * (x + 0.044715 * (x * x * x))))


def _dot(a, b, precision=None):
    return jnp.dot(a, b, preferred_element_type=F32, precision=precision)


def _dot_nt(a, b, precision=None):
    return lax.dot_general(a, b, (((1,), (1,)), ((), ())), preferred_element_type=F32,
                           precision=precision)


def _dot_tn(a, b, precision=None):
    return lax.dot_general(a, b, (((0,), (0,)), ((), ())), preferred_element_type=F32,
                           precision=precision)


def _iota(shape, dim):
    return lax.broadcasted_iota(jnp.int32, shape, dim)


def _shift_rows(x, tail, s):
    r = pltpu.roll(x, s, 0)
    rt = pltpu.roll(tail, s, 0)
    head = jnp.where(_iota(tail.shape, 0) < s, rt, r[0:SUBLANES])
    return jnp.concatenate([head, r[SUBLANES:]], axis=0)


def _causal_conv(x, tail, w):
    y = x * w[CONV_WIDTH - 1:CONV_WIDTH, :]
    for s in range(1, CONV_WIDTH):
        y = y + _shift_rows(x, tail, s) * w[CONV_WIDTH - 1 - s:CONV_WIDTH - s, :]
    return y


def _chunk_cumsum(x):
    pos = _iota(x.shape, 0) & (CHUNK - 1)
    s = 1
    while s < CHUNK:
        x = x + jnp.where(pos >= s, pltpu.roll(x, s, 0), 0.0)
        s *= 2
    return x


def _chunk_last(x):
    return jnp.concatenate(
        [jnp.broadcast_to(x[(c + 1) * CHUNK - 1:(c + 1) * CHUNK], (CHUNK, x.shape[1]))
         for c in range(x.shape[0] // CHUNK)], axis=0)


def _row_form(col):
    n = col.shape[0]
    return jnp.broadcast_to(col, (n, n)).T


def _block_masks(n):
    ri = _iota((n, n), 0)
    ci = _iota((n, n), 1)
    same = (ri // CHUNK) == (ci // CHUNK)
    return ri, ci, same & (ri >= ci), same & (ri > ci)


def _pack_pairs(x):
    c = x.shape[1] // 2
    bits = lambda v: lax.bitcast_convert_type(v.astype(BF16).astype(F32), jnp.uint32)
    return (bits(x[:, :c]) >> 16) | bits(x[:, c:])


def _unpack_pairs(p):
    lo = lax.bitcast_convert_type(p << 16, F32)
    hi = lax.bitcast_convert_type(p & jnp.uint32(0xFFFF0000), F32)
    return lo, hi


def _heads(x, width, count):
    return [x[:, h * width:(h + 1) * width] for h in range(count)]


def _rmsnorm_kernel(x_ref, g_ref, o_ref):
    x = x_ref[...]
    ms = jnp.mean(x * x, axis=-1, keepdims=True)
    o_ref[...] = (x * lax.rsqrt(ms + NORM_EPS) * g_ref[...]).astype(o_ref.dtype)


def _rmsnorm(x2d, g, out_dtype, tm=512):
    t, d = x2d.shape
    return pl.pallas_call(
        _rmsnorm_kernel,
        grid=(t // tm,),
        in_specs=[pl.BlockSpec((tm, d), lambda i: (i, 0)),
                  pl.BlockSpec((1, d), lambda i: (0, 0))],
        out_specs=pl.BlockSpec((tm, d), lambda i: (i, 0)),
        out_shape=jax.ShapeDtypeStruct((t, d), out_dtype),
        compiler_params=_cparams(("parallel",)),
        name="rmsnorm",
    )(x2d, g.reshape(1, d))


def _mm_kernel(a_ref, w_ref, o_ref):
    o_ref[...] = _dot(a_ref[...], w_ref[...]).astype(o_ref.dtype)


def _mm_res_kernel(a_ref, w_ref, r_ref, o_ref):
    o_ref[...] = r_ref[...] + _dot(a_ref[...], w_ref[...])


def _matmul(a, w, layer, out_dtype=F32, residual=None, tm=512, tn=512, name="matmul"):
    m, k = a.shape
    n = w.shape[2]
    tm, tn = min(tm, m), min(tn, n)
    in_specs = [pl.BlockSpec((tm, k), lambda i, j: (i, 0)),
                pl.BlockSpec((None, k, tn), lambda i, j: (layer, 0, j))]
    args = [a, w]
    body = _mm_kernel
    if residual is not None:
        in_specs.append(pl.BlockSpec((tm, tn), lambda i, j: (i, j)))
        args.append(residual)
        body = _mm_res_kernel
    return pl.pallas_call(
        body,
        grid=(m // tm, n // tn),
        in_specs=in_specs,
        out_specs=pl.BlockSpec((tm, tn), lambda i, j: (i, j)),
        out_shape=jax.ShapeDtypeStruct((m, n), out_dtype),
        compiler_params=_cparams(("parallel", "parallel")),
        name=name,
    )(*args)


def _lru_kernel(x_ref, gate_ref, cw_ref, cb_ref, wa_ref, ba_ref, wx_ref, bx_ref, lam_ref,
                o_ref, tail_ref, h_ref):
    @pl.when(pl.program_id(1) == 0)
    def _():
        tail_ref[...] = jnp.zeros_like(tail_ref)
        h_ref[...] = jnp.zeros_like(h_ref)

    x = x_ref[...]
    n = x.shape[0]
    u = _causal_conv(x, tail_ref[...], cw_ref[...]) + cb_ref[...]
    tail_ref[...] = x[n - SUBLANES:, :]
    ubs = _heads(u, LRU_BLOCK, LRU_BLOCKS)
    r = jnp.concatenate([_dot(ub, wa_ref[b]) for b, ub in enumerate(ubs)], axis=1)
    i = jnp.concatenate([_dot(ub, wx_ref[b]) for b, ub in enumerate(ubs)], axis=1)
    r = _sigmoid(r + ba_ref[...])
    i = _sigmoid(i + bx_ref[...])
    log_a = (-LRU_C * r) * _softplus(-lam_ref[...])
    a = jnp.exp(log_a)
    xin = jnp.sqrt(-jnp.tanh(log_a) * (a * a + 1.0)) * (i * u)
    row = _iota(a.shape, 0)
    s = 1
    while s < n:
        keep = row >= s
        a_sh = jnp.where(keep, pltpu.roll(a, s, 0), 1.0)
        x_sh = jnp.where(keep, pltpu.roll(xin, s, 0), 0.0)
        xin = a * x_sh + xin
        a = a * a_sh
        s *= 2
    h = xin + a * h_ref[0:1, :]
    h_ref[...] = jnp.broadcast_to(h[n - 1:n, :], h_ref.shape)
    o_ref[...] = (h * _gelu_tanh(gate_ref[...])).astype(o_ref.dtype)


def _lru(proj, batch, seq, layer, cw, cb, wa, ba, wx, bx, lam):
    t = proj.shape[0]
    nsb = seq // SEQ_BLOCK
    row = lambda b, s: b * nsb + s
    vec = lambda a: a.reshape(a.shape[0], 1, BW)
    vspec = pl.BlockSpec((None, 1, BW), lambda b, s: (layer, 0, 0))
    wspec = pl.BlockSpec((None, LRU_BLOCKS, LRU_BLOCK, LRU_BLOCK), lambda b, s: (layer, 0, 0, 0))
    return pl.pallas_call(
        _lru_kernel,
        grid=(batch, nsb),
        in_specs=[pl.BlockSpec((SEQ_BLOCK, BW), lambda b, s: (row(b, s), OFF_AX // BW)),
                  pl.BlockSpec((SEQ_BLOCK, BW), lambda b, s: (row(b, s), OFF_AG // BW)),
                  pl.BlockSpec((None, CONV_WIDTH, BW), lambda b, s: (layer, 0, 0)),
                  vspec, wspec, vspec, wspec, vspec, vspec],
        out_specs=pl.BlockSpec((SEQ_BLOCK, BW), lambda b, s: (row(b, s), 0)),
        out_shape=jax.ShapeDtypeStruct((t, BW), BF16),
        scratch_shapes=[pltpu.VMEM((SUBLANES, BW), F32), pltpu.VMEM((SUBLANES, BW), F32)],
        compiler_params=_cparams(("parallel", "arbitrary")),
        name="rg_lru",
    )(proj, proj, cw, vec(cb), wa, vec(ba), wx, vec(bx), vec(lam))


def _gla_kernel(q_ref, k_ref, v_ref, g_ref, sm_ref, wd_ref, bd_ref, ng_ref, o_ref, st_ref):
    @pl.when(pl.program_id(1) == 0)
    def _():
        st_ref[...] = jnp.zeros_like(st_ref)

    nh, dk, dv = GLA_HEADS, GLA_DK, GLA_DV
    _, _, tril, _ = _block_masks(SEQ_BLOCK)
    log_alpha = _log_sigmoid(_dot(sm_ref[...], wd_ref[...], HI) + bd_ref[...]) / GLA_TAU
    bcum = _chunk_cumsum(log_alpha)
    b_last = _chunk_last(bcum)
    k = k_ref[...]
    q_decs = _heads(q_ref[...] * (GLA_DK ** -0.5) * jnp.exp(bcum), dk, nh)
    k_decs = _heads(k * jnp.exp(-bcum), dk, nh)
    k_tails = _heads(k * jnp.exp(b_last - bcum), dk, nh)
    b_lasts = _heads(b_last, dk, nh)
    vs = _heads(v_ref[...], dv, nh)
    attns = [jnp.where(tril, _dot_nt(qd, kd), 0.0) for qd, kd in zip(q_decs, k_decs)]
    o_intras = [_dot(a, v) for a, v in zip(attns, vs)]
    sts = [st_ref[h] for h in range(nh)]
    for c in range(N_CHUNKS):
        sl = slice(c * CHUNK, (c + 1) * CHUNK)
        outs = [oi[sl] + _dot_nt(qd[sl], st) for oi, qd, st in zip(o_intras, q_decs, sts)]
        kvs = [_dot_tn(v[sl], kt[sl]) for v, kt in zip(vs, k_tails)]
        sts = [st * jnp.exp(bl[c * CHUNK:c * CHUNK + 1, :]) + kv for st, bl, kv in zip(sts, b_lasts, kvs)]
        for h, o in enumerate(outs):
            cols = slice(h * dv, (h + 1) * dv)
            o = o * lax.rsqrt(jnp.mean(o * o, axis=-1, keepdims=True) + NORM_EPS) * ng_ref[...]
            o_ref[sl, cols] = (o * _silu(g_ref[sl, cols])).astype(o_ref.dtype)
    for h in range(nh):
        st_ref[h] = sts[h]


def _gla(proj, small, batch, seq, layer, w_decay, b_decay, norm_g):
    t = proj.shape[0]
    nsb = seq // SEQ_BLOCK
    wq = GLA_HEADS * GLA_DK
    row = lambda b, s: b * nsb + s
    depth = w_decay.shape[0]
    wd = jnp.zeros((depth, LANES, wq), F32).at[:, LANE_DECAY:LANE_DECAY + GLA_RANK].set(w_decay)
    return pl.pallas_call(
        _gla_kernel,
        grid=(batch, nsb),
        in_specs=[pl.BlockSpec((SEQ_BLOCK, wq), lambda b, s: (row(b, s), OFF_BQ // wq)),
                  pl.BlockSpec((SEQ_BLOCK, wq), lambda b, s: (row(b, s), OFF_BK // wq)),
                  pl.BlockSpec((SEQ_BLOCK, BW), lambda b, s: (row(b, s), OFF_BV // BW)),
                  pl.BlockSpec((SEQ_BLOCK, BW), lambda b, s: (row(b, s), OFF_BG // BW)),
                  pl.BlockSpec((SEQ_BLOCK, LANES), lambda b, s: (row(b, s), 0)),
                  pl.BlockSpec((None, LANES, wq), lambda b, s: (layer, 0, 0)),
                  pl.BlockSpec((None, 1, wq), lambda b, s: (layer, 0, 0)),
                  pl.BlockSpec((None, 1, GLA_DV), lambda b, s: (layer, 0, 0))],
        out_specs=pl.BlockSpec((SEQ_BLOCK, BW), lambda b, s: (row(b, s), 0)),
        out_shape=jax.ShapeDtypeStruct((t, BW), BF16),
        scratch_shapes=[pltpu.VMEM((GLA_HEADS, GLA_DV, GLA_DK), F32)],
        compiler_params=_cparams(("parallel", "arbitrary")),
        name="gla",
    )(proj, proj, proj, proj, small, wd, b_decay.reshape(depth, 1, wq), norm_g.reshape(depth, 1, GLA_DV))


def _unit_lower_inverse(lows, ri, ci):
    eye = (ri == ci).astype(F32)
    same16 = (ri // 16) == (ci // 16)
    same32 = (ri // 32) == (ci // 32)
    d1 = [jnp.where(same16, low, 0.0) for low in lows]
    d2 = [_dot(a, a) for a in d1]
    m = [eye - a for a in d1]
    d4 = [_dot(a, a) for a in d2]
    m = [a + _dot(a, b) for a, b in zip(m, d2)]
    d8 = [_dot(a, a) for a in d4]
    m = [a + _dot(a, b) for a, b in zip(m, d4)]
    m = [a + _dot(a, b) for a, b in zip(m, d8)]
    c1 = [jnp.where(same32 & ~same16, low, 0.0) for low in lows]
    t = [_dot(a, b) for a, b in zip(m, c1)]
    m = [a - _dot(b, a) for a, b in zip(m, t)]
    c2 = [jnp.where(same32, 0.0, low) for low in lows]
    t = [_dot(a, b) for a, b in zip(m, c2)]
    m = [a - _dot(b, a) for a, b in zip(m, t)]
    return m


def _gdn_kernel(q_ref, k_ref, v_ref, g_ref, sm_ref, cwq_ref, cwk_ref, cwv_ref, alog_ref, dtb_ref,
                ng_ref, o_ref, st_ref, tq_ref, tk_ref, tv_ref):
    @pl.when(pl.program_id(1) == 0)
    def _():
        st_ref[...] = jnp.zeros_like(st_ref)
        tq_ref[...] = jnp.zeros_like(tq_ref)
        tk_ref[...] = jnp.zeros_like(tk_ref)
        tv_ref[...] = jnp.zeros_like(tv_ref)

    n = SEQ_BLOCK
    nh, hd = GDN_HEADS, GDN_HEAD_DIM
    q_raw, k_raw, v_raw = q_ref[...], k_ref[...], v_ref[...]
    qs = _heads(_silu(_causal_conv(q_raw, tq_ref[...], cwq_ref[...])), hd, nh)
    ks = _heads(_silu(_causal_conv(k_raw, tk_ref[...], cwk_ref[...])), hd, nh)
    vs = _heads(_silu(_causal_conv(v_raw, tv_ref[...], cwv_ref[...])), hd, nh)
    tq_ref[...] = q_raw[n - SUBLANES:, :]
    tk_ref[...] = k_raw[n - SUBLANES:, :]
    tv_ref[...] = v_raw[n - SUBLANES:, :]
    sm = sm_ref[...]
    beta_lanes = _sigmoid(sm)
    gcum_lanes = _chunk_cumsum(-jnp.exp(alog_ref[...]) * _softplus(sm + dtb_ref[...]))
    ri, ci, tril, strict = _block_masks(n)

    qs = [q * lax.rsqrt(jnp.sum(q * q, axis=-1, keepdims=True) + NORM_EPS) * (hd ** -0.5) for q in qs]
    ks = [k * lax.rsqrt(jnp.sum(k * k, axis=-1, keepdims=True) + NORM_EPS) for k in ks]
    betas = [beta_lanes[:, LANE_BETA + h:LANE_BETA + h + 1] for h in range(nh)]
    gcums = [gcum_lanes[:, LANE_ALPHA + h:LANE_ALPHA + h + 1] for h in range(nh)]
    g_lasts = [_chunk_last(g) for g in gcums]
    gbs = [jnp.broadcast_to(g, (n, n)) for g in gcums]
    decays = [jnp.where(tril, jnp.exp(jnp.where(tril, gb - gb.T, 0.0)), 0.0) for gb in gbs]
    kks = [_dot_nt(k, k) for k in ks]
    lows = [jnp.where(strict, b * kk * d, 0.0) for b, kk, d in zip(betas, kks, decays)]
    qks = [_dot_nt(q, k) * d for q, k, d in zip(qs, ks, decays)]
    minvs = _unit_lower_inverse(lows, ri, ci)
    e_gs = [jnp.exp(g) for g in gcums]
    uws = [_dot(mi, jnp.concatenate([v * b, k * (b * e)], axis=1))
           for mi, v, k, b, e in zip(minvs, vs, ks, betas, e_gs)]
    ps = [_dot(qk, uw) for qk, uw in zip(qks, uws)]
    q_effs = [q * e - p[:, hd:] for q, e, p in zip(qs, e_gs, ps)]
    k_tails = [k * jnp.exp(gl - g) for k, gl, g in zip(ks, g_lasts, gcums)]
    sts = [st_ref[h] for h in range(nh)]
    for c in range(N_CHUNKS):
        sl = slice(c * CHUNK, (c + 1) * CHUNK)
        outs = [p[sl, :hd] + _dot(qe[sl], st) for p, qe, st in zip(ps, q_effs, sts)]
        abs_ = [_dot_tn(kt[sl], uw[sl]) for kt, uw in zip(k_tails, uws)]
        sts = [st * jnp.exp(gl[c * CHUNK:c * CHUNK + 1, :]) + ab[:, :hd] - _dot(ab[:, hd:], st)
               for gl, st, ab in zip(g_lasts, sts, abs_)]
        for h, o in enumerate(outs):
            cols = slice(h * hd, (h + 1) * hd)
            o = o * lax.rsqrt(jnp.mean(o * o, axis=-1, keepdims=True) + NORM_EPS) * ng_ref[...]
            o_ref[sl, cols] = (o * _silu(g_ref[sl, cols])).astype(o_ref.dtype)
    for h in range(nh):
        st_ref[h] = sts[h]


def _gdn(proj, small, batch, seq, layer, conv_w, a_log, dt_bias, norm_g):
    t = proj.shape[0]
    nsb = seq // SEQ_BLOCK
    hd = GDN_HEAD_DIM
    row = lambda b, s: b * nsb + s
    depth = a_log.shape[0]
    lane_row = lambda a: jnp.zeros((depth, 1, LANES), F32).at[:, 0, LANE_ALPHA:LANE_ALPHA + GDN_HEADS].set(a)
    lspec = pl.BlockSpec((None, 1, LANES), lambda b, s: (layer, 0, 0))
    return pl.pallas_call(
        _gdn_kernel,
        grid=(batch, nsb),
        in_specs=[pl.BlockSpec((SEQ_BLOCK, BW), lambda b, s: (row(b, s), OFF_CQ // BW)),
                  pl.BlockSpec((SEQ_BLOCK, BW), lambda b, s: (row(b, s), OFF_CK // BW)),
                  pl.BlockSpec((SEQ_BLOCK, BW), lambda b, s: (row(b, s), OFF_CV // BW)),
                  pl.BlockSpec((SEQ_BLOCK, BW), lambda b, s: (row(b, s), OFF_CG // BW)),
                  pl.BlockSpec((SEQ_BLOCK, LANES), lambda b, s: (row(b, s), 0)),
                  pl.BlockSpec((None, CONV_WIDTH, BW), lambda b, s: (layer, 0, 0)),
                  pl.BlockSpec((None, CONV_WIDTH, BW), lambda b, s: (layer, 0, 1)),
                  pl.BlockSpec((None, CONV_WIDTH, BW), lambda b, s: (layer, 0, 2)),
                  lspec, lspec,
                  pl.BlockSpec((None, 1, hd), lambda b, s: (layer, 0, 0))],
        out_specs=pl.BlockSpec((SEQ_BLOCK, BW), lambda b, s: (row(b, s), 0)),
        out_shape=jax.ShapeDtypeStruct((t, BW), BF16),
        scratch_shapes=[pltpu.VMEM((GDN_HEADS, hd, hd), F32)] + [pltpu.VMEM((SUBLANES, BW), F32)] * 3,
        compiler_params=_cparams(("parallel", "arbitrary")),
        name="gdn",
    )(proj, proj, proj, proj, small, conv_w, conv_w, conv_w, lane_row(a_log), lane_row(dt_bias),
      norm_g.reshape(depth, 1, hd))


def _mlstm_kernel(q_ref, k_ref, v_ref, og_ref, sm_ref, bi_ref, bf_ref, ng_ref, o_ref,
                  c_ref, n_ref, m_ref):
    @pl.when(pl.program_id(1) == 0)
    def _():
        c_ref[...] = jnp.zeros_like(c_ref)
        n_ref[...] = jnp.zeros_like(n_ref)
        m_ref[...] = jnp.zeros_like(m_ref)

    n = SEQ_BLOCK
    nh, dqk, dv = MLSTM_HEADS, MLSTM_DQK, MLSTM_DV
    hr = range(nh)
    _, _, tril, _ = _block_masks(n)
    qs = _heads(q_ref[...], dqk, nh)
    ks = _heads(k_ref[...] * (MLSTM_DQK ** -0.5), dqk, nh)
    vs = _heads(v_ref[...], dv, nh)
    sm = sm_ref[...]
    ig_lanes = sm + bi_ref[...]
    bcum_lanes = _chunk_cumsum(_log_sigmoid(sm + bf_ref[...]))
    igs = [ig_lanes[:, LANE_I + h:LANE_I + h + 1] for h in hr]
    bcums = [bcum_lanes[:, LANE_F + h:LANE_F + h + 1] for h in hr]
    b_lasts = [_chunk_last(b) for b in bcums]
    logds = [jnp.where(tril, b + _row_form(i - b), -jnp.inf) for b, i in zip(bcums, igs)]
    m_intras = [jnp.max(ld, axis=1, keepdims=True) for ld in logds]
    tails = [bl - b + i for bl, b, i in zip(b_lasts, bcums, igs)]
    spread = lambda xs: jnp.concatenate([jnp.broadcast_to(x, (CHUNK, 1)) for x in xs], axis=0)
    m_prev_rows, a_max_rows, s_olds, s_news = [], [], [], []
    for h in hr:
        m_st = m_ref[h, 0:1, 0:1]
        m_prev, a_max, s_old, s_new = [], [], [], []
        for c in range(N_CHUNKS):
            a_c = jnp.max(tails[h][c * CHUNK:(c + 1) * CHUNK], axis=0, keepdims=True)
            bl_c = bcums[h][(c + 1) * CHUNK - 1:(c + 1) * CHUNK, :]
            m_new = jnp.maximum(bl_c + m_st, a_c)
            m_prev.append(m_st)
            a_max.append(a_c)
            s_old.append(jnp.exp(bl_c + m_st - m_new))
            s_new.append(jnp.exp(a_c - m_new))
            m_st = m_new
        m_ref[h] = jnp.broadcast_to(m_st, m_ref.shape[1:])
        m_prev_rows.append(spread(m_prev))
        a_max_rows.append(spread(a_max))
        s_olds.append(s_old)
        s_news.append(s_new)
    k_ws = [k * jnp.exp(t - a) for k, t, a in zip(ks, tails, a_max_rows)]
    m_inters = [b + mp for b, mp in zip(bcums, m_prev_rows)]
    m_is = [jnp.maximum(mi, mx) for mi, mx in zip(m_inters, m_intras)]
    s_inters = [jnp.exp(mi - m) for mi, m in zip(m_inters, m_is)]
    scs = [_dot_nt(q, k) * jnp.exp(ld - m) for q, k, ld, m in zip(qs, ks, logds, m_is)]
    num_intras = [_dot(sc, v) for sc, v in zip(scs, vs)]
    den_intras = [jnp.sum(sc, axis=1, keepdims=True) for sc in scs]
    floors = [jnp.exp(-m) for m in m_is]
    c_sts = [c_ref[h] for h in hr]
    n_sts = [n_ref[h, 0:1, :] for h in hr]
    for c in range(N_CHUNKS):
        sl = slice(c * CHUNK, (c + 1) * CHUNK)
        nums = [ni[sl] + si[sl] * _dot(q[sl], cs) for ni, si, q, cs in zip(num_intras, s_inters, qs, c_sts)]
        dens = [di[sl] + si[sl] * jnp.sum(q[sl] * ns, axis=1, keepdims=True)
                for di, si, q, ns in zip(den_intras, s_inters, qs, n_sts)]
        kvs = [_dot_tn(kw[sl], v[sl]) for kw, v in zip(k_ws, vs)]
        c_sts = [cs * so[c] + kv * sn[c] for cs, so, kv, sn in zip(c_sts, s_olds, kvs, s_news)]
        n_sts = [ns * so[c] + jnp.sum(kw[sl], axis=0, keepdims=True) * sn[c]
                 for ns, so, kw, sn in zip(n_sts, s_olds, k_ws, s_news)]
        for h in hr:
            cols = slice(h * dv, (h + 1) * dv)
            hh = nums[h] / jnp.maximum(jnp.abs(dens[h]), floors[h][sl])
            hh = hh * lax.rsqrt(jnp.mean(hh * hh, axis=-1, keepdims=True) + NORM_EPS) * ng_ref[...]
            o_ref[sl, cols] = (hh * _sigmoid(og_ref[sl, cols])).astype(o_ref.dtype)
    for h in hr:
        c_ref[h] = c_sts[h]
        n_ref[h] = jnp.broadcast_to(n_sts[h], n_ref.shape[1:])


def _mlstm(proj, small, batch, seq, layer, b_i, b_f, norm_g):
    t = proj.shape[0]
    nsb = seq // SEQ_BLOCK
    nh, dqk, dv = MLSTM_HEADS, MLSTM_DQK, MLSTM_DV
    wq = nh * dqk
    row = lambda b, s: b * nsb + s
    depth = b_i.shape[0]
    lane_row = lambda a, off: jnp.zeros((depth, 1, LANES), F32).at[:, 0, off:off + nh].set(a)
    lspec = pl.BlockSpec((None, 1, LANES), lambda b, s: (layer, 0, 0))
    return pl.pallas_call(
        _mlstm_kernel,
        grid=(batch, nsb),
        in_specs=[pl.BlockSpec((SEQ_BLOCK, wq), lambda b, s: (row(b, s), OFF_DQ // wq)),
                  pl.BlockSpec((SEQ_BLOCK, wq), lambda b, s: (row(b, s), OFF_DK // wq)),
                  pl.BlockSpec((SEQ_BLOCK, BW), lambda b, s: (row(b, s), OFF_DV // BW)),
                  pl.BlockSpec((SEQ_BLOCK, BW), lambda b, s: (row(b, s), OFF_DO // BW)),
                  pl.BlockSpec((SEQ_BLOCK, LANES), lambda b, s: (row(b, s), 0)),
                  lspec, lspec,
                  pl.BlockSpec((None, 1, dv), lambda b, s: (layer, 0, 0))],
        out_specs=pl.BlockSpec((SEQ_BLOCK, BW), lambda b, s: (row(b, s), 0)),
        out_shape=jax.ShapeDtypeStruct((t, BW), BF16),
        scratch_shapes=[pltpu.VMEM((nh, dqk, dv), F32), pltpu.VMEM((nh, SUBLANES, dqk), F32),
                        pltpu.VMEM((nh, SUBLANES, LANES), F32)],
        compiler_params=_cparams(("parallel", "arbitrary")),
        name="mlstm",
    )(proj, proj, proj, proj, small, lane_row(b_i, LANE_I), lane_row(b_f, LANE_F),
      norm_g.reshape(depth, 1, dv))


def _merge_kernel(xn_ref, ya_ref, yb_ref, yc_ref, yd_ref, wg_ref, bg_ref, wb_ref, o_ref):
    xn = xn_ref[...]
    acc = None
    for n, y_ref in enumerate((ya_ref, yb_ref, yc_ref, yd_ref)):
        gate = _sigmoid(_dot(xn, wg_ref[n]) + bg_ref[n])
        term = gate * _dot(y_ref[...], wb_ref[n])
        acc = term if acc is None else acc + term
    o_ref[...] = acc.astype(o_ref.dtype)


def _merge(xn, ys, layer, wg, bg, wb, tm=512, tn=256):
    t, d = xn.shape
    tm = min(tm, t)
    depth, nb = wg.shape[:2]
    yspec = pl.BlockSpec((tm, BW), lambda i, j: (i, 0))
    return pl.pallas_call(
        _merge_kernel,
        grid=(t // tm, d // tn),
        in_specs=[pl.BlockSpec((tm, d), lambda i, j: (i, 0)), yspec, yspec, yspec, yspec,
                  pl.BlockSpec((None, nb, d, tn), lambda i, j: (layer, 0, 0, j)),
                  pl.BlockSpec((None, nb, 1, tn), lambda i, j: (layer, 0, 0, j)),
                  pl.BlockSpec((None, nb, BW, tn), lambda i, j: (layer, 0, 0, j))],
        out_specs=pl.BlockSpec((tm, tn), lambda i, j: (i, j)),
        out_shape=jax.ShapeDtypeStruct((t, d), BF16),
        compiler_params=_cparams(("parallel", "parallel")),
        name="gated_merge",
    )(xn, *ys, wg, bg.reshape(depth, nb, 1, d), wb)


def _router_kernel(x_ref, g_ref, wrh_ref, wrl_ref, br_ref, h_ref, info_ref, idx_ref, cnt_ref, carry_ref):
    @pl.when(pl.program_id(0) == 0)
    def _():
        carry_ref[...] = jnp.zeros_like(carry_ref)

    x = x_ref[...]
    tm = x.shape[0]
    h = x * lax.rsqrt(jnp.mean(x * x, axis=-1, keepdims=True) + NORM_EPS) * g_ref[...]
    h_ref[...] = _pack_pairs(h)
    h_hi = h.astype(BF16)
    h_lo = (h - h_hi.astype(F32)).astype(BF16)
    logits = (_dot(h_hi, wrh_ref[...]) + (_dot(h_lo, wrh_ref[...]) + _dot(h_hi, wrl_ref[...]))
              + br_ref[...])
    lane = _iota((tm, LANES), 1).astype(F32)
    big = float(LANES)
    gl = jnp.where(lane < N_GROUPS, logits, -jnp.inf)
    gmax = jnp.max(gl, axis=1, keepdims=True)
    grp = jnp.min(jnp.where(gl == gmax, lane, big), axis=1, keepdims=True)
    p_grp = 1.0 / jnp.sum(jnp.exp(gl - gmax), axis=1, keepdims=True)
    lo = N_GROUPS + grp * EXPERTS_PER_GROUP
    el = jnp.where((lane >= lo) & (lane < lo + EXPERTS_PER_GROUP), logits, -jnp.inf)
    v0 = jnp.max(el, axis=1, keepdims=True)
    i0 = jnp.min(jnp.where(el == v0, lane, big), axis=1, keepdims=True)
    el = jnp.where(lane == i0, -jnp.inf, el)
    v1 = jnp.max(el, axis=1, keepdims=True)
    i1 = jnp.min(jnp.where(el == v1, lane, big), axis=1, keepdims=True)
    e = jnp.exp(v1 - v0)
    w0 = p_grp / (1.0 + e)
    w1 = p_grp * e / (1.0 + e)
    e0 = i0 - N_GROUPS
    e1 = i1 - N_GROUPS
    oh0 = (lane == e0).astype(F32)
    oh1 = (lane == e1).astype(F32)
    both = oh0 + oh1
    strict = (_iota((tm, tm), 0) > _iota((tm, tm), 1)).astype(BF16)
    before = carry_ref[0:1, :] + _dot(strict, both.astype(BF16))
    r0 = jnp.sum(oh0 * before, axis=1, keepdims=True)
    r1 = jnp.sum(oh1 * before, axis=1, keepdims=True)
    total = carry_ref[0:1, :] + jnp.sum(both, axis=0, keepdims=True)
    carry_ref[...] = jnp.broadcast_to(total, carry_ref.shape)
    cnt_ref[...] = jnp.broadcast_to(total, cnt_ref.shape)
    info_ref[...] = jnp.where(lane == 0.0, w0, jnp.where(lane == 1.0, w1, 0.0))
    idx = jnp.where(lane == 0.0, e0, jnp.where(lane == 1.0, e1,
                                               jnp.where(lane == 2.0, r0, jnp.where(lane == 3.0, r1, 0.0))))
    idx_ref[...] = idx.astype(jnp.int32)


def _router(x2d, layer, g, w_grp, b_grp, w_exp, b_exp, tm=256):
    t, d = x2d.shape
    tm = min(tm, t)
    depth = g.shape[0]
    wr = (jnp.zeros((depth, d, LANES), F32).at[:, :, :N_GROUPS].set(w_grp)
          .at[:, :, N_GROUPS:N_GROUPS + N_EXPERTS].set(w_exp))
    br = (jnp.zeros((depth, 1, LANES), F32).at[:, 0, :N_GROUPS].set(b_grp)
          .at[:, 0, N_GROUPS:N_GROUPS + N_EXPERTS].set(b_exp))
    wr_hi = wr.astype(BF16)
    wr_lo = (wr - wr_hi.astype(F32)).astype(BF16)
    return pl.pallas_call(
        _router_kernel,
        grid=(t // tm,),
        in_specs=[pl.BlockSpec((tm, d), lambda i: (i, 0)),
                  pl.BlockSpec((None, 1, d), lambda i: (layer, 0, 0)),
                  pl.BlockSpec((None, d, LANES), lambda i: (layer, 0, 0)),
                  pl.BlockSpec((None, d, LANES), lambda i: (layer, 0, 0)),
                  pl.BlockSpec((None, 1, LANES), lambda i: (layer, 0, 0))],
        out_specs=[pl.BlockSpec((tm, d // 2), lambda i: (i, 0)),
                   pl.BlockSpec((tm, LANES), lambda i: (i, 0)),
                   pl.BlockSpec((tm, LANES), lambda i: (i, 0)),
                   pl.BlockSpec((SUBLANES, LANES), lambda i: (0, 0))],
        out_shape=[jax.ShapeDtypeStruct((t, d // 2), jnp.uint32),
                   jax.ShapeDtypeStruct((t, LANES), F32),
                   jax.ShapeDtypeStruct((t, LANES), jnp.int32),
                   jax.ShapeDtypeStruct((SUBLANES, LANES), F32)],
        scratch_shapes=[pltpu.VMEM((SUBLANES, LANES), F32)],
        compiler_params=_cparams(("arbitrary",)),
        name="router",
    )(x2d, g.reshape(depth, 1, d), wr_hi, wr_lo, br)


def _dispatch_kernel(pos_ref, zrow_ref, h_ref, xs_ref, zero_ref, sem, zsem):
    i = pl.program_id(0)
    tm = h_ref.shape[0]
    zt = zero_ref.shape[0]

    @pl.when(i == 0)
    def _():
        zero_ref[...] = jnp.zeros_like(zero_ref)

        def zero_copy(e):
            start = pl.multiple_of(jnp.maximum(zrow_ref[e], 0), zt)
            return pltpu.make_async_copy(zero_ref, xs_ref.at[pl.ds(start, zt)], zsem)

        def zissue(e, carry):
            @pl.when(zrow_ref[e] >= 0)
            def _():
                zero_copy(e).start()
            return carry

        def zdrain(e, carry):
            @pl.when(zrow_ref[e] >= 0)
            def _():
                zero_copy(e).wait()
            return carry

        lax.fori_loop(0, zrow_ref.shape[0], zissue, 0)
        lax.fori_loop(0, zrow_ref.shape[0], zdrain, 0)

    def row_copy(r, p):
        return pltpu.make_async_copy(h_ref.at[pl.ds(r, 1)], xs_ref.at[pl.ds(p, 1)], sem)

    def issue(r, carry):
        t = i * tm + r
        for k in range(2):
            row_copy(r, pos_ref[2 * t + k]).start()
        return carry

    lax.fori_loop(0, tm, issue, 0)
    for k in range(2):
        pltpu.make_async_copy(h_ref, xs_ref.at[pl.ds(0, tm)], sem).wait()


def _dispatch(pos, zrow, h, n_rows, tm=ROW_TILE):
    t, d = h.shape
    tm = min(tm, t)
    return pl.pallas_call(
        _dispatch_kernel,
        grid_spec=pltpu.PrefetchScalarGridSpec(
            num_scalar_prefetch=2,
            grid=(t // tm,),
            in_specs=[pl.BlockSpec((tm, d), lambda i, pos, zrow: (i, 0))],
            out_specs=pl.BlockSpec(memory_space=pl.ANY),
            scratch_shapes=[pltpu.VMEM((EXPERT_TILE, d), h.dtype), pltpu.SemaphoreType.DMA(()),
                            pltpu.SemaphoreType.DMA(())],
        ),
        out_shape=jax.ShapeDtypeStruct((n_rows, d), h.dtype),
        compiler_params=_cparams(("arbitrary",)),
        name="dispatch",
    )(pos, zrow, h)


def _combine_kernel(keep_x, pos_ref, x_ref, info_ref, g_ref, ys_ref, *rest):
    outs, (buf_ref, sem) = rest[:-2], rest[-2:]
    i = pl.program_id(0)
    n = pl.num_programs(0)
    tm = x_ref.shape[0]
    half = x_ref.shape[1] // 2

    def row_copy(slot, r, k, p):
        return pltpu.make_async_copy(ys_ref.at[pl.ds(p, 1)], buf_ref.at[slot, k, pl.ds(r, 1)],
                                     sem.at[slot])

    def gather_tile(tile, slot):
        def issue(r, carry):
            t = tile * tm + r
            for k in range(2):
                row_copy(slot, r, k, pos_ref[2 * t + k]).start()
            return carry
        lax.fori_loop(0, tm, issue, 0)

    @pl.when(i == 0)
    def _():
        gather_tile(0, 0)

    @pl.when(i + 1 < n)
    def _():
        gather_tile(i + 1, (i + 1) % 2)

    slot = i % 2

    for k in range(2):
        pltpu.make_async_copy(ys_ref.at[pl.ds(0, tm)], buf_ref.at[slot, k], sem.at[slot]).wait()
    info = info_ref[...]
    w0, w1 = info[:, 0:1], info[:, 1:2]
    lo0, hi0 = _unpack_pairs(buf_ref[slot, 0])
    lo1, hi1 = _unpack_pairs(buf_ref[slot, 1])
    x = x_ref[...]
    x_lo = x[:, :half] + w0 * lo0 + w1 * lo1
    x_hi = x[:, half:] + w0 * hi0 + w1 * hi1
    ms = (jnp.sum(x_lo * x_lo, axis=-1, keepdims=True)
          + jnp.sum(x_hi * x_hi, axis=-1, keepdims=True)) / x.shape[1]
    inv = lax.rsqrt(ms + NORM_EPS)
    g = g_ref[...]
    if keep_x:
        outs[0][:, :half] = x_lo
        outs[0][:, half:] = x_hi
    outs[-1][:, :half] = (x_lo * inv * g[:, :half]).astype(outs[-1].dtype)
    outs[-1][:, half:] = (x_hi * inv * g[:, half:]).astype(outs[-1].dtype)


def _combine(pos, x2d, info, ys, g, keep_x, norm_dtype, tm=ROW_TILE):
    t, d = x2d.shape
    tm = min(tm, t)
    row_spec = pl.BlockSpec((tm, d), lambda i, pos: (i, 0))
    out_specs = [row_spec] * (2 if keep_x else 1)
    out_shape = ([jax.ShapeDtypeStruct((t, d), F32)] if keep_x else []) + [jax.ShapeDtypeStruct((t, d), norm_dtype)]
    return pl.pallas_call(
        functools.partial(_combine_kernel, keep_x),
        grid_spec=pltpu.PrefetchScalarGridSpec(
            num_scalar_prefetch=1,
            grid=(t // tm,),
            in_specs=[row_spec,
                      pl.BlockSpec((tm, LANES), lambda i, pos: (i, 0)),
                      pl.BlockSpec((1, d), lambda i, pos: (0, 0)),
                      pl.BlockSpec(memory_space=pl.ANY)],
            out_specs=out_specs,
            scratch_shapes=[pltpu.VMEM((2, 2, tm, d // 2), ys.dtype),
                            pltpu.SemaphoreType.DMA((2,))],
        ),
        out_shape=out_shape,
        compiler_params=_cparams(("arbitrary",)),
        name="combine",
    )(pos, x2d, info, g.reshape(1, d), ys)


def _expert_kernel(te_ref, nu_ref, xs_ref, wg_ref, wu_ref, wd_ref, ys_ref):
    del te_ref
    i = pl.program_id(0)

    @pl.when(i < nu_ref[0])
    def _():
        lo, hi = _unpack_pairs(xs_ref[...])
        x = jnp.concatenate([lo.astype(BF16), hi.astype(BF16)], axis=1)
        a = _dot(x, wg_ref[...])
        u = _dot(x, wu_ref[...])
        act = (_silu(a) * u).astype(BF16)
        ys_ref[...] = _pack_pairs(_dot(act, wd_ref[...]))

    @pl.when(i >= nu_ref[0])
    def _():
        ys_ref[...] = jnp.zeros_like(ys_ref)


def _experts(tile_expert, n_used, xs, wg, wu, wd):
    p, dh = xs.shape
    d, f = wg.shape[1:]
    tm = EXPERT_TILE
    src_tile = lambda i, te, nu: (jnp.maximum(jnp.minimum(i, nu[0] - 1), 0), 0)
    return pl.pallas_call(
        _expert_kernel,
        grid_spec=pltpu.PrefetchScalarGridSpec(
            num_scalar_prefetch=2,
            grid=(p // tm,),
            in_specs=[pl.BlockSpec((tm, dh), src_tile),
                      pl.BlockSpec((None, d, f), lambda i, te, nu: (te[i], 0, 0)),
                      pl.BlockSpec((None, d, f), lambda i, te, nu: (te[i], 0, 0)),
                      pl.BlockSpec((None, f, d), lambda i, te, nu: (te[i], 0, 0))],
            out_specs=pl.BlockSpec((tm, dh), lambda i, te, nu: (i, 0)),
        ),
        out_shape=jax.ShapeDtypeStruct((p, dh), xs.dtype),
        compiler_params=_cparams(("arbitrary",)),
        name="experts",
    )(tile_expert, n_used, xs, wg, wu, wd)


def _hier_moe(x2d, layer, ffn_g, w_grp, b_grp, w_exp, b_exp, wg, wu, wd, next_g, keep_x, norm_dtype):
    t, d = x2d.shape
    tm = EXPERT_TILE
    n_rows = 2 * t + N_EXPERTS * tm
    h, info, idx, cnt = _router(x2d, layer, ffn_g, w_grp, b_grp, w_exp, b_exp)
    counts = cnt[0, :N_EXPERTS].astype(jnp.int32)
    padded = ((counts + tm - 1) // tm) * tm
    ends = jnp.cumsum(padded)
    offs = ends - padded
    pos = (offs[idx[:, 0:2]] + idx[:, 2:4]).reshape(2 * t)
    n_used = (ends[-1] // tm).reshape(1).astype(jnp.int32)
    tail = n_used[0] + jnp.arange(N_EXPERTS, dtype=jnp.int32)
    zrow = jnp.concatenate([jnp.where(padded > 0, ends - tm, -1),
                            jnp.where(tail < n_rows // tm, tail * tm, -1)]).astype(jnp.int32)
    tiles = jnp.minimum(jnp.arange(n_rows // tm, dtype=jnp.int32), n_used[0] - 1)
    tile_expert = jnp.sum(((ends // tm)[None, :] <= tiles[:, None]).astype(jnp.int32), axis=1)
    tile_expert = jnp.minimum(tile_expert, N_EXPERTS - 1) + layer * N_EXPERTS
    xs = _dispatch(pos, zrow, h, n_rows)
    ys = _experts(tile_expert, n_used, xs, wg, wu, wd)
    return _combine(pos, x2d, info, ys, next_g, keep_x, norm_dtype)


def _repack_kernel(w_ref, big_ref, small_ref):
    c0 = 5 * BW
    c1 = c0 + GLA_RANK
    c2 = c1 + 4 * BW
    c3 = c2 + 2 * GDN_HEADS
    c4 = c3 + 3 * BW
    c5 = c4 + 2 * MLSTM_HEADS
    big_ref[:, 0:c0] = w_ref[:, 0:c0].astype(BF16)
    big_ref[:, c0:c0 + 4 * BW] = w_ref[:, c1:c2].astype(BF16)
    big_ref[:, c0 + 4 * BW:] = w_ref[:, c3:c4].astype(BF16)
    n_small = GLA_RANK + 2 * GDN_HEADS + 2 * MLSTM_HEADS
    small = jnp.concatenate([w_ref[:, c0:c1], w_ref[:, c2:c3], w_ref[:, c4:c5],
                             jnp.zeros((w_ref.shape[0], LANES - n_small), F32)], axis=1)
    small_ref[...] = small.astype(BF16)


def _split_w_in(w, tk=256):
    depth, k, n = w.shape
    return pl.pallas_call(
        _repack_kernel,
        grid=(depth, k // tk),
        in_specs=[pl.BlockSpec((None, tk, n), lambda l, i: (l, i, 0))],
        out_specs=[pl.BlockSpec((None, tk, PROJ_BIG), lambda l, i: (l, i, 0)),
                   pl.BlockSpec((None, tk, LANES), lambda l, i: (l, i, 0))],
        out_shape=[jax.ShapeDtypeStruct((depth, k, PROJ_BIG), BF16),
                   jax.ShapeDtypeStruct((depth, k, LANES), BF16)],
        compiler_params=_cparams(("parallel", "parallel")),
        name="repack_w_in",
    )(w)


def kernel(x, mix_norm, w_in, lru_conv_w, lru_conv_b, lru_w_a, lru_b_a, lru_w_x, lru_b_x, lru_lambda,
           gla_w_decay, gla_b_decay, gla_norm, gdn_conv_w, gdn_a_log, gdn_dt_bias, gdn_norm,
           mlstm_b_i, mlstm_b_f, mlstm_norm, w_branch, w_merge_gate, b_merge_gate, w_out,
           ffn_norm, w_group_router, b_group_router, w_expert_router, b_expert_router,
           w_exp_gate, w_exp_up, w_exp_down, final_norm):
    batch, seq, d = x.shape
    depth = w_in.shape[0]
    x2d = x.reshape(batch * seq, d)
    w_big, w_small = _split_w_in(w_in)
    wg_merge = w_merge_gate.astype(BF16)
    wb_merge = w_branch.astype(BF16)
    w_o = w_out.astype(BF16)
    f = w_exp_gate.shape[-1]
    we_gate = w_exp_gate.astype(BF16).reshape(depth * N_EXPERTS, d, f)
    we_up = w_exp_up.astype(BF16).reshape(depth * N_EXPERTS, d, f)
    we_down = w_exp_down.astype(BF16).reshape(depth * N_EXPERTS, f, d)
    xn = _rmsnorm(x2d, mix_norm[0], BF16)
    for l in range(depth):
        last = l == depth - 1
        proj = _matmul(xn, w_big, l, tm=1024, name="in_proj")
        small = _matmul(xn, w_small, l, name="in_proj_small")
        y_a = _lru(proj, batch, seq, l, lru_conv_w, lru_conv_b, lru_w_a, lru_b_a, lru_w_x, lru_b_x,
                   lru_lambda)
        y_b = _gla(proj, small, batch, seq, l, gla_w_decay, gla_b_decay, gla_norm)
        y_c = _gdn(proj, small, batch, seq, l, gdn_conv_w, gdn_a_log, gdn_dt_bias, gdn_norm)
        y_d = _mlstm(proj, small, batch, seq, l, mlstm_b_i, mlstm_b_f, mlstm_norm)
        merged = _merge(xn, (y_a, y_b, y_c, y_d), l, wg_merge, b_merge_gate, wb_merge)
        x2d = _matmul(merged, w_o, l, residual=x2d, tm=1024, name="out_proj")
        outs = _hier_moe(x2d, l, ffn_norm, w_group_router, b_group_router, w_expert_router,
                         b_expert_router, we_gate, we_up, we_down,
                         final_norm if last else mix_norm[l + 1], not last, F32 if last else BF16)
        if last:
            (y,) = outs
        else:
            x2d, xn = outs
    return y.reshape(batch, seq, d)
```

```python
import functools

import jax
import jax.numpy as jnp
from jax import lax
from jax.experimental import pallas as pl
from jax.experimental.pallas import tpu as pltpu

F32 = jnp.float32
BF16 = jnp.bfloat16
HI = lax.Precision.HIGHEST

D_MODEL = 4096
BW = D_MODEL // 4
CONV_WIDTH = 4
CHUNK = 64
NORM_EPS = 1e-6
LRU_BLOCKS = 8
LRU_BLOCK = BW // LRU_BLOCKS
LRU_C = 8.0
GLA_HEADS = 4
GLA_DV = BW // GLA_HEADS
GLA_DK = GLA_DV // 2
GLA_RANK = 16
GLA_TAU = 16.0
GDN_HEAD_DIM = 128
GDN_HEADS = BW // GDN_HEAD_DIM
MLSTM_HEADS = 4
MLSTM_DV = BW // MLSTM_HEADS
MLSTM_DQK = MLSTM_DV // 2
N_GROUPS = 4
EXPERTS_PER_GROUP = 8
N_EXPERTS = N_GROUPS * EXPERTS_PER_GROUP
D_EXPERT = D_MODEL // 8

LANES = 128
SUBLANES = 8
VMEM_LIMIT = 56 * 1024 * 1024

PROJ_BIG = 12 * BW
OFF_AX, OFF_AG = 0, BW
OFF_BQ, OFF_BK, OFF_BV, OFF_BG = 2 * BW, 2 * BW + 512, 3 * BW, 4 * BW
OFF_CQ, OFF_CK, OFF_CV, OFF_CG = 5 * BW, 6 * BW, 7 * BW, 8 * BW
OFF_DQ, OFF_DK, OFF_DV, OFF_DO = 9 * BW, 9 * BW + 512, 10 * BW, 11 * BW
LANE_DECAY, LANE_BETA, LANE_ALPHA, LANE_I, LANE_F = 0, 16, 24, 32, 36

SEQ_BLOCK = 256
N_CHUNKS = SEQ_BLOCK // CHUNK
EXPERT_TILE = 256
ROW_TILE = 256


def _cparams(sem):
    return pltpu.CompilerParams(dimension_semantics=sem, vmem_limit_bytes=VMEM_LIMIT)


def _softplus(x):
    return jnp.maximum(x, 0.0) + jnp.log1p(jnp.exp(-jnp.abs(x)))


def _log_sigmoid(x):
    return -_softplus(-x)


def _sigmoid(x):
    return 1.0 / (1.0 + jnp.exp(-x))


def _silu(x):
    return x * _sigmoid(x)


def _gelu_tanh(x):
    return 0.5 * x * (1.0 + jnp.tanh(0.7978845608028654 * (x + 0.044715 * (x * x * x))))


def _dot(a, b, precision=None):
    return jnp.dot(a, b, preferred_element_type=F32, precision=precision)


def _dot_nt(a, b, precision=None):
    return lax.dot_general(a, b, (((1,), (1,)), ((), ())), preferred_element_type=F32,
                           precision=precision)


def _dot_tn(a, b, precision=None):
    return lax.dot_general(a, b, (((0,), (0,)), ((), ())), preferred_element_type=F32,
                           precision=precision)


def _iota(shape, dim):
    return lax.broadcasted_iota(jnp.int32, shape, dim)


def _shift_rows(x, tail, s):
    r = pltpu.roll(x, s, 0)
    rt = pltpu.roll(tail, s, 0)
    head = jnp.where(_iota(tail.shape, 0) < s, rt, r[0:SUBLANES])
    return jnp.concatenate([head, r[SUBLANES:]], axis=0)


def _causal_conv(x, tail, w):
    y = x * w[CONV_WIDTH - 1:CONV_WIDTH, :]
    for s in range(1, CONV_WIDTH):
        y = y + _shift_rows(x, tail, s) * w[CONV_WIDTH - 1 - s:CONV_WIDTH - s, :]
    return y


def _chunk_cumsum(x):
    pos = _iota(x.shape, 0) & (CHUNK - 1)
    s = 1
    while s < CHUNK:
        x = x + jnp.where(pos >= s, pltpu.roll(x, s, 0), 0.0)
        s *= 2
    return x


def _chunk_last(x):
    return jnp.concatenate(
        [jnp.broadcast_to(x[(c + 1) * CHUNK - 1:(c + 1) * CHUNK], (CHUNK, x.shape[1]))
         for c in range(x.shape[0] // CHUNK)], axis=0)


def _row_form(col):
    n = col.shape[0]
    return jnp.broadcast_to(col, (n, n)).T


def _block_masks(n):
    ri = _iota((n, n), 0)
    ci = _iota((n, n), 1)
    same = (ri // CHUNK) == (ci // CHUNK)
    return ri, ci, same & (ri >= ci), same & (ri > ci)


def _pack_pairs(x):
    c = x.shape[1] // 2
    bits = lambda v: lax.bitcast_convert_type(v.astype(BF16).astype(F32), jnp.uint32)
    return (bits(x[:, :c]) >> 16) | bits(x[:, c:])


def _unpack_pairs(p):
    lo = lax.bitcast_convert_type(p << 16, F32)
    hi = lax.bitcast_convert_type(p & jnp.uint32(0xFFFF0000), F32)
    return lo, hi


def _heads(x, width, count):
    return [x[:, h * width:(h + 1) * width] for h in range(count)]


def _rmsnorm_kernel(x_ref, g_ref, o_ref):
    x = x_ref[...]
    ms = jnp.mean(x * x, axis=-1, keepdims=True)
    o_ref[...] = (x * lax.rsqrt(ms + NORM_EPS) * g_ref[...]).astype(o_ref.dtype)


def _rmsnorm(x2d, g, out_dtype, tm=512):
    t, d = x2d.shape
    return pl.pallas_call(
        _rmsnorm_kernel,
        grid=(t // tm,),
        in_specs=[pl.BlockSpec((tm, d), lambda i: (i, 0)),
                  pl.BlockSpec((1, d), lambda i: (0, 0))],
        out_specs=pl.BlockSpec((tm, d), lambda i: (i, 0)),
        out_shape=jax.ShapeDtypeStruct((t, d), out_dtype),
        compiler_params=_cparams(("parallel",)),
        name="rmsnorm",
    )(x2d, g.reshape(1, d))


def _mm_kernel(a_ref, w_ref, o_ref):
    o_ref[...] = _dot(a_ref[...], w_ref[...]).astype(o_ref.dtype)


def _mm_res_kernel(a_ref, w_ref, r_ref, o_ref):
    o_ref[...] = r_ref[...] + _dot(a_ref[...], w_ref[...])


def _matmul(a, w, layer, out_dtype=F32, residual=None, tm=512, tn=512, name="matmul"):
    m, k = a.shape
    n = w.shape[2]
    tm, tn = min(tm, m), min(tn, n)
    in_specs = [pl.BlockSpec((tm, k), lambda i, j: (i, 0)),
                pl.BlockSpec((None, k, tn), lambda i, j: (layer, 0, j))]
    args = [a, w]
    body = _mm_kernel
    if residual is not None:
        in_specs.append(pl.BlockSpec((tm, tn), lambda i, j: (i, j)))
        args.append(residual)
        body = _mm_res_kernel
    return pl.pallas_call(
        body,
        grid=(m // tm, n // tn),
        in_specs=in_specs,
        out_specs=pl.BlockSpec((tm, tn), lambda i, j: (i, j)),
        out_shape=jax.ShapeDtypeStruct((m, n), out_dtype),
        compiler_params=_cparams(("parallel", "parallel")),
        name=name,
    )(*args)


def _lru_kernel(x_ref, gate_ref, cw_ref, cb_ref, wa_ref, ba_ref, wx_ref, bx_ref, lam_ref,
                o_ref, tail_ref, h_ref):
    @pl.when(pl.program_id(1) == 0)
    def _():
        tail_ref[...] = jnp.zeros_like(tail_ref)
        h_ref[...] = jnp.zeros_like(h_ref)

    x = x_ref[...]
    n = x.shape[0]
    u = _causal_conv(x, tail_ref[...], cw_ref[...]) + cb_ref[...]
    tail_ref[...] = x[n - SUBLANES:, :]
    ubs = _heads(u, LRU_BLOCK, LRU_BLOCKS)
    r = jnp.concatenate([_dot(ub, wa_ref[b]) for b, ub in enumerate(ubs)], axis=1)
    i = jnp.concatenate([_dot(ub, wx_ref[b]) for b, ub in enumerate(ubs)], axis=1)
    r = _sigmoid(r + ba_ref[...])
    i = _sigmoid(i + bx_ref[...])
    log_a = (-LRU_C * r) * _softplus(-lam_ref[...])
    a = jnp.exp(log_a)
    xin = jnp.sqrt(-jnp.tanh(log_a) * (a * a + 1.0)) * (i * u)
    row = _iota(a.shape, 0)
    s = 1
    while s < n:
        keep = row >= s
        a_sh = jnp.where(keep, pltpu.roll(a, s, 0), 1.0)
        x_sh = jnp.where(keep, pltpu.roll(xin, s, 0), 0.0)
        xin = a * x_sh + xin
        a = a * a_sh
        s *= 2
    h = xin + a * h_ref[0:1, :]
    h_ref[...] = jnp.broadcast_to(h[n - 1:n, :], h_ref.shape)
    o_ref[...] = (h * _gelu_tanh(gate_ref[...])).astype(o_ref.dtype)


def _lru(proj, batch, seq, layer, cw, cb, wa, ba, wx, bx, lam):
    t = proj.shape[0]
    nsb = seq // SEQ_BLOCK
    row = lambda b, s: b * nsb + s
    vec = lambda a: a.reshape(a.shape[0], 1, BW)
    vspec = pl.BlockSpec((None, 1, BW), lambda b, s: (layer, 0, 0))
    wspec = pl.BlockSpec((None, LRU_BLOCKS, LRU_BLOCK, LRU_BLOCK), lambda b, s: (layer, 0, 0, 0))
    return pl.pallas_call(
        _lru_kernel,
        grid=(batch, nsb),
        in_specs=[pl.BlockSpec((SEQ_BLOCK, BW), lambda b, s: (row(b, s), OFF_AX // BW)),
                  pl.BlockSpec((SEQ_BLOCK, BW), lambda b, s: (row(b, s), OFF_AG // BW)),
                  pl.BlockSpec((None, CONV_WIDTH, BW), lambda b, s: (layer, 0, 0)),
                  vspec, wspec, vspec, wspec, vspec, vspec],
        out_specs=pl.BlockSpec((SEQ_BLOCK, BW), lambda b, s: (row(b, s), 0)),
        out_shape=jax.ShapeDtypeStruct((t, BW), BF16),
        scratch_shapes=[pltpu.VMEM((SUBLANES, BW), F32), pltpu.VMEM((SUBLANES, BW), F32)],
        compiler_params=_cparams(("parallel", "arbitrary")),
        name="rg_lru",
    )(proj, proj, cw, vec(cb), wa, vec(ba), wx, vec(bx), vec(lam))


def _gla_kernel(q_ref, k_ref, v_ref, g_ref, sm_ref, wd_ref, bd_ref, ng_ref, o_ref, st_ref):
    @pl.when(pl.program_id(1) == 0)
    def _():
        st_ref[...] = jnp.zeros_like(st_ref)

    nh, dk, dv = GLA_HEADS, GLA_DK, GLA_DV
    _, _, tril, _ = _block_masks(SEQ_BLOCK)
    log_alpha = _log_sigmoid(_dot(sm_ref[...], wd_ref[...], HI) + bd_ref[...]) / GLA_TAU
    bcum = _chunk_cumsum(log_alpha)
    b_last = _chunk_last(bcum)
    k = k_ref[...]
    q_decs = _heads(q_ref[...] * (GLA_DK ** -0.5) * jnp.exp(bcum), dk, nh)
    k_decs = _heads(k * jnp.exp(-bcum), dk, nh)
    k_tails = _heads(k * jnp.exp(b_last - bcum), dk, nh)
    b_lasts = _heads(b_last, dk, nh)
    vs = _heads(v_ref[...], dv, nh)
    attns = [jnp.where(tril, _dot_nt(qd, kd), 0.0) for qd, kd in zip(q_decs, k_decs)]
    o_intras = [_dot(a, v) for a, v in zip(attns, vs)]
    sts = [st_ref[h] for h in range(nh)]
    for c in range(N_CHUNKS):
        sl = slice(c * CHUNK, (c + 1) * CHUNK)
        outs = [oi[sl] + _dot_nt(qd[sl], st) for oi, qd, st in zip(o_intras, q_decs, sts)]
        kvs = [_dot_tn(v[sl], kt[sl]) for v, kt in zip(vs, k_tails)]
        sts = [st * jnp.exp(bl[c * CHUNK:c * CHUNK + 1, :]) + kv for st, bl, kv in zip(sts, b_lasts, kvs)]
        for h, o in enumerate(outs):
            cols = slice(h * dv, (h + 1) * dv)
            o = o * lax.rsqrt(jnp.mean(o * o, axis=-1, keepdims=True) + NORM_EPS) * ng_ref[...]
            o_ref[sl, cols] = (o * _silu(g_ref[sl, cols])).astype(o_ref.dtype)
    for h in range(nh):
        st_ref[h] = sts[h]


def _gla(proj, small, batch, seq, layer, w_decay, b_decay, norm_g):
    t = proj.shape[0]
    nsb = seq // SEQ_BLOCK
    wq = GLA_HEADS * GLA_DK
    row = lambda b, s: b * nsb + s
    depth = w_decay.shape[0]
    wd = jnp.zeros((depth, LANES, wq), F32).at[:, LANE_DECAY:LANE_DECAY + GLA_RANK].set(w_decay)
    return pl.pallas_call(
        _gla_kernel,
        grid=(batch, nsb),
        in_specs=[pl.BlockSpec((SEQ_BLOCK, wq), lambda b, s: (row(b, s), OFF_BQ // wq)),
                  pl.BlockSpec((SEQ_BLOCK, wq), lambda b, s: (row(b, s), OFF_BK // wq)),
                  pl.BlockSpec((SEQ_BLOCK, BW), lambda b, s: (row(b, s), OFF_BV // BW)),
                  pl.BlockSpec((SEQ_BLOCK, BW), lambda b, s: (row(b, s), OFF_BG // BW)),
                  pl.BlockSpec((SEQ_BLOCK, LANES), lambda b, s: (row(b, s), 0)),
                  pl.BlockSpec((None, LANES, wq), lambda b, s: (layer, 0, 0)),
                  pl.BlockSpec((None, 1, wq), lambda b, s: (layer, 0, 0)),
                  pl.BlockSpec((None, 1, GLA_DV), lambda b, s: (layer, 0, 0))],
        out_specs=pl.BlockSpec((SEQ_BLOCK, BW), lambda b, s: (row(b, s), 0)),
        out_shape=jax.ShapeDtypeStruct((t, BW), BF16),
        scratch_shapes=[pltpu.VMEM((GLA_HEADS, GLA_DV, GLA_DK), F32)],
        compiler_params=_cparams(("parallel", "arbitrary")),
        name="gla",
    )(proj, proj, proj, proj, small, wd, b_decay.reshape(depth, 1, wq), norm_g.reshape(depth, 1, GLA_DV))


def _unit_lower_inverse(lows, ri, ci):
    eye = (ri == ci).astype(F32)
    same16 = (ri // 16) == (ci // 16)
    same32 = (ri // 32) == (ci // 32)
    d1 = [jnp.where(same16, low, 0.0) for low in lows]
    d2 = [_dot(a, a) for a in d1]
    m = [eye - a for a in d1]
    d4 = [_dot(a, a) for a in d2]
    m = [a + _dot(a, b) for a, b in zip(m, d2)]
    d8 = [_dot(a, a) for a in d4]
    m = [a + _dot(a, b) for a, b in zip(m, d4)]
    m = [a + _dot(a, b) for a, b in zip(m, d8)]
    c1 = [jnp.where(same32 & ~same16, low, 0.0) for low in lows]
    t = [_dot(a, b) for a, b in zip(m, c1)]
    m = [a - _dot(b, a) for a, b in zip(m, t)]
    c2 = [jnp.where(same32, 0.0, low) for low in lows]
    t = [_dot(a, b) for a, b in zip(m, c2)]
    m = [a - _dot(b, a) for a, b in zip(m, t)]
    return m


def _gdn_kernel(q_ref, k_ref, v_ref, g_ref, sm_ref, cwq_ref, cwk_ref, cwv_ref, alog_ref, dtb_ref,
                ng_ref, o_ref, st_ref, tq_ref, tk_ref, tv_ref):
    @pl.when(pl.program_id(1) == 0)
    def _():
        st_ref[...] = jnp.zeros_like(st_ref)
        tq_ref[...] = jnp.zeros_like(tq_ref)
        tk_ref[...] = jnp.zeros_like(tk_ref)
        tv_ref[...] = jnp.zeros_like(tv_ref)

    n = SEQ_BLOCK
    nh, hd = GDN_HEADS, GDN_HEAD_DIM
    q_raw, k_raw, v_raw = q_ref[...], k_ref[...], v_ref[...]
    qs = _heads(_silu(_causal_conv(q_raw, tq_ref[...], cwq_ref[...])), hd, nh)
    ks = _heads(_silu(_causal_conv(k_raw, tk_ref[...], cwk_ref[...])), hd, nh)
    vs = _heads(_silu(_causal_conv(v_raw, tv_ref[...], cwv_ref[...])), hd, nh)
    tq_ref[...] = q_raw[n - SUBLANES:, :]
    tk_ref[...] = k_raw[n - SUBLANES:, :]
    tv_ref[...] = v_raw[n - SUBLANES:, :]
    sm = sm_ref[...]
    beta_lanes = _sigmoid(sm)
    gcum_lanes = _chunk_cumsum(-jnp.exp(alog_ref[...]) * _softplus(sm + dtb_ref[...]))
    ri, ci, tril, strict = _block_masks(n)

    qs = [q * lax.rsqrt(jnp.sum(q * q, axis=-1, keepdims=True) + NORM_EPS) * (hd ** -0.5) for q in qs]
    ks = [k * lax.rsqrt(jnp.sum(k * k, axis=-1, keepdims=True) + NORM_EPS) for k in ks]
    betas = [beta_lanes[:, LANE_BETA + h:LANE_BETA + h + 1] for h in range(nh)]
    gcums = [gcum_lanes[:, LANE_ALPHA + h:LANE_ALPHA + h + 1] for h in range(nh)]
    g_lasts = [_chunk_last(g) for g in gcums]
    gbs = [jnp.broadcast_to(g, (n, n)) for g in gcums]
    decays = [jnp.where(tril, jnp.exp(jnp.where(tril, gb - gb.T, 0.0)), 0.0) for gb in gbs]
    kks = [_dot_nt(k, k) for k in ks]
    lows = [jnp.where(strict, b * kk * d, 0.0) for b, kk, d in zip(betas, kks, decays)]
    qks = [_dot_nt(q, k) * d for q, k, d in zip(qs, ks, decays)]
    minvs = _unit_lower_inverse(lows, ri, ci)
    e_gs = [jnp.exp(g) for g in gcums]
    uws = [_dot(mi, jnp.concatenate([v * b, k * (b * e)], axis=1))
           for mi, v, k, b, e in zip(minvs, vs, ks, betas, e_gs)]
    ps = [_dot(qk, uw) for qk, uw in zip(qks, uws)]
    q_effs = [q * e - p[:, hd:] for q, e, p in zip(qs, e_gs, ps)]
    k_tails = [k * jnp.exp(gl - g) for k, gl, g in zip(ks, g_lasts, gcums)]
    sts = [st_ref[h] for h in range(nh)]
    for c in range(N_CHUNKS):
        sl = slice(c * CHUNK, (c + 1) * CHUNK)
        outs = [p[sl, :hd] + _dot(qe[sl], st) for p, qe, st in zip(ps, q_effs, sts)]
        abs_ = [_dot_tn(kt[sl], uw[sl]) for kt, uw in zip(k_tails, uws)]
        sts = [st * jnp.exp(gl[c * CHUNK:c * CHUNK + 1, :]) + ab[:, :hd] - _dot(ab[:, hd:], st)
               for gl, st, ab in zip(g_lasts, sts, abs_)]
        for h, o in enumerate(outs):
            cols = slice(h * hd, (h + 1) * hd)
            o = o * lax.rsqrt(jnp.mean(o * o, axis=-1, keepdims=True) + NORM_EPS) * ng_ref[...]
            o_ref[sl, cols] = (o * _silu(g_ref[sl, cols])).astype(o_ref.dtype)
    for h in range(nh):
        st_ref[h] = sts[h]


def _gdn(proj, small, batch, seq, layer, conv_w, a_log, dt_bias, norm_g):
    t = proj.shape[0]
    nsb = seq // SEQ_BLOCK
    hd = GDN_HEAD_DIM
    row = lambda b, s: b * nsb + s
    depth = a_log.shape[0]
    lane_row = lambda a: jnp.zeros((depth, 1, LANES), F32).at[:, 0, LANE_ALPHA:LANE_ALPHA + GDN_HEADS].set(a)
    lspec = pl.BlockSpec((None, 1, LANES), lambda b, s: (layer, 0, 0))
    return pl.pallas_call(
        _gdn_kernel,
        grid=(batch, nsb),
        in_specs=[pl.BlockSpec((SEQ_BLOCK, BW), lambda b, s: (row(b, s), OFF_CQ // BW)),
                  pl.BlockSpec((SEQ_BLOCK, BW), lambda b, s: (row(b, s), OFF_CK // BW)),
                  pl.BlockSpec((SEQ_BLOCK, BW), lambda b, s: (row(b, s), OFF_CV // BW)),
                  pl.BlockSpec((SEQ_BLOCK, BW), lambda b, s: (row(b, s), OFF_CG // BW)),
                  pl.BlockSpec((SEQ_BLOCK, LANES), lambda b, s: (row(b, s), 0)),
                  pl.BlockSpec((None, CONV_WIDTH, BW), lambda b, s: (layer, 0, 0)),
                  pl.BlockSpec((None, CONV_WIDTH, BW), lambda b, s: (layer, 0, 1)),
                  pl.BlockSpec((None, CONV_WIDTH, BW), lambda b, s: (layer, 0, 2)),
                  lspec, lspec,
                  pl.BlockSpec((None, 1, hd), lambda b, s: (layer, 0, 0))],
        out_specs=pl.BlockSpec((SEQ_BLOCK, BW), lambda b, s: (row(b, s), 0)),
        out_shape=jax.ShapeDtypeStruct((t, BW), BF16),
        scratch_shapes=[pltpu.VMEM((GDN_HEADS, hd, hd), F32)] + [pltpu.VMEM((SUBLANES, BW), F32)] * 3,
        compiler_params=_cparams(("parallel", "arbitrary")),
        name="gdn",
    )(proj, proj, proj, proj, small, conv_w, conv_w, conv_w, lane_row(a_log), lane_row(dt_bias),
      norm_g.reshape(depth, 1, hd))


def _mlstm_kernel(q_ref, k_ref, v_ref, og_ref, sm_ref, bi_ref, bf_ref, ng_ref, o_ref,
                  c_ref, n_ref, m_ref):
    @pl.when(pl.program_id(1) == 0)
    def _():
        c_ref[...] = jnp.zeros_like(c_ref)
        n_ref[...] = jnp.zeros_like(n_ref)
        m_ref[...] = jnp.zeros_like(m_ref)

    n = SEQ_BLOCK
    nh, dqk, dv = MLSTM_HEADS, MLSTM_DQK, MLSTM_DV
    hr = range(nh)
    _, _, tril, _ = _block_masks(n)
    qs = _heads(q_ref[...], dqk, nh)
    ks = _heads(k_ref[...] * (MLSTM_DQK ** -0.5), dqk, nh)
    vs = _heads(v_ref[...], dv, nh)
    sm = sm_ref[...]
    ig_lanes = sm + bi_ref[...]
    bcum_lanes = _chunk_cumsum(_log_sigmoid(sm + bf_ref[...]))
    igs = [ig_lanes[:, LANE_I + h:LANE_I + h + 1] for h in hr]
    bcums = [bcum_lanes[:, LANE_F + h:LANE_F + h + 1] for h in hr]
    b_lasts = [_chunk_last(b) for b in bcums]
    logds = [jnp.where(tril, b + _row_form(i - b), -jnp.inf) for b, i in zip(bcums, igs)]
    m_intras = [jnp.max(ld, axis=1, keepdims=True) for ld in logds]
    tails = [bl - b + i for bl, b, i in zip(b_lasts, bcums, igs)]
    spread = lambda xs: jnp.concatenate([jnp.broadcast_to(x, (CHUNK, 1)) for x in xs], axis=0)
    m_prev_rows, a_max_rows, s_olds, s_news = [], [], [], []
    for h in hr:
        m_st = m_ref[h, 0:1, 0:1]
        m_prev, a_max, s_old, s_new = [], [], [], []
        for c in range(N_CHUNKS):
            a_c = jnp.max(tails[h][c * CHUNK:(c + 1) * CHUNK], axis=0, keepdims=True)
            bl_c = bcums[h][(c + 1) * CHUNK - 1:(c + 1) * CHUNK, :]
            m_new = jnp.maximum(bl_c + m_st, a_c)
            m_prev.append(m_st)
            a_max.append(a_c)
            s_old.append(jnp.exp(bl_c + m_st - m_new))
            s_new.append(jnp.exp(a_c - m_new))
            m_st = m_new
        m_ref[h] = jnp.broadcast_to(m_st, m_ref.shape[1:])
        m_prev_rows.append(spread(m_prev))
        a_max_rows.append(spread(a_max))
        s_olds.append(s_old)
        s_news.append(s_new)
    k_ws = [k * jnp.exp(t - a) for k, t, a in zip(ks, tails, a_max_rows)]
    m_inters = [b + mp for b, mp in zip(bcums, m_prev_rows)]
    m_is = [jnp.maximum(mi, mx) for mi, mx in zip(m_inters, m_intras)]
    s_inters = [jnp.exp(mi - m) for mi, m in zip(m_inters, m_is)]
    scs = [_dot_nt(q, k) * jnp.exp(ld - m) for q, k, ld, m in zip(qs, ks, logds, m_is)]
    num_intras = [_dot(sc, v) for sc, v in zip(scs, vs)]
    den_intras = [jnp.sum(sc, axis=1, keepdims=True) for sc in scs]
    floors = [jnp.exp(-m) for m in m_is]
    c_sts = [c_ref[h] for h in hr]
    n_sts = [n_ref[h, 0:1, :] for h in hr]
    for c in range(N_CHUNKS):
        sl = slice(c * CHUNK, (c + 1) * CHUNK)
        nums = [ni[sl] + si[sl] * _dot(q[sl], cs) for ni, si, q, cs in zip(num_intras, s_inters, qs, c_sts)]
        dens = [di[sl] + si[sl] * jnp.sum(q[sl] * ns, axis=1, keepdims=True)
                for di, si, q, ns in zip(den_intras, s_inters, qs, n_sts)]
        kvs = [_dot_tn(kw[sl], v[sl]) for kw, v in zip(k_ws, vs)]
        c_sts = [cs * so[c] + kv * sn[c] for cs, so, kv, sn in zip(c_sts, s_olds, kvs, s_news)]
        n_sts = [ns * so[c] + jnp.sum(kw[sl], axis=0, keepdims=True) * sn[c]
                 for ns, so, kw, sn in zip(n_sts, s_olds, k_ws, s_news)]
        for h in hr:
            cols = slice(h * dv, (h + 1) * dv)
            hh = nums[h] / jnp.maximum(jnp.abs(dens[h]), floors[h][sl])
            hh = hh * lax.rsqrt(jnp.mean(hh * hh, axis=-1, keepdims=True) + NORM_EPS) * ng_ref[...]
            o_ref[sl, cols] = (hh * _sigmoid(og_ref[sl, cols])).astype(o_ref.dtype)
    for h in hr:
        c_ref[h] = c_sts[h]
        n_ref[h] = jnp.broadcast_to(n_sts[h], n_ref.shape[1:])


def _mlstm(proj, small, batch, seq, layer, b_i, b_f, norm_g):
    t = proj.shape[0]
    nsb = seq // SEQ_BLOCK
    nh, dqk, dv = MLSTM_HEADS, MLSTM_DQK, MLSTM_DV
    wq = nh * dqk
    row = lambda b, s: b * nsb + s
    depth = b_i.shape[0]
    lane_row = lambda a, off: jnp.zeros((depth, 1, LANES), F32).at[:, 0, off:off + nh].set(a)
    lspec = pl.BlockSpec((None, 1, LANES), lambda b, s: (layer, 0, 0))
    return pl.pallas_call(
        _mlstm_kernel,
        grid=(batch, nsb),
        in_specs=[pl.BlockSpec((SEQ_BLOCK, wq), lambda b, s: (row(b, s), OFF_DQ // wq)),
                  pl.BlockSpec((SEQ_BLOCK, wq), lambda b, s: (row(b, s), OFF_DK // wq)),
                  pl.BlockSpec((SEQ_BLOCK, BW), lambda b, s: (row(b, s), OFF_DV // BW)),
                  pl.BlockSpec((SEQ_BLOCK, BW), lambda b, s: (row(b, s), OFF_DO // BW)),
                  pl.BlockSpec((SEQ_BLOCK, LANES), lambda b, s: (row(b, s), 0)),
                  lspec, lspec,
                  pl.BlockSpec((None, 1, dv), lambda b, s: (layer, 0, 0))],
        out_specs=pl.BlockSpec((SEQ_BLOCK, BW), lambda b, s: (row(b, s), 0)),
        out_shape=jax.ShapeDtypeStruct((t, BW), BF16),
        scratch_shapes=[pltpu.VMEM((nh, dqk, dv), F32), pltpu.VMEM((nh, SUBLANES, dqk), F32),
                        pltpu.VMEM((nh, SUBLANES, LANES), F32)],
        compiler_params=_cparams(("parallel", "arbitrary")),
        name="mlstm",
    )(proj, proj, proj, proj, small, lane_row(b_i, LANE_I), lane_row(b_f, LANE_F),
      norm_g.reshape(depth, 1, dv))


def _merge_kernel(xn_ref, ya_ref, yb_ref, yc_ref, yd_ref, wg_ref, bg_ref, wb_ref, o_ref):
    xn = xn_ref[...]
    acc = None
    for n, y_ref in enumerate((ya_ref, yb_ref, yc_ref, yd_ref)):
        gate = _sigmoid(_dot(xn, wg_ref[n]) + bg_ref[n])
        term = gate * _dot(y_ref[...], wb_ref[n])
        acc = term if acc is None else acc + term
    o_ref[...] = acc.astype(o_ref.dtype)


def _merge(xn, ys, layer, wg, bg, wb, tm=512, tn=256):
    t, d = xn.shape
    tm = min(tm, t)
    depth, nb = wg.shape[:2]
    yspec = pl.BlockSpec((tm, BW), lambda i, j: (i, 0))
    return pl.pallas_call(
        _merge_kernel,
        grid=(t // tm, d // tn),
        in_specs=[pl.BlockSpec((tm, d), lambda i, j: (i, 0)), yspec, yspec, yspec, yspec,
                  pl.BlockSpec((None, nb, d, tn), lambda i, j: (layer, 0, 0, j)),
                  pl.BlockSpec((None, nb, 1, tn), lambda i, j: (layer, 0, 0, j)),
                  pl.BlockSpec((None, nb, BW, tn), lambda i, j: (layer, 0, 0, j))],
        out_specs=pl.BlockSpec((tm, tn), lambda i, j: (i, j)),
        out_shape=jax.ShapeDtypeStruct((t, d), BF16),
        compiler_params=_cparams(("parallel", "parallel")),
        name="gated_merge",
    )(xn, *ys, wg, bg.reshape(depth, nb, 1, d), wb)


def _router_kernel(x_ref, g_ref, wrh_ref, wrl_ref, br_ref, h_ref, info_ref, idx_ref, cnt_ref, carry_ref):
    @pl.when(pl.program_id(0) == 0)
    def _():
        carry_ref[...] = jnp.zeros_like(carry_ref)

    x = x_ref[...]
    tm = x.shape[0]
    h = x * lax.rsqrt(jnp.mean(x * x, axis=-1, keepdims=True) + NORM_EPS) * g_ref[...]
    h_ref[...] = _pack_pairs(h)
    h_hi = h.astype(BF16)
    h_lo = (h - h_hi.astype(F32)).astype(BF16)
    logits = (_dot(h_hi, wrh_ref[...]) + (_dot(h_lo, wrh_ref[...]) + _dot(h_hi, wrl_ref[...]))
              + br_ref[...])
    lane = _iota((tm, LANES), 1).astype(F32)
    big = float(LANES)
    gl = jnp.where(lane < N_GROUPS, logits, -jnp.inf)
    gmax = jnp.max(gl, axis=1, keepdims=True)
    grp = jnp.min(jnp.where(gl == gmax, lane, big), axis=1, keepdims=True)
    p_grp = 1.0 / jnp.sum(jnp.exp(gl - gmax), axis=1, keepdims=True)
    lo = N_GROUPS + grp * EXPERTS_PER_GROUP
    el = jnp.where((lane >= lo) & (lane < lo + EXPERTS_PER_GROUP), logits, -jnp.inf)
    v0 = jnp.max(el, axis=1, keepdims=True)
    i0 = jnp.min(jnp.where(el == v0, lane, big), axis=1, keepdims=True)
    el = jnp.where(lane == i0, -jnp.inf, el)
    v1 = jnp.max(el, axis=1, keepdims=True)
    i1 = jnp.min(jnp.where(el == v1, lane, big), axis=1, keepdims=True)
    e = jnp.exp(v1 - v0)
    w0 = p_grp / (1.0 + e)
    w1 = p_grp * e / (1.0 + e)
    e0 = i0 - N_GROUPS
    e1 = i1 - N_GROUPS
    oh0 = (lane == e0).astype(F32)
    oh1 = (lane == e1).astype(F32)
    both = oh0 + oh1
    strict = (_iota((tm, tm), 0) > _iota((tm, tm), 1)).astype(BF16)
    before = carry_ref[0:1, :] + _dot(strict, both.astype(BF16))
    r0 = jnp.sum(oh0 * before, axis=1, keepdims=True)
    r1 = jnp.sum(oh1 * before, axis=1, keepdims=True)
    total = carry_ref[0:1, :] + jnp.sum(both, axis=0, keepdims=True)
    carry_ref[...] = jnp.broadcast_to(total, carry_ref.shape)
    cnt_ref[...] = jnp.broadcast_to(total, cnt_ref.shape)
    info_ref[...] = jnp.where(lane == 0.0, w0, jnp.where(lane == 1.0, w1, 0.0))
    idx = jnp.where(lane == 0.0, e0, jnp.where(lane == 1.0, e1,
                                               jnp.where(lane == 2.0, r0, jnp.where(lane == 3.0, r1, 0.0))))
    idx_ref[...] = idx.astype(jnp.int32)


def _router(x2d, layer, g, w_grp, b_grp, w_exp, b_exp, tm=256):
    t, d = x2d.shape
    tm = min(tm, t)
    depth = g.shape[0]
    wr = (jnp.zeros((depth, d, LANES), F32).at[:, :, :N_GROUPS].set(w_grp)
          .at[:, :, N_GROUPS:N_GROUPS + N_EXPERTS].set(w_exp))
    br = (jnp.zeros((depth, 1, LANES), F32).at[:, 0, :N_GROUPS].set(b_grp)
          .at[:, 0, N_GROUPS:N_GROUPS + N_EXPERTS].set(b_exp))
    wr_hi = wr.astype(BF16)
    wr_lo = (wr - wr_hi.astype(F32)).astype(BF16)
    return pl.pallas_call(
        _router_kernel,
        grid=(t // tm,),
        in_specs=[pl.BlockSpec((tm, d), lambda i: (i, 0)),
                  pl.BlockSpec((None, 1, d), lambda i: (layer, 0, 0)),
                  pl.BlockSpec((None, d, LANES), lambda i: (layer, 0, 0)),
                  pl.BlockSpec((None, d, LANES), lambda i: (layer, 0, 0)),
                  pl.BlockSpec((None, 1, LANES), lambda i: (layer, 0, 0))],
        out_specs=[pl.BlockSpec((tm, d // 2), lambda i: (i, 0)),
                   pl.BlockSpec((tm, LANES), lambda i: (i, 0)),
                   pl.BlockSpec((tm, LANES), lambda i: (i, 0)),
                   pl.BlockSpec((SUBLANES, LANES), lambda i: (0, 0))],
        out_shape=[jax.ShapeDtypeStruct((t, d // 2), jnp.uint32),
                   jax.ShapeDtypeStruct((t, LANES), F32),
                   jax.ShapeDtypeStruct((t, LANES), jnp.int32),
                   jax.ShapeDtypeStruct((SUBLANES, LANES), F32)],
        scratch_shapes=[pltpu.VMEM((SUBLANES, LANES), F32)],
        compiler_params=_cparams(("arbitrary",)),
        name="router",
    )(x2d, g.reshape(depth, 1, d), wr_hi, wr_lo, br)


def _dispatch_kernel(pos_ref, zrow_ref, h_ref, xs_ref, zero_ref, sem, zsem):
    i = pl.program_id(0)
    tm = h_ref.shape[0]
    zt = zero_ref.shape[0]

    @pl.when(i == 0)
    def _():
        zero_ref[...] = jnp.zeros_like(zero_ref)

        def zero_copy(e):
            start = pl.multiple_of(jnp.maximum(zrow_ref[e], 0), zt)
            return pltpu.make_async_copy(zero_ref, xs_ref.at[pl.ds(start, zt)], zsem)

        def zissue(e, carry):
            @pl.when(zrow_ref[e] >= 0)
            def _():
                zero_copy(e).start()
            return carry

        def zdrain(e, carry):
            @pl.when(zrow_ref[e] >= 0)
            def _():
                zero_copy(e).wait()
            return carry

        lax.fori_loop(0, zrow_ref.shape[0], zissue, 0)
        lax.fori_loop(0, zrow_ref.shape[0], zdrain, 0)

    def row_copy(r, p):
        return pltpu.make_async_copy(h_ref.at[pl.ds(r, 1)], xs_ref.at[pl.ds(p, 1)], sem)

    def issue(r, carry):
        t = i * tm + r
        for k in range(2):
            row_copy(r, pos_ref[2 * t + k]).start()
        return carry

    lax.fori_loop(0, tm, issue, 0)
    for k in range(2):
        pltpu.make_async_copy(h_ref, xs_ref.at[pl.ds(0, tm)], sem).wait()


def _dispatch(pos, zrow, h, n_rows, tm=ROW_TILE):
    t, d = h.shape
    tm = min(tm, t)
    return pl.pallas_call(
        _dispatch_kernel,
        grid_spec=pltpu.PrefetchScalarGridSpec(
            num_scalar_prefetch=2,
            grid=(t // tm,),
            in_specs=[pl.BlockSpec((tm, d), lambda i, pos, zrow: (i, 0))],
            out_specs=pl.BlockSpec(memory_space=pl.ANY),
            scratch_shapes=[pltpu.VMEM((EXPERT_TILE, d), h.dtype), pltpu.SemaphoreType.DMA(()),
                            pltpu.SemaphoreType.DMA(())],
        ),
        out_shape=jax.ShapeDtypeStruct((n_rows, d), h.dtype),
        compiler_params=_cparams(("arbitrary",)),
        name="dispatch",
    )(pos, zrow, h)


def _combine_kernel(keep_x, pos_ref, x_ref, info_ref, g_ref, ys_ref, *rest):
    outs, (buf_ref, sem) = rest[:-2], rest[-2:]
    i = pl.program_id(0)
    n = pl.num_programs(0)
    tm = x_ref.shape[0]
    half = x_ref.shape[1] // 2

    def row_copy(slot, r, k, p):
        return pltpu.make_async_copy(ys_ref.at[pl.ds(p, 1)], buf_ref.at[slot, k, pl.ds(r, 1)],
                                     sem.at[slot])

    def gather_tile(tile, slot):
        def issue(r, carry):
            t = tile * tm + r
            for k in range(2):
                row_copy(slot, r, k, pos_ref[2 * t + k]).start()
            return carry
        lax.fori_loop(0, tm, issue, 0)

    @pl.when(i == 0)
    def _():
        gather_tile(0, 0)

    @pl.when(i + 1 < n)
    def _():
        gather_tile(i + 1, (i + 1) % 2)

    slot = i % 2

    for k in range(2):
        pltpu.make_async_copy(ys_ref.at[pl.ds(0, tm)], buf_ref.at[slot, k], sem.at[slot]).wait()
    info = info_ref[...]
    w0, w1 = info[:, 0:1], info[:, 1:2]
    lo0, hi0 = _unpack_pairs(buf_ref[slot, 0])
    lo1, hi1 = _unpack_pairs(buf_ref[slot, 1])
    x = x_ref[...]
    x_lo = x[:, :half] + w0 * lo0 + w1 * lo1
    x_hi = x[:, half:] + w0 * hi0 + w1 * hi1
    ms = (jnp.sum(x_lo * x_lo, axis=-1, keepdims=True)
          + jnp.sum(x_hi * x_hi, axis=-1, keepdims=True)) / x.shape[1]
    inv = lax.rsqrt(ms + NORM_EPS)
    g = g_ref[...]
    if keep_x:
        outs[0][:, :half] = x_lo
        outs[0][:, half:] = x_hi
    outs[-1][:, :half] = (x_lo * inv * g[:, :half]).astype(outs[-1].dtype)
    outs[-1][:, half:] = (x_hi * inv * g[:, half:]).astype(outs[-1].dtype)


def _combine(pos, x2d, info, ys, g, keep_x, norm_dtype, tm=ROW_TILE):
    t, d = x2d.shape
    tm = min(tm, t)
    row_spec = pl.BlockSpec((tm, d), lambda i, pos: (i, 0))
    out_specs = [row_spec] * (2 if keep_x else 1)
    out_shape = ([jax.ShapeDtypeStruct((t, d), F32)] if keep_x else []) + [jax.ShapeDtypeStruct((t, d), norm_dtype)]
    return pl.pallas_call(
        functools.partial(_combine_kernel, keep_x),
        grid_spec=pltpu.PrefetchScalarGridSpec(
            num_scalar_prefetch=1,
            grid=(t // tm,),
            in_specs=[row_spec,
                      pl.BlockSpec((tm, LANES), lambda i, pos: (i, 0)),
                      pl.BlockSpec((1, d), lambda i, pos: (0, 0)),
                      pl.BlockSpec(memory_space=pl.ANY)],
            out_specs=out_specs,
            scratch_shapes=[pltpu.VMEM((2, 2, tm, d // 2), ys.dtype),
                            pltpu.SemaphoreType.DMA((2,))],
        ),
        out_shape=out_shape,
        compiler_params=_cparams(("arbitrary",)),
        name="combine",
    )(pos, x2d, info, g.reshape(1, d), ys)


def _new_expert(te_ref, i):
    return (i == 0) | (te_ref[i] != te_ref[jnp.maximum(i - 1, 0)])


def _expert_up_kernel(te_ref, nu_ref, xs_ref, wg_ref, wu_ref, act_ref, wgb_ref, wub_ref):
    i = pl.program_id(0)

    @pl.when(_new_expert(te_ref, i))
    def _():
        wgb_ref[...] = wg_ref[...].astype(BF16)
        wub_ref[...] = wu_ref[...].astype(BF16)

    @pl.when(i < nu_ref[0])
    def _():
        lo, hi = _unpack_pairs(xs_ref[...])
        x = jnp.concatenate([lo.astype(BF16), hi.astype(BF16)], axis=1)
        a = _dot(x, wgb_ref[...])
        u = _dot(x, wub_ref[...])
        act_ref[...] = (_silu(a) * u).astype(BF16)

    @pl.when(i >= nu_ref[0])
    def _():
        act_ref[...] = jnp.zeros_like(act_ref)


def _expert_down_kernel(te_ref, nu_ref, act_ref, wd_ref, ys_ref, wdb_ref):
    i = pl.program_id(0)

    @pl.when(_new_expert(te_ref, i))
    def _():
        wdb_ref[...] = wd_ref[...].astype(BF16)

    @pl.when(i < nu_ref[0])
    def _():
        ys_ref[...] = _pack_pairs(_dot(act_ref[...], wdb_ref[...]))

    @pl.when(i >= nu_ref[0])
    def _():
        ys_ref[...] = jnp.zeros_like(ys_ref)


def _experts(tile_expert, n_used, xs, wg, wu, wd):
    p, dh = xs.shape
    d, f = wg.shape[1:]
    tm = EXPERT_TILE
    src_tile = lambda i, te, nu: (jnp.maximum(jnp.minimum(i, nu[0] - 1), 0), 0)
    by_expert = lambda i, te, nu: (te[i], 0, 0)
    by_tile = lambda i, te, nu: (i, 0)
    act = pl.pallas_call(
        _expert_up_kernel,
        grid_spec=pltpu.PrefetchScalarGridSpec(
            num_scalar_prefetch=2,
            grid=(p // tm,),
            in_specs=[pl.BlockSpec((tm, dh), src_tile),
                      pl.BlockSpec((None, d, f), by_expert),
                      pl.BlockSpec((None, d, f), by_expert)],
            out_specs=pl.BlockSpec((tm, f), by_tile),
            scratch_shapes=[pltpu.VMEM((d, f), BF16), pltpu.VMEM((d, f), BF16)],
        ),
        out_shape=jax.ShapeDtypeStruct((p, f), BF16),
        compiler_params=_cparams(("arbitrary",)),
        name="experts_up",
    )(tile_expert, n_used, xs, wg, wu)
    return pl.pallas_call(
        _expert_down_kernel,
        grid_spec=pltpu.PrefetchScalarGridSpec(
            num_scalar_prefetch=2,
            grid=(p // tm,),
            in_specs=[pl.BlockSpec((tm, f), by_tile),
                      pl.BlockSpec((None, f, d), by_expert)],
            out_specs=pl.BlockSpec((tm, dh), by_tile),
            scratch_shapes=[pltpu.VMEM((f, d), BF16)],
        ),
        out_shape=jax.ShapeDtypeStruct((p, dh), xs.dtype),
        compiler_params=_cparams(("arbitrary",)),
        name="experts_down",
    )(tile_expert, n_used, act, wd)


def _hier_moe(x2d, layer, ffn_g, w_grp, b_grp, w_exp, b_exp, wg, wu, wd, next_g, keep_x, norm_dtype):
    t, d = x2d.shape
    tm = EXPERT_TILE
    n_rows = 2 * t + N_EXPERTS * tm
    h, info, idx, cnt = _router(x2d, layer, ffn_g, w_grp, b_grp, w_exp, b_exp)
    counts = cnt[0, :N_EXPERTS].astype(jnp.int32)
    padded = ((counts + tm - 1) // tm) * tm
    ends = jnp.cumsum(padded)
    offs = ends - padded
    pos = (offs[idx[:, 0:2]] + idx[:, 2:4]).reshape(2 * t)
    n_used = (ends[-1] // tm).reshape(1).astype(jnp.int32)
    tail = n_used[0] + jnp.arange(N_EXPERTS, dtype=jnp.int32)
    zrow = jnp.concatenate([jnp.where(padded > 0, ends - tm, -1),
                            jnp.where(tail < n_rows // tm, tail * tm, -1)]).astype(jnp.int32)
    tiles = jnp.minimum(jnp.arange(n_rows // tm, dtype=jnp.int32), n_used[0] - 1)
    tile_expert = jnp.sum(((ends // tm)[None, :] <= tiles[:, None]).astype(jnp.int32), axis=1)
    tile_expert = jnp.minimum(tile_expert, N_EXPERTS - 1) + layer * N_EXPERTS
    xs = _dispatch(pos, zrow, h, n_rows)
    ys = _experts(tile_expert, n_used, xs, wg, wu, wd)
    return _combine(pos, x2d, info, ys, next_g, keep_x, norm_dtype)


def _repack_kernel(w_ref, big_ref, small_ref):
    c0 = 5 * BW
    c1 = c0 + GLA_RANK
    c2 = c1 + 4 * BW
    c3 = c2 + 2 * GDN_HEADS
    c4 = c3 + 3 * BW
    c5 = c4 + 2 * MLSTM_HEADS
    step = 512
    for dst, src, n in ((0, 0, c0), (c0, c1, c2 - c1), (c0 + c2 - c1, c3, c4 - c3)):
        for o in range(0, n, step):
            big_ref[:, dst + o:dst + o + step] = w_ref[src + o:src + o + step, :].T.astype(BF16)
    n_small = GLA_RANK + 2 * GDN_HEADS + 2 * MLSTM_HEADS
    small = jnp.concatenate([w_ref[c0:c1, :], w_ref[c2:c3, :], w_ref[c4:c5, :],
                             jnp.zeros((LANES - n_small, w_ref.shape[1]), F32)], axis=0)
    small_ref[...] = small.T.astype(BF16)


def _split_w_in(w, tk=256):
    depth, k, n = w.shape
    w = jnp.swapaxes(w, 1, 2)
    return pl.pallas_call(
        _repack_kernel,
        grid=(depth, k // tk),
        in_specs=[pl.BlockSpec((None, n, tk), lambda l, i: (l, 0, i))],
        out_specs=[pl.BlockSpec((None, tk, PROJ_BIG), lambda l, i: (l, i, 0)),
                   pl.BlockSpec((None, tk, LANES), lambda l, i: (l, i, 0))],
        out_shape=[jax.ShapeDtypeStruct((depth, k, PROJ_BIG), BF16),
                   jax.ShapeDtypeStruct((depth, k, LANES), BF16)],
        compiler_params=_cparams(("parallel", "parallel")),
        name="repack_w_in",
    )(w)


def kernel(x, mix_norm, w_in, lru_conv_w, lru_conv_b, lru_w_a, lru_b_a, lru_w_x, lru_b_x, lru_lambda,
           gla_w_decay, gla_b_decay, gla_norm, gdn_conv_w, gdn_a_log, gdn_dt_bias, gdn_norm,
           mlstm_b_i, mlstm_b_f, mlstm_norm, w_branch, w_merge_gate, b_merge_gate, w_out,
           ffn_norm, w_group_router, b_group_router, w_expert_router, b_expert_router,
           w_exp_gate, w_exp_up, w_exp_down, final_norm):
    batch, seq, d = x.shape
    depth = w_in.shape[0]
    x2d = x.reshape(batch * seq, d)
    w_big, w_small = _split_w_in(w_in)
    wg_merge = w_merge_gate.astype(BF16)
    wb_merge = w_branch.astype(BF16)
    w_o = w_out.astype(BF16)
    f = w_exp_gate.shape[-1]
    we_gate = w_exp_gate.reshape(depth * N_EXPERTS, d, f)
    we_up = w_exp_up.reshape(depth * N_EXPERTS, d, f)
    we_down = w_exp_down.reshape(depth * N_EXPERTS, f, d)
    xn = _rmsnorm(x2d, mix_norm[0], BF16)
    for l in range(depth):
        last = l == depth - 1
        proj = _matmul(xn, w_big, l, tm=1024, name="in_proj")
        small = _matmul(xn, w_small, l, name="in_proj_small")
        y_a = _lru(proj, batch, seq, l, lru_conv_w, lru_conv_b, lru_w_a, lru_b_a, lru_w_x, lru_b_x,
                   lru_lambda)
        y_b = _gla(proj, small, batch, seq, l, gla_w_decay, gla_b_decay, gla_norm)
        y_c = _gdn(proj, small, batch, seq, l, gdn_conv_w, gdn_a_log, gdn_dt_bias, gdn_norm)
        y_d = _mlstm(proj, small, batch, seq, l, mlstm_b_i, mlstm_b_f, mlstm_norm)
        merged = _merge(xn, (y_a, y_b, y_c, y_d), l, wg_merge, b_merge_gate, wb_merge)
        x2d = _matmul(merged, w_o, l, residual=x2d, tm=1024, name="out_proj")
        outs = _hier_moe(x2d, l, ffn_norm, w_group_router, b_group_router, w_expert_router,
                         b_expert_router, we_gate, we_up, we_down,
                         final_norm if last else mix_norm[l + 1], not last, F32 if last else BF16)
        if last:
            (y,) = outs
        else:
            x2d, xn = outs
    return y.reshape(batch, seq, d)
```

```python
import functools

import jax
import jax.numpy as jnp
from jax import lax
from jax.experimental import pallas as pl
from jax.experimental.pallas import tpu as pltpu

F32 = jnp.float32
BF16 = jnp.bfloat16
HI = lax.Precision.HIGHEST

D_MODEL = 4096
BW = D_MODEL // 4
CONV_WIDTH = 4
CHUNK = 64
NORM_EPS = 1e-6
LRU_BLOCKS = 8
LRU_BLOCK = BW // LRU_BLOCKS
LRU_C = 8.0
GLA_HEADS = 4
GLA_DV = BW // GLA_HEADS
GLA_DK = GLA_DV // 2
GLA_RANK = 16
GLA_TAU = 16.0
GDN_HEAD_DIM = 128
GDN_HEADS = BW // GDN_HEAD_DIM
MLSTM_HEADS = 4
MLSTM_DV = BW // MLSTM_HEADS
MLSTM_DQK = MLSTM_DV // 2
N_GROUPS = 4
EXPERTS_PER_GROUP = 8
N_EXPERTS = N_GROUPS * EXPERTS_PER_GROUP
D_EXPERT = D_MODEL // 8

LANES = 128
SUBLANES = 8
VMEM_LIMIT = 56 * 1024 * 1024

PROJ_BIG = 12 * BW
OFF_AX, OFF_AG = 0, BW
OFF_BQ, OFF_BK, OFF_BV, OFF_BG = 2 * BW, 2 * BW + 512, 3 * BW, 4 * BW
OFF_CQ, OFF_CK, OFF_CV, OFF_CG = 5 * BW, 6 * BW, 7 * BW, 8 * BW
OFF_DQ, OFF_DK, OFF_DV, OFF_DO = 9 * BW, 9 * BW + 512, 10 * BW, 11 * BW
LANE_DECAY, LANE_BETA, LANE_ALPHA, LANE_I, LANE_F = 0, 16, 24, 32, 36

SEQ_BLOCK = 256
N_CHUNKS = SEQ_BLOCK // CHUNK
EXPERT_TILE = 256
ROW_TILE = 256


def _cparams(sem):
    return pltpu.CompilerParams(dimension_semantics=sem, vmem_limit_bytes=VMEM_LIMIT)


def _softplus(x):
    return jnp.maximum(x, 0.0) + jnp.log1p(jnp.exp(-jnp.abs(x)))


def _log_sigmoid(x):
    return -_softplus(-x)


def _sigmoid(x):
    return 1.0 / (1.0 + jnp.exp(-x))


def _silu(x):
    return x * _sigmoid(x)


def _gelu_tanh(x):
    return 0.5 * x * (1.0 + jnp.tanh(0.7978845608028654 * (x + 0.044715 * (x * x * x))))


def _dot(a, b, precision=None):
    return jnp.dot(a, b, preferred_element_type=F32, precision=precision)


def _dot_nt(a, b, precision=None):
    return lax.dot_general(a, b, (((1,), (1,)), ((), ())), preferred_element_type=F32,
                           precision=precision)


def _dot_tn(a, b, precision=None):
    return lax.dot_general(a, b, (((0,), (0,)), ((), ())), preferred_element_type=F32,
                           precision=precision)


def _iota(shape, dim):
    return lax.broadcasted_iota(jnp.int32, shape, dim)


def _shift_rows(x, tail, s):
    r = pltpu.roll(x, s, 0)
    rt = pltpu.roll(tail, s, 0)
    head = jnp.where(_iota(tail.shape, 0) < s, rt, r[0:SUBLANES])
    return jnp.concatenate([head, r[SUBLANES:]], axis=0)


def _causal_conv(x, tail, w):
    y = x * w[CONV_WIDTH - 1:CONV_WIDTH, :]
    for s in range(1, CONV_WIDTH):
        y = y + _shift_rows(x, tail, s) * w[CONV_WIDTH - 1 - s:CONV_WIDTH - s, :]
    return y


def _chunk_cumsum(x):
    pos = _iota(x.shape, 0) & (CHUNK - 1)
    s = 1
    while s < CHUNK:
        x = x + jnp.where(pos >= s, pltpu.roll(x, s, 0), 0.0)
        s *= 2
    return x


def _chunk_last(x):
    return jnp.concatenate(
        [jnp.broadcast_to(x[(c + 1) * CHUNK - 1:(c + 1) * CHUNK], (CHUNK, x.shape[1]))
         for c in range(x.shape[0] // CHUNK)], axis=0)


def _row_form(col):
    n = col.shape[0]
    return jnp.broadcast_to(col, (n, n)).T


def _block_masks(n):
    ri = _iota((n, n), 0)
    ci = _iota((n, n), 1)
    same = (ri // CHUNK) == (ci // CHUNK)
    return ri, ci, same & (ri >= ci), same & (ri > ci)


def _pack_pairs(x):
    c = x.shape[1] // 2
    bits = lambda v: lax.bitcast_convert_type(v.astype(BF16).astype(F32), jnp.uint32)
    return (bits(x[:, :c]) >> 16) | bits(x[:, c:])


def _unpack_pairs(p):
    lo = lax.bitcast_convert_type(p << 16, F32)
    hi = lax.bitcast_convert_type(p & jnp.uint32(0xFFFF0000), F32)
    return lo, hi


def _heads(x, width, count):
    return [x[:, h * width:(h + 1) * width] for h in range(count)]


def _rmsnorm_kernel(x_ref, g_ref, o_ref):
    x = x_ref[...]
    ms = jnp.mean(x * x, axis=-1, keepdims=True)
    o_ref[...] = (x * lax.rsqrt(ms + NORM_EPS) * g_ref[...]).astype(o_ref.dtype)


def _rmsnorm(x2d, g, out_dtype, tm=512):
    t, d = x2d.shape
    return pl.pallas_call(
        _rmsnorm_kernel,
        grid=(t // tm,),
        in_specs=[pl.BlockSpec((tm, d), lambda i: (i, 0)),
                  pl.BlockSpec((1, d), lambda i: (0, 0))],
        out_specs=pl.BlockSpec((tm, d), lambda i: (i, 0)),
        out_shape=jax.ShapeDtypeStruct((t, d), out_dtype),
        compiler_params=_cparams(("parallel",)),
        name="rmsnorm",
    )(x2d, g.reshape(1, d))


def _mm_kernel(a_ref, w_ref, o_ref):
    o_ref[...] = _dot(a_ref[...], w_ref[...]).astype(o_ref.dtype)


def _mm_res_kernel(a_ref, w_ref, r_ref, o_ref):
    o_ref[...] = r_ref[...] + _dot(a_ref[...], w_ref[...])


def _matmul(a, w, layer, out_dtype=F32, residual=None, tm=512, tn=512, name="matmul"):
    m, k = a.shape
    n = w.shape[2]
    tm, tn = min(tm, m), min(tn, n)
    in_specs = [pl.BlockSpec((tm, k), lambda i, j: (i, 0)),
                pl.BlockSpec((None, k, tn), lambda i, j: (layer, 0, j))]
    args = [a, w]
    body = _mm_kernel
    if residual is not None:
        in_specs.append(pl.BlockSpec((tm, tn), lambda i, j: (i, j)))
        args.append(residual)
        body = _mm_res_kernel
    return pl.pallas_call(
        body,
        grid=(m // tm, n // tn),
        in_specs=in_specs,
        out_specs=pl.BlockSpec((tm, tn), lambda i, j: (i, j)),
        out_shape=jax.ShapeDtypeStruct((m, n), out_dtype),
        compiler_params=_cparams(("parallel", "parallel")),
        name=name,
    )(*args)


def _lru_kernel(x_ref, gate_ref, cw_ref, cb_ref, wa_ref, ba_ref, wx_ref, bx_ref, lam_ref,
                o_ref, tail_ref, h_ref):
    @pl.when(pl.program_id(1) == 0)
    def _():
        tail_ref[...] = jnp.zeros_like(tail_ref)
        h_ref[...] = jnp.zeros_like(h_ref)

    x = x_ref[...]
    n = x.shape[0]
    u = _causal_conv(x, tail_ref[...], cw_ref[...]) + cb_ref[...]
    tail_ref[...] = x[n - SUBLANES:, :]
    ubs = _heads(u, LRU_BLOCK, LRU_BLOCKS)
    r = jnp.concatenate([_dot(ub, wa_ref[b]) for b, ub in enumerate(ubs)], axis=1)
    i = jnp.concatenate([_dot(ub, wx_ref[b]) for b, ub in enumerate(ubs)], axis=1)
    r = _sigmoid(r + ba_ref[...])
    i = _sigmoid(i + bx_ref[...])
    log_a = (-LRU_C * r) * _softplus(-lam_ref[...])
    a = jnp.exp(log_a)
    xin = jnp.sqrt(-jnp.tanh(log_a) * (a * a + 1.0)) * (i * u)
    pos = _iota(a.shape, 0) & (SUBLANES - 1)
    s = 1
    while s < SUBLANES:
        keep = pos >= s
        a_sh = jnp.where(keep, pltpu.roll(a, s, 0), 1.0)
        x_sh = jnp.where(keep, pltpu.roll(xin, s, 0), 0.0)
        xin = a * x_sh + xin
        a = a * a_sh
        s *= 2
    carry = h_ref[0:1, :]
    groups = []
    for g in range(n // SUBLANES):
        rows = slice(g * SUBLANES, (g + 1) * SUBLANES)
        hg = xin[rows] + a[rows] * carry
        groups.append(hg)
        carry = hg[SUBLANES - 1:SUBLANES, :]
    h = jnp.concatenate(groups, axis=0)
    h_ref[...] = jnp.broadcast_to(carry, h_ref.shape)
    o_ref[...] = (h * _gelu_tanh(gate_ref[...])).astype(o_ref.dtype)


def _lru(proj, batch, seq, layer, cw, cb, wa, ba, wx, bx, lam):
    t = proj.shape[0]
    nsb = seq // SEQ_BLOCK
    row = lambda b, s: b * nsb + s
    vec = lambda a: a.reshape(a.shape[0], 1, BW)
    vspec = pl.BlockSpec((None, 1, BW), lambda b, s: (layer, 0, 0))
    wspec = pl.BlockSpec((None, LRU_BLOCKS, LRU_BLOCK, LRU_BLOCK), lambda b, s: (layer, 0, 0, 0))
    return pl.pallas_call(
        _lru_kernel,
        grid=(batch, nsb),
        in_specs=[pl.BlockSpec((SEQ_BLOCK, BW), lambda b, s: (row(b, s), OFF_AX // BW)),
                  pl.BlockSpec((SEQ_BLOCK, BW), lambda b, s: (row(b, s), OFF_AG // BW)),
                  pl.BlockSpec((None, CONV_WIDTH, BW), lambda b, s: (layer, 0, 0)),
                  vspec, wspec, vspec, wspec, vspec, vspec],
        out_specs=pl.BlockSpec((SEQ_BLOCK, BW), lambda b, s: (row(b, s), 0)),
        out_shape=jax.ShapeDtypeStruct((t, BW), BF16),
        scratch_shapes=[pltpu.VMEM((SUBLANES, BW), F32), pltpu.VMEM((SUBLANES, BW), F32)],
        compiler_params=_cparams(("parallel", "arbitrary")),
        name="rg_lru",
    )(proj, proj, cw, vec(cb), wa, vec(ba), wx, vec(bx), vec(lam))


def _gla_kernel(q_ref, k_ref, v_ref, g_ref, sm_ref, wd_ref, bd_ref, ng_ref, o_ref, st_ref):
    @pl.when(pl.program_id(1) == 0)
    def _():
        st_ref[...] = jnp.zeros_like(st_ref)

    nh, dk, dv = GLA_HEADS, GLA_DK, GLA_DV
    _, _, tril, _ = _block_masks(SEQ_BLOCK)
    log_alpha = _log_sigmoid(_dot(sm_ref[...], wd_ref[...], HI) + bd_ref[...]) / GLA_TAU
    bcum = _chunk_cumsum(log_alpha)
    b_last = _chunk_last(bcum)
    k = k_ref[...]
    q_decs = _heads(q_ref[...] * (GLA_DK ** -0.5) * jnp.exp(bcum), dk, nh)
    k_decs = _heads(k * jnp.exp(-bcum), dk, nh)
    k_tails = _heads(k * jnp.exp(b_last - bcum), dk, nh)
    b_lasts = _heads(b_last, dk, nh)
    vs = _heads(v_ref[...], dv, nh)
    attns = [jnp.where(tril, _dot_nt(qd, kd), 0.0) for qd, kd in zip(q_decs, k_decs)]
    o_intras = [_dot(a, v) for a, v in zip(attns, vs)]
    sts = [st_ref[h] for h in range(nh)]
    for c in range(N_CHUNKS):
        sl = slice(c * CHUNK, (c + 1) * CHUNK)
        outs = [oi[sl] + _dot_nt(qd[sl], st) for oi, qd, st in zip(o_intras, q_decs, sts)]
        kvs = [_dot_tn(v[sl], kt[sl]) for v, kt in zip(vs, k_tails)]
        sts = [st * jnp.exp(bl[c * CHUNK:c * CHUNK + 1, :]) + kv for st, bl, kv in zip(sts, b_lasts, kvs)]
        for h, o in enumerate(outs):
            cols = slice(h * dv, (h + 1) * dv)
            o = o * lax.rsqrt(jnp.mean(o * o, axis=-1, keepdims=True) + NORM_EPS) * ng_ref[...]
            o_ref[sl, cols] = (o * _silu(g_ref[sl, cols])).astype(o_ref.dtype)
    for h in range(nh):
        st_ref[h] = sts[h]


def _gla(proj, small, batch, seq, layer, w_decay, b_decay, norm_g):
    t = proj.shape[0]
    nsb = seq // SEQ_BLOCK
    wq = GLA_HEADS * GLA_DK
    row = lambda b, s: b * nsb + s
    depth = w_decay.shape[0]
    wd = jnp.zeros((depth, LANES, wq), F32).at[:, LANE_DECAY:LANE_DECAY + GLA_RANK].set(w_decay)
    return pl.pallas_call(
        _gla_kernel,
        grid=(batch, nsb),
        in_specs=[pl.BlockSpec((SEQ_BLOCK, wq), lambda b, s: (row(b, s), OFF_BQ // wq)),
                  pl.BlockSpec((SEQ_BLOCK, wq), lambda b, s: (row(b, s), OFF_BK // wq)),
                  pl.BlockSpec((SEQ_BLOCK, BW), lambda b, s: (row(b, s), OFF_BV // BW)),
                  pl.BlockSpec((SEQ_BLOCK, BW), lambda b, s: (row(b, s), OFF_BG // BW)),
                  pl.BlockSpec((SEQ_BLOCK, LANES), lambda b, s: (row(b, s), 0)),
                  pl.BlockSpec((None, LANES, wq), lambda b, s: (layer, 0, 0)),
                  pl.BlockSpec((None, 1, wq), lambda b, s: (layer, 0, 0)),
                  pl.BlockSpec((None, 1, GLA_DV), lambda b, s: (layer, 0, 0))],
        out_specs=pl.BlockSpec((SEQ_BLOCK, BW), lambda b, s: (row(b, s), 0)),
        out_shape=jax.ShapeDtypeStruct((t, BW), BF16),
        scratch_shapes=[pltpu.VMEM((GLA_HEADS, GLA_DV, GLA_DK), F32)],
        compiler_params=_cparams(("parallel", "arbitrary")),
        name="gla",
    )(proj, proj, proj, proj, small, wd, b_decay.reshape(depth, 1, wq), norm_g.reshape(depth, 1, GLA_DV))


def _unit_lower_inverse(lows, ri, ci):
    eye = (ri == ci).astype(F32)
    same16 = (ri // 16) == (ci // 16)
    same32 = (ri // 32) == (ci // 32)
    d1 = [jnp.where(same16, low, 0.0) for low in lows]
    d2 = [_dot(a, a) for a in d1]
    m = [eye - a for a in d1]
    d4 = [_dot(a, a) for a in d2]
    m = [a + _dot(a, b) for a, b in zip(m, d2)]
    d8 = [_dot(a, a) for a in d4]
    m = [a + _dot(a, b) for a, b in zip(m, d4)]
    m = [a + _dot(a, b) for a, b in zip(m, d8)]
    c1 = [jnp.where(same32 & ~same16, low, 0.0) for low in lows]
    t = [_dot(a, b) for a, b in zip(m, c1)]
    m = [a - _dot(b, a) for a, b in zip(m, t)]
    c2 = [jnp.where(same32, 0.0, low) for low in lows]
    t = [_dot(a, b) for a, b in zip(m, c2)]
    m = [a - _dot(b, a) for a, b in zip(m, t)]
    return m


def _gdn_kernel(q_ref, k_ref, v_ref, g_ref, sm_ref, cwq_ref, cwk_ref, cwv_ref, alog_ref, dtb_ref,
                ng_ref, o_ref, st_ref, tq_ref, tk_ref, tv_ref):
    @pl.when(pl.program_id(1) == 0)
    def _():
        st_ref[...] = jnp.zeros_like(st_ref)
        tq_ref[...] = jnp.zeros_like(tq_ref)
        tk_ref[...] = jnp.zeros_like(tk_ref)
        tv_ref[...] = jnp.zeros_like(tv_ref)

    n = SEQ_BLOCK
    nh, hd = GDN_HEADS, GDN_HEAD_DIM
    q_raw, k_raw, v_raw = q_ref[...], k_ref[...], v_ref[...]
    qs = _heads(_silu(_causal_conv(q_raw, tq_ref[...], cwq_ref[...])), hd, nh)
    ks = _heads(_silu(_causal_conv(k_raw, tk_ref[...], cwk_ref[...])), hd, nh)
    vs = _heads(_silu(_causal_conv(v_raw, tv_ref[...], cwv_ref[...])), hd, nh)
    tq_ref[...] = q_raw[n - SUBLANES:, :]
    tk_ref[...] = k_raw[n - SUBLANES:, :]
    tv_ref[...] = v_raw[n - SUBLANES:, :]
    sm = sm_ref[...]
    beta_lanes = _sigmoid(sm)
    gcum_lanes = _chunk_cumsum(-jnp.exp(alog_ref[...]) * _softplus(sm + dtb_ref[...]))
    ri, ci, tril, strict = _block_masks(n)

    qs = [q * lax.rsqrt(jnp.sum(q * q, axis=-1, keepdims=True) + NORM_EPS) * (hd ** -0.5) for q in qs]
    ks = [k * lax.rsqrt(jnp.sum(k * k, axis=-1, keepdims=True) + NORM_EPS) for k in ks]
    betas = [beta_lanes[:, LANE_BETA + h:LANE_BETA + h + 1] for h in range(nh)]
    gcums = [gcum_lanes[:, LANE_ALPHA + h:LANE_ALPHA + h + 1] for h in range(nh)]
    g_lasts = [_chunk_last(g) for g in gcums]
    gbs = [jnp.broadcast_to(g, (n, n)) for g in gcums]
    decays = [jnp.where(tril, jnp.exp(jnp.where(tril, gb - gb.T, 0.0)), 0.0) for gb in gbs]
    kks = [_dot_nt(k, k) for k in ks]
    lows = [jnp.where(strict, b * kk * d, 0.0) for b, kk, d in zip(betas, kks, decays)]
    qks = [_dot_nt(q, k) * d for q, k, d in zip(qs, ks, decays)]
    minvs = _unit_lower_inverse(lows, ri, ci)
    e_gs = [jnp.exp(g) for g in gcums]
    uws = [_dot(mi, jnp.concatenate([v * b, k * (b * e)], axis=1))
           for mi, v, k, b, e in zip(minvs, vs, ks, betas, e_gs)]
    ps = [_dot(qk, uw) for qk, uw in zip(qks, uws)]
    q_effs = [q * e - p[:, hd:] for q, e, p in zip(qs, e_gs, ps)]
    k_tails = [k * jnp.exp(gl - g) for k, gl, g in zip(ks, g_lasts, gcums)]
    sts = [st_ref[h] for h in range(nh)]
    for c in range(N_CHUNKS):
        sl = slice(c * CHUNK, (c + 1) * CHUNK)
        outs = [p[sl, :hd] + _dot(qe[sl], st) for p, qe, st in zip(ps, q_effs, sts)]
        abs_ = [_dot_tn(kt[sl], uw[sl]) for kt, uw in zip(k_tails, uws)]
        sts = [st * jnp.exp(gl[c * CHUNK:c * CHUNK + 1, :]) + ab[:, :hd] - _dot(ab[:, hd:], st)
               for gl, st, ab in zip(g_lasts, sts, abs_)]
        for h, o in enumerate(outs):
            cols = slice(h * hd, (h + 1) * hd)
            o = o * lax.rsqrt(jnp.mean(o * o, axis=-1, keepdims=True) + NORM_EPS) * ng_ref[...]
            o_ref[sl, cols] = (o * _silu(g_ref[sl, cols])).astype(o_ref.dtype)
    for h in range(nh):
        st_ref[h] = sts[h]


def _gdn(proj, small, batch, seq, layer, conv_w, a_log, dt_bias, norm_g):
    t = proj.shape[0]
    nsb = seq // SEQ_BLOCK
    hd = GDN_HEAD_DIM
    row = lambda b, s: b * nsb + s
    depth = a_log.shape[0]
    lane_row = lambda a: jnp.zeros((depth, 1, LANES), F32).at[:, 0, LANE_ALPHA:LANE_ALPHA + GDN_HEADS].set(a)
    lspec = pl.BlockSpec((None, 1, LANES), lambda b, s: (layer, 0, 0))
    return pl.pallas_call(
        _gdn_kernel,
        grid=(batch, nsb),
        in_specs=[pl.BlockSpec((SEQ_BLOCK, BW), lambda b, s: (row(b, s), OFF_CQ // BW)),
                  pl.BlockSpec((SEQ_BLOCK, BW), lambda b, s: (row(b, s), OFF_CK // BW)),
                  pl.BlockSpec((SEQ_BLOCK, BW), lambda b, s: (row(b, s), OFF_CV // BW)),
                  pl.BlockSpec((SEQ_BLOCK, BW), lambda b, s: (row(b, s), OFF_CG // BW)),
                  pl.BlockSpec((SEQ_BLOCK, LANES), lambda b, s: (row(b, s), 0)),
                  pl.BlockSpec((None, CONV_WIDTH, BW), lambda b, s: (layer, 0, 0)),
                  pl.BlockSpec((None, CONV_WIDTH, BW), lambda b, s: (layer, 0, 1)),
                  pl.BlockSpec((None, CONV_WIDTH, BW), lambda b, s: (layer, 0, 2)),
                  lspec, lspec,
                  pl.BlockSpec((None, 1, hd), lambda b, s: (layer, 0, 0))],
        out_specs=pl.BlockSpec((SEQ_BLOCK, BW), lambda b, s: (row(b, s), 0)),
        out_shape=jax.ShapeDtypeStruct((t, BW), BF16),
        scratch_shapes=[pltpu.VMEM((GDN_HEADS, hd, hd), F32)] + [pltpu.VMEM((SUBLANES, BW), F32)] * 3,
        compiler_params=_cparams(("parallel", "arbitrary")),
        name="gdn",
    )(proj, proj, proj, proj, small, conv_w, conv_w, conv_w, lane_row(a_log), lane_row(dt_bias),
      norm_g.reshape(depth, 1, hd))


def _mlstm_kernel(q_ref, k_ref, v_ref, og_ref, sm_ref, bi_ref, bf_ref, ng_ref, o_ref,
                  c_ref, n_ref, m_ref):
    @pl.when(pl.program_id(1) == 0)
    def _():
        c_ref[...] = jnp.zeros_like(c_ref)
        n_ref[...] = jnp.zeros_like(n_ref)
        m_ref[...] = jnp.zeros_like(m_ref)

    n = SEQ_BLOCK
    nh, dqk, dv = MLSTM_HEADS, MLSTM_DQK, MLSTM_DV
    hr = range(nh)
    _, _, tril, _ = _block_masks(n)
    qs = _heads(q_ref[...], dqk, nh)
    ks = _heads(k_ref[...] * (MLSTM_DQK ** -0.5), dqk, nh)
    vs = _heads(v_ref[...], dv, nh)
    sm = sm_ref[...]
    ig_lanes = sm + bi_ref[...]
    bcum_lanes = _chunk_cumsum(_log_sigmoid(sm + bf_ref[...]))
    igs = [ig_lanes[:, LANE_I + h:LANE_I + h + 1] for h in hr]
    bcums = [bcum_lanes[:, LANE_F + h:LANE_F + h + 1] for h in hr]
    b_lasts = [_chunk_last(b) for b in bcums]
    logds = [jnp.where(tril, b + _row_form(i - b), -jnp.inf) for b, i in zip(bcums, igs)]
    m_intras = [jnp.max(ld, axis=1, keepdims=True) for ld in logds]
    tails = [bl - b + i for bl, b, i in zip(b_lasts, bcums, igs)]
    spread = lambda xs: jnp.concatenate([jnp.broadcast_to(x, (CHUNK, 1)) for x in xs], axis=0)
    m_prev_rows, a_max_rows, s_olds, s_news = [], [], [], []
    for h in hr:
        m_st = m_ref[h, 0:1, 0:1]
        m_prev, a_max, s_old, s_new = [], [], [], []
        for c in range(N_CHUNKS):
            a_c = jnp.max(tails[h][c * CHUNK:(c + 1) * CHUNK], axis=0, keepdims=True)
            bl_c = bcums[h][(c + 1) * CHUNK - 1:(c + 1) * CHUNK, :]
            m_new = jnp.maximum(bl_c + m_st, a_c)
            m_prev.append(m_st)
            a_max.append(a_c)
            s_old.append(jnp.exp(bl_c + m_st - m_new))
            s_new.append(jnp.exp(a_c - m_new))
            m_st = m_new
        m_ref[h] = jnp.broadcast_to(m_st, m_ref.shape[1:])
        m_prev_rows.append(spread(m_prev))
        a_max_rows.append(spread(a_max))
        s_olds.append(s_old)
        s_news.append(s_new)
    k_ws = [k * jnp.exp(t - a) for k, t, a in zip(ks, tails, a_max_rows)]
    m_inters = [b + mp for b, mp in zip(bcums, m_prev_rows)]
    m_is = [jnp.maximum(mi, mx) for mi, mx in zip(m_inters, m_intras)]
    s_inters = [jnp.exp(mi - m) for mi, m in zip(m_inters, m_is)]
    scs = [_dot_nt(q, k) * jnp.exp(ld - m) for q, k, ld, m in zip(qs, ks, logds, m_is)]
    num_intras = [_dot(sc, v) for sc, v in zip(scs, vs)]
    den_intras = [jnp.sum(sc, axis=1, keepdims=True) for sc in scs]
    floors = [jnp.exp(-m) for m in m_is]
    c_sts = [c_ref[h] for h in hr]
    n_sts = [n_ref[h, 0:1, :] for h in hr]
    for c in range(N_CHUNKS):
        sl = slice(c * CHUNK, (c + 1) * CHUNK)
        nums = [ni[sl] + si[sl] * _dot(q[sl], cs) for ni, si, q, cs in zip(num_intras, s_inters, qs, c_sts)]
        dens = [di[sl] + si[sl] * jnp.sum(q[sl] * ns, axis=1, keepdims=True)
                for di, si, q, ns in zip(den_intras, s_inters, qs, n_sts)]
        kvs = [_dot_tn(kw[sl], v[sl]) for kw, v in zip(k_ws, vs)]
        c_sts = [cs * so[c] + kv * sn[c] for cs, so, kv, sn in zip(c_sts, s_olds, kvs, s_news)]
        n_sts = [ns * so[c] + jnp.sum(kw[sl], axis=0, keepdims=True) * sn[c]
                 for ns, so, kw, sn in zip(n_sts, s_olds, k_ws, s_news)]
        for h in hr:
            cols = slice(h * dv, (h + 1) * dv)
            hh = nums[h] / jnp.maximum(jnp.abs(dens[h]), floors[h][sl])
            hh = hh * lax.rsqrt(jnp.mean(hh * hh, axis=-1, keepdims=True) + NORM_EPS) * ng_ref[...]
            o_ref[sl, cols] = (hh * _sigmoid(og_ref[sl, cols])).astype(o_ref.dtype)
    for h in hr:
        c_ref[h] = c_sts[h]
        n_ref[h] = jnp.broadcast_to(n_sts[h], n_ref.shape[1:])


def _mlstm(proj, small, batch, seq, layer, b_i, b_f, norm_g):
    t = proj.shape[0]
    nsb = seq // SEQ_BLOCK
    nh, dqk, dv = MLSTM_HEADS, MLSTM_DQK, MLSTM_DV
    wq = nh * dqk
    row = lambda b, s: b * nsb + s
    depth = b_i.shape[0]
    lane_row = lambda a, off: jnp.zeros((depth, 1, LANES), F32).at[:, 0, off:off + nh].set(a)
    lspec = pl.BlockSpec((None, 1, LANES), lambda b, s: (layer, 0, 0))
    return pl.pallas_call(
        _mlstm_kernel,
        grid=(batch, nsb),
        in_specs=[pl.BlockSpec((SEQ_BLOCK, wq), lambda b, s: (row(b, s), OFF_DQ // wq)),
                  pl.BlockSpec((SEQ_BLOCK, wq), lambda b, s: (row(b, s), OFF_DK // wq)),
                  pl.BlockSpec((SEQ_BLOCK, BW), lambda b, s: (row(b, s), OFF_DV // BW)),
                  pl.BlockSpec((SEQ_BLOCK, BW), lambda b, s: (row(b, s), OFF_DO // BW)),
                  pl.BlockSpec((SEQ_BLOCK, LANES), lambda b, s: (row(b, s), 0)),
                  lspec, lspec,
                  pl.BlockSpec((None, 1, dv), lambda b, s: (layer, 0, 0))],
        out_specs=pl.BlockSpec((SEQ_BLOCK, BW), lambda b, s: (row(b, s), 0)),
        out_shape=jax.ShapeDtypeStruct((t, BW), BF16),
        scratch_shapes=[pltpu.VMEM((nh, dqk, dv), F32), pltpu.VMEM((nh, SUBLANES, dqk), F32),
                        pltpu.VMEM((nh, SUBLANES, LANES), F32)],
        compiler_params=_cparams(("parallel", "arbitrary")),
        name="mlstm",
    )(proj, proj, proj, proj, small, lane_row(b_i, LANE_I), lane_row(b_f, LANE_F),
      norm_g.reshape(depth, 1, dv))


def _merge_kernel(xn_ref, ya_ref, yb_ref, yc_ref, yd_ref, wg_ref, bg_ref, wb_ref, o_ref):
    xn = xn_ref[...]
    acc = None
    for n, y_ref in enumerate((ya_ref, yb_ref, yc_ref, yd_ref)):
        gate = _sigmoid(_dot(xn, wg_ref[n]) + bg_ref[n])
        term = gate * _dot(y_ref[...], wb_ref[n])
        acc = term if acc is None else acc + term
    o_ref[...] = acc.astype(o_ref.dtype)


def _merge(xn, ys, layer, wg, bg, wb, tm=512, tn=256):
    t, d = xn.shape
    tm = min(tm, t)
    depth, nb = wg.shape[:2]
    yspec = pl.BlockSpec((tm, BW), lambda i, j: (i, 0))
    return pl.pallas_call(
        _merge_kernel,
        grid=(t // tm, d // tn),
        in_specs=[pl.BlockSpec((tm, d), lambda i, j: (i, 0)), yspec, yspec, yspec, yspec,
                  pl.BlockSpec((None, nb, d, tn), lambda i, j: (layer, 0, 0, j)),
                  pl.BlockSpec((None, nb, 1, tn), lambda i, j: (layer, 0, 0, j)),
                  pl.BlockSpec((None, nb, BW, tn), lambda i, j: (layer, 0, 0, j))],
        out_specs=pl.BlockSpec((tm, tn), lambda i, j: (i, j)),
        out_shape=jax.ShapeDtypeStruct((t, d), BF16),
        compiler_params=_cparams(("parallel", "parallel")),
        name="gated_merge",
    )(xn, *ys, wg, bg.reshape(depth, nb, 1, d), wb)


def _router_kernel(x_ref, g_ref, wrh_ref, wrl_ref, br_ref, h_ref, info_ref, idx_ref, cnt_ref, carry_ref):
    @pl.when(pl.program_id(0) == 0)
    def _():
        carry_ref[...] = jnp.zeros_like(carry_ref)

    x = x_ref[...]
    tm = x.shape[0]
    h = x * lax.rsqrt(jnp.mean(x * x, axis=-1, keepdims=True) + NORM_EPS) * g_ref[...]
    h_ref[...] = _pack_pairs(h)
    h_hi = h.astype(BF16)
    h_lo = (h - h_hi.astype(F32)).astype(BF16)
    logits = (_dot(h_hi, wrh_ref[...]) + (_dot(h_lo, wrh_ref[...]) + _dot(h_hi, wrl_ref[...]))
              + br_ref[...])
    lane = _iota((tm, LANES), 1).astype(F32)
    big = float(LANES)
    gl = jnp.where(lane < N_GROUPS, logits, -jnp.inf)
    gmax = jnp.max(gl, axis=1, keepdims=True)
    grp = jnp.min(jnp.where(gl == gmax, lane, big), axis=1, keepdims=True)
    p_grp = 1.0 / jnp.sum(jnp.exp(gl - gmax), axis=1, keepdims=True)
    lo = N_GROUPS + grp * EXPERTS_PER_GROUP
    el = jnp.where((lane >= lo) & (lane < lo + EXPERTS_PER_GROUP), logits, -jnp.inf)
    v0 = jnp.max(el, axis=1, keepdims=True)
    i0 = jnp.min(jnp.where(el == v0, lane, big), axis=1, keepdims=True)
    el = jnp.where(lane == i0, -jnp.inf, el)
    v1 = jnp.max(el, axis=1, keepdims=True)
    i1 = jnp.min(jnp.where(el == v1, lane, big), axis=1, keepdims=True)
    e = jnp.exp(v1 - v0)
    w0 = p_grp / (1.0 + e)
    w1 = p_grp * e / (1.0 + e)
    e0 = i0 - N_GROUPS
    e1 = i1 - N_GROUPS
    oh0 = (lane == e0).astype(F32)
    oh1 = (lane == e1).astype(F32)
    both = oh0 + oh1
    strict = (_iota((tm, tm), 0) > _iota((tm, tm), 1)).astype(BF16)
    before = carry_ref[0:1, :] + _dot(strict, both.astype(BF16))
    r0 = jnp.sum(oh0 * before, axis=1, keepdims=True)
    r1 = jnp.sum(oh1 * before, axis=1, keepdims=True)
    total = carry_ref[0:1, :] + jnp.sum(both, axis=0, keepdims=True)
    carry_ref[...] = jnp.broadcast_to(total, carry_ref.shape)
    cnt_ref[...] = jnp.broadcast_to(total, cnt_ref.shape)
    info_ref[...] = jnp.where(lane == 0.0, w0, jnp.where(lane == 1.0, w1, 0.0))
    idx = jnp.where(lane == 0.0, e0, jnp.where(lane == 1.0, e1,
                                               jnp.where(lane == 2.0, r0, jnp.where(lane == 3.0, r1, 0.0))))
    idx_ref[...] = idx.astype(jnp.int32)


def _router(x2d, layer, g, w_grp, b_grp, w_exp, b_exp, tm=256):
    t, d = x2d.shape
    tm = min(tm, t)
    depth = g.shape[0]
    wr = (jnp.zeros((depth, d, LANES), F32).at[:, :, :N_GROUPS].set(w_grp)
          .at[:, :, N_GROUPS:N_GROUPS + N_EXPERTS].set(w_exp))
    br = (jnp.zeros((depth, 1, LANES), F32).at[:, 0, :N_GROUPS].set(b_grp)
          .at[:, 0, N_GROUPS:N_GROUPS + N_EXPERTS].set(b_exp))
    wr_hi = wr.astype(BF16)
    wr_lo = (wr - wr_hi.astype(F32)).astype(BF16)
    return pl.pallas_call(
        _router_kernel,
        grid=(t // tm,),
        in_specs=[pl.BlockSpec((tm, d), lambda i: (i, 0)),
                  pl.BlockSpec((None, 1, d), lambda i: (layer, 0, 0)),
                  pl.BlockSpec((None, d, LANES), lambda i: (layer, 0, 0)),
                  pl.BlockSpec((None, d, LANES), lambda i: (layer, 0, 0)),
                  pl.BlockSpec((None, 1, LANES), lambda i: (layer, 0, 0))],
        out_specs=[pl.BlockSpec((tm, d // 2), lambda i: (i, 0)),
                   pl.BlockSpec((tm, LANES), lambda i: (i, 0)),
                   pl.BlockSpec((tm, LANES), lambda i: (i, 0)),
                   pl.BlockSpec((SUBLANES, LANES), lambda i: (0, 0))],
        out_shape=[jax.ShapeDtypeStruct((t, d // 2), jnp.uint32),
                   jax.ShapeDtypeStruct((t, LANES), F32),
                   jax.ShapeDtypeStruct((t, LANES), jnp.int32),
                   jax.ShapeDtypeStruct((SUBLANES, LANES), F32)],
        scratch_shapes=[pltpu.VMEM((SUBLANES, LANES), F32)],
        compiler_params=_cparams(("arbitrary",)),
        name="router",
    )(x2d, g.reshape(depth, 1, d), wr_hi, wr_lo, br)


def _dispatch_kernel(pos_ref, zrow_ref, h_ref, xs_ref, zero_ref, sem, zsem):
    i = pl.program_id(0)
    tm = h_ref.shape[0]
    zt = zero_ref.shape[0]

    @pl.when(i == 0)
    def _():
        zero_ref[...] = jnp.zeros_like(zero_ref)

        def zero_copy(e):
            start = pl.multiple_of(jnp.maximum(zrow_ref[e], 0), zt)
            return pltpu.make_async_copy(zero_ref, xs_ref.at[pl.ds(start, zt)], zsem)

        def zissue(e, carry):
            @pl.when(zrow_ref[e] >= 0)
            def _():
                zero_copy(e).start()
            return carry

        def zdrain(e, carry):
            @pl.when(zrow_ref[e] >= 0)
            def _():
                zero_copy(e).wait()
            return carry

        lax.fori_loop(0, zrow_ref.shape[0], zissue, 0)
        lax.fori_loop(0, zrow_ref.shape[0], zdrain, 0)

    def row_copy(r, p):
        return pltpu.make_async_copy(h_ref.at[pl.ds(r, 1)], xs_ref.at[pl.ds(p, 1)], sem)

    def issue(r, carry):
        t = i * tm + r
        for k in range(2):
            row_copy(r, pos_ref[2 * t + k]).start()
        return carry

    lax.fori_loop(0, tm, issue, 0)
    for k in range(2):
        pltpu.make_async_copy(h_ref, xs_ref.at[pl.ds(0, tm)], sem).wait()


def _dispatch(pos, zrow, h, n_rows, tm=ROW_TILE):
    t, d = h.shape
    tm = min(tm, t)
    return pl.pallas_call(
        _dispatch_kernel,
        grid_spec=pltpu.PrefetchScalarGridSpec(
            num_scalar_prefetch=2,
            grid=(t // tm,),
            in_specs=[pl.BlockSpec((tm, d), lambda i, pos, zrow: (i, 0))],
            out_specs=pl.BlockSpec(memory_space=pl.ANY),
            scratch_shapes=[pltpu.VMEM((EXPERT_TILE, d), h.dtype), pltpu.SemaphoreType.DMA(()),
                            pltpu.SemaphoreType.DMA(())],
        ),
        out_shape=jax.ShapeDtypeStruct((n_rows, d), h.dtype),
        compiler_params=_cparams(("arbitrary",)),
        name="dispatch",
    )(pos, zrow, h)


def _combine_kernel(keep_x, pos_ref, x_ref, info_ref, g_ref, ys_ref, *rest):
    outs, (buf_ref, sem) = rest[:-2], rest[-2:]
    i = pl.program_id(0)
    n = pl.num_programs(0)
    tm = x_ref.shape[0]
    half = x_ref.shape[1] // 2

    def row_copy(slot, r, k, p):
        return pltpu.make_async_copy(ys_ref.at[pl.ds(p, 1)], buf_ref.at[slot, k, pl.ds(r, 1)],
                                     sem.at[slot])

    def gather_tile(tile, slot):
        def issue(r, carry):
            t = tile * tm + r
            for k in range(2):
                row_copy(slot, r, k, pos_ref[2 * t + k]).start()
            return carry
        lax.fori_loop(0, tm, issue, 0)

    @pl.when(i == 0)
    def _():
        gather_tile(0, 0)

    @pl.when(i + 1 < n)
    def _():
        gather_tile(i + 1, (i + 1) % 2)

    slot = i % 2

    for k in range(2):
        pltpu.make_async_copy(ys_ref.at[pl.ds(0, tm)], buf_ref.at[slot, k], sem.at[slot]).wait()
    info = info_ref[...]
    w0, w1 = info[:, 0:1], info[:, 1:2]
    lo0, hi0 = _unpack_pairs(buf_ref[slot, 0])
    lo1, hi1 = _unpack_pairs(buf_ref[slot, 1])
    x = x_ref[...]
    x_lo = x[:, :half] + w0 * lo0 + w1 * lo1
    x_hi = x[:, half:] + w0 * hi0 + w1 * hi1
    ms = (jnp.sum(x_lo * x_lo, axis=-1, keepdims=True)
          + jnp.sum(x_hi * x_hi, axis=-1, keepdims=True)) / x.shape[1]
    inv = lax.rsqrt(ms + NORM_EPS)
    g = g_ref[...]
    if keep_x:
        outs[0][:, :half] = x_lo
        outs[0][:, half:] = x_hi
    outs[-1][:, :half] = (x_lo * inv * g[:, :half]).astype(outs[-1].dtype)
    outs[-1][:, half:] = (x_hi * inv * g[:, half:]).astype(outs[-1].dtype)


def _combine(pos, x2d, info, ys, g, keep_x, norm_dtype, tm=ROW_TILE):
    t, d = x2d.shape
    tm = min(tm, t)
    row_spec = pl.BlockSpec((tm, d), lambda i, pos: (i, 0))
    out_specs = [row_spec] * (2 if keep_x else 1)
    out_shape = ([jax.ShapeDtypeStruct((t, d), F32)] if keep_x else []) + [jax.ShapeDtypeStruct((t, d), norm_dtype)]
    return pl.pallas_call(
        functools.partial(_combine_kernel, keep_x),
        grid_spec=pltpu.PrefetchScalarGridSpec(
            num_scalar_prefetch=1,
            grid=(t // tm,),
            in_specs=[row_spec,
                      pl.BlockSpec((tm, LANES), lambda i, pos: (i, 0)),
                      pl.BlockSpec((1, d), lambda i, pos: (0, 0)),
                      pl.BlockSpec(memory_space=pl.ANY)],
            out_specs=out_specs,
            scratch_shapes=[pltpu.VMEM((2, 2, tm, d // 2), ys.dtype),
                            pltpu.SemaphoreType.DMA((2,))],
        ),
        out_shape=out_shape,
        compiler_params=_cparams(("arbitrary",)),
        name="combine",
    )(pos, x2d, info, g.reshape(1, d), ys)


def _new_expert(te_ref, i):
    return (i == 0) | (te_ref[i] != te_ref[jnp.maximum(i - 1, 0)])


def _expert_kernel(te_ref, nu_ref, nx_ref, xs_ref, wg_hbm, wu_hbm, wd_hbm, ys_ref,
                   fg_ref, fu_ref, fd_ref, bg_ref, bu_ref, bd_ref, sem):
    i = pl.program_id(0)

    def fetch(e):
        return (pltpu.make_async_copy(wg_hbm.at[e], fg_ref, sem.at[0]),
                pltpu.make_async_copy(wu_hbm.at[e], fu_ref, sem.at[1]),
                pltpu.make_async_copy(wd_hbm.at[e], fd_ref, sem.at[2]))

    @pl.when(i == 0)
    def _():
        for copy in fetch(te_ref[0]):
            copy.start()

    @pl.when(_new_expert(te_ref, i))
    def _():
        for copy in fetch(te_ref[i]):
            copy.wait()
        bg_ref[...] = fg_ref[...].astype(BF16)
        bu_ref[...] = fu_ref[...].astype(BF16)
        bd_ref[...] = fd_ref[...].astype(BF16)

        @pl.when(nx_ref[i] >= 0)
        def _():
            for copy in fetch(nx_ref[i]):
                copy.start()

    @pl.when(i < nu_ref[0])
    def _():
        lo, hi = _unpack_pairs(xs_ref[...])
        x = jnp.concatenate([lo.astype(BF16), hi.astype(BF16)], axis=1)
        a = _dot(x, bg_ref[...])
        u = _dot(x, bu_ref[...])
        act = (_silu(a) * u).astype(BF16)
        ys_ref[...] = _pack_pairs(_dot(act, bd_ref[...]))

    @pl.when(i >= nu_ref[0])
    def _():
        ys_ref[...] = jnp.zeros_like(ys_ref)


def _experts(tile_expert, n_used, next_expert, xs, wg, wu, wd):
    p, dh = xs.shape
    d, f = wg.shape[1:]
    tm = EXPERT_TILE
    src_tile = lambda i, te, nu, nx: (jnp.maximum(jnp.minimum(i, nu[0] - 1), 0), 0)
    hbm = pl.BlockSpec(memory_space=pl.ANY)
    return pl.pallas_call(
        _expert_kernel,
        grid_spec=pltpu.PrefetchScalarGridSpec(
            num_scalar_prefetch=3,
            grid=(p // tm,),
            in_specs=[pl.BlockSpec((tm, dh), src_tile), hbm, hbm, hbm],
            out_specs=pl.BlockSpec((tm, dh), lambda i, te, nu, nx: (i, 0)),
            scratch_shapes=[pltpu.VMEM((d, f), F32), pltpu.VMEM((d, f), F32), pltpu.VMEM((f, d), F32),
                            pltpu.VMEM((d, f), BF16), pltpu.VMEM((d, f), BF16), pltpu.VMEM((f, d), BF16),
                            pltpu.SemaphoreType.DMA((3,))],
        ),
        out_shape=jax.ShapeDtypeStruct((p, dh), xs.dtype),
        compiler_params=_cparams(("arbitrary",)),
        name="experts",
    )(tile_expert, n_used, next_expert, xs, wg, wu, wd)


def _hier_moe(x2d, layer, ffn_g, w_grp, b_grp, w_exp, b_exp, wg, wu, wd, next_g, keep_x, norm_dtype):
    t, d = x2d.shape
    tm = EXPERT_TILE
    n_rows = 2 * t + N_EXPERTS * tm
    h, info, idx, cnt = _router(x2d, layer, ffn_g, w_grp, b_grp, w_exp, b_exp)
    counts = cnt[0, :N_EXPERTS].astype(jnp.int32)
    padded = ((counts + tm - 1) // tm) * tm
    ends = jnp.cumsum(padded)
    offs = ends - padded
    pos = (offs[idx[:, 0:2]] + idx[:, 2:4]).reshape(2 * t)
    n_used = (ends[-1] // tm).reshape(1).astype(jnp.int32)
    tail = n_used[0] + jnp.arange(N_EXPERTS, dtype=jnp.int32)
    zrow = jnp.concatenate([jnp.where(padded > 0, ends - tm, -1),
                            jnp.where(tail < n_rows // tm, tail * tm, -1)]).astype(jnp.int32)
    tiles = jnp.minimum(jnp.arange(n_rows // tm, dtype=jnp.int32), n_used[0] - 1)
    tile_expert = jnp.sum(((ends // tm)[None, :] <= tiles[:, None]).astype(jnp.int32), axis=1)
    tile_expert = jnp.minimum(tile_expert, N_EXPERTS - 1)
    ids = jnp.arange(N_EXPERTS, dtype=jnp.int32)
    later = (ids[None, :] > ids[:, None]) & (padded[None, :] > 0)
    following = jnp.min(jnp.where(later, ids[None, :], N_EXPERTS), axis=1)
    following = jnp.where(following < N_EXPERTS, following + layer * N_EXPERTS, -1).astype(jnp.int32)
    next_expert = following[tile_expert]
    tile_expert = tile_expert + layer * N_EXPERTS
    xs = _dispatch(pos, zrow, h, n_rows)
    ys = _experts(tile_expert, n_used, next_expert, xs, wg, wu, wd)
    return _combine(pos, x2d, info, ys, next_g, keep_x, norm_dtype)


def _repack_kernel(w_ref, big_ref, small_ref):
    c0 = 5 * BW
    c1 = c0 + GLA_RANK
    c2 = c1 + 4 * BW
    c3 = c2 + 2 * GDN_HEADS
    c4 = c3 + 3 * BW
    c5 = c4 + 2 * MLSTM_HEADS
    step = 512
    for dst, src, n in ((0, 0, c0), (c0, c1, c2 - c1), (c0 + c2 - c1, c3, c4 - c3)):
        for o in range(0, n, step):
            big_ref[:, dst + o:dst + o + step] = w_ref[src + o:src + o + step, :].T.astype(BF16)
    n_small = GLA_RANK + 2 * GDN_HEADS + 2 * MLSTM_HEADS
    small = jnp.concatenate([w_ref[c0:c1, :], w_ref[c2:c3, :], w_ref[c4:c5, :],
                             jnp.zeros((LANES - n_small, w_ref.shape[1]), F32)], axis=0)
    small_ref[...] = small.T.astype(BF16)


def _split_w_in(w, tk=256):
    depth, k, n = w.shape
    w = jnp.swapaxes(w, 1, 2)
    return pl.pallas_call(
        _repack_kernel,
        grid=(depth, k // tk),
        in_specs=[pl.BlockSpec((None, n, tk), lambda l, i: (l, 0, i))],
        out_specs=[pl.BlockSpec((None, tk, PROJ_BIG), lambda l, i: (l, i, 0)),
                   pl.BlockSpec((None, tk, LANES), lambda l, i: (l, i, 0))],
        out_shape=[jax.ShapeDtypeStruct((depth, k, PROJ_BIG), BF16),
                   jax.ShapeDtypeStruct((depth, k, LANES), BF16)],
        compiler_params=_cparams(("parallel", "parallel")),
        name="repack_w_in",
    )(w)


def kernel(x, mix_norm, w_in, lru_conv_w, lru_conv_b, lru_w_a, lru_b_a, lru_w_x, lru_b_x, lru_lambda,
           gla_w_decay, gla_b_decay, gla_norm, gdn_conv_w, gdn_a_log, gdn_dt_bias, gdn_norm,
           mlstm_b_i, mlstm_b_f, mlstm_norm, w_branch, w_merge_gate, b_merge_gate, w_out,
           ffn_norm, w_group_router, b_group_router, w_expert_router, b_expert_router,
           w_exp_gate, w_exp_up, w_exp_down, final_norm):
    batch, seq, d = x.shape
    depth = w_in.shape[0]
    x2d = x.reshape(batch * seq, d)
    w_big, w_small = _split_w_in(w_in)
    wg_merge = w_merge_gate.astype(BF16)
    wb_merge = w_branch.astype(BF16)
    w_o = w_out.astype(BF16)
    f = w_exp_gate.shape[-1]
    we_gate = w_exp_gate.reshape(depth * N_EXPERTS, d, f)
    we_up = w_exp_up.reshape(depth * N_EXPERTS, d, f)
    we_down = w_exp_down.reshape(depth * N_EXPERTS, f, d)
    xn = _rmsnorm(x2d, mix_norm[0], BF16)
    for l in range(depth):
        last = l == depth - 1
        proj = _matmul(xn, w_big, l, tm=1024, tn=1024, name="in_proj")
        small = _matmul(xn, w_small, l, name="in_proj_small")
        y_a = _lru(proj, batch, seq, l, lru_conv_w, lru_conv_b, lru_w_a, lru_b_a, lru_w_x, lru_b_x,
                   lru_lambda)
        y_b = _gla(proj, small, batch, seq, l, gla_w_decay, gla_b_decay, gla_norm)
        y_c = _gdn(proj, small, batch, seq, l, gdn_conv_w, gdn_a_log, gdn_dt_bias, gdn_norm)
        y_d = _mlstm(proj, small, batch, seq, l, mlstm_b_i, mlstm_b_f, mlstm_norm)
        merged = _merge(xn, (y_a, y_b, y_c, y_d), l, wg_merge, b_merge_gate, wb_merge)
        x2d = _matmul(merged, w_o, l, residual=x2d, tm=1024, name="out_proj")
        outs = _hier_moe(x2d, l, ffn_norm, w_group_router, b_group_router, w_expert_router,
                         b_expert_router, we_gate, we_up, we_down,
                         final_norm if last else mix_norm[l + 1], not last, F32 if last else BF16)
        if last:
            (y,) = outs
        else:
            x2d, xn = outs
    return y.reshape(batch, seq, d)
```

```python
import functools

import jax
import jax.numpy as jnp
from jax import lax
from jax.experimental import pallas as pl
from jax.experimental.pallas import tpu as pltpu

F32 = jnp.float32
BF16 = jnp.bfloat16
HI = lax.Precision.HIGHEST

D_MODEL = 4096
BW = D_MODEL // 4
CONV_WIDTH = 4
CHUNK = 64
NORM_EPS = 1e-6
LRU_BLOCKS = 8
LRU_BLOCK = BW // LRU_BLOCKS
LRU_C = 8.0
GLA_HEADS = 4
GLA_DV = BW // GLA_HEADS
GLA_DK = GLA_DV // 2
GLA_RANK = 16
GLA_TAU = 16.0
GDN_HEAD_DIM = 128
GDN_HEADS = BW // GDN_HEAD_DIM
MLSTM_HEADS = 4
MLSTM_DV = BW // MLSTM_HEADS
MLSTM_DQK = MLSTM_DV // 2
N_GROUPS = 4
EXPERTS_PER_GROUP = 8
N_EXPERTS = N_GROUPS * EXPERTS_PER_GROUP
D_EXPERT = D_MODEL // 8

LANES = 128
SUBLANES = 8
VMEM_LIMIT = 56 * 1024 * 1024

PROJ_BIG = 12 * BW
OFF_AX, OFF_AG = 0, BW
OFF_BQ, OFF_BK, OFF_BV, OFF_BG = 2 * BW, 2 * BW + 512, 3 * BW, 4 * BW
OFF_CQ, OFF_CK, OFF_CV, OFF_CG = 5 * BW, 6 * BW, 7 * BW, 8 * BW
OFF_DQ, OFF_DK, OFF_DV, OFF_DO = 9 * BW, 9 * BW + 512, 10 * BW, 11 * BW
LANE_DECAY, LANE_BETA, LANE_ALPHA, LANE_I, LANE_F = 0, 16, 24, 32, 36

SEQ_BLOCK = 256
N_CHUNKS = SEQ_BLOCK // CHUNK
EXPERT_TILE = 256
ROW_TILE = 256


def _cparams(sem):
    return pltpu.CompilerParams(dimension_semantics=sem, vmem_limit_bytes=VMEM_LIMIT)


def _softplus(x):
    return jnp.maximum(x, 0.0) + jnp.log1p(jnp.exp(-jnp.abs(x)))


def _log_sigmoid(x):
    return -_softplus(-x)


def _sigmoid(x):
    return 1.0 / (1.0 + jnp.exp(-x))


def _silu(x):
    return x * _sigmoid(x)


def _gelu_tanh(x):
    return 0.5 * x * (1.0 + jnp.tanh(0.7978845608028654 * (x + 0.044715 * (x * x * x))))


def _dot(a, b, precision=None):
    return jnp.dot(a, b, preferred_element_type=F32, precision=precision)


def _dot_nt(a, b, precision=None):
    return lax.dot_general(a, b, (((1,), (1,)), ((), ())), preferred_element_type=F32,
                           precision=precision)


def _dot_tn(a, b, precision=None):
    return lax.dot_general(a, b, (((0,), (0,)), ((), ())), preferred_element_type=F32,
                           precision=precision)


def _iota(shape, dim):
    return lax.broadcasted_iota(jnp.int32, shape, dim)


def _shift_rows(x, tail, s):
    r = pltpu.roll(x, s, 0)
    rt = pltpu.roll(tail, s, 0)
    head = jnp.where(_iota(tail.shape, 0) < s, rt, r[0:SUBLANES])
    return jnp.concatenate([head, r[SUBLANES:]], axis=0)


def _causal_conv(x, tail, w):
    y = x * w[CONV_WIDTH - 1:CONV_WIDTH, :]
    for s in range(1, CONV_WIDTH):
        y = y + _shift_rows(x, tail, s) * w[CONV_WIDTH - 1 - s:CONV_WIDTH - s, :]
    return y


def _chunk_cumsum(x):
    pos = _iota(x.shape, 0) & (CHUNK - 1)
    s = 1
    while s < CHUNK:
        x = x + jnp.where(pos >= s, pltpu.roll(x, s, 0), 0.0)
        s *= 2
    return x


def _chunk_last(x):
    return jnp.concatenate(
        [jnp.broadcast_to(x[(c + 1) * CHUNK - 1:(c + 1) * CHUNK], (CHUNK, x.shape[1]))
         for c in range(x.shape[0] // CHUNK)], axis=0)


def _row_form(col):
    n = col.shape[0]
    return jnp.broadcast_to(col, (n, n)).T


def _block_masks(n):
    ri = _iota((n, n), 0)
    ci = _iota((n, n), 1)
    same = (ri // CHUNK) == (ci // CHUNK)
    return ri, ci, same & (ri >= ci), same & (ri > ci)


def _pack_pairs(x):
    c = x.shape[1] // 2
    bits = lambda v: lax.bitcast_convert_type(v.astype(BF16).astype(F32), jnp.uint32)
    return (bits(x[:, :c]) >> 16) | bits(x[:, c:])


def _unpack_pairs(p):
    lo = lax.bitcast_convert_type(p << 16, F32)
    hi = lax.bitcast_convert_type(p & jnp.uint32(0xFFFF0000), F32)
    return lo, hi


def _heads(x, width, count):
    return [x[:, h * width:(h + 1) * width] for h in range(count)]


def _rmsnorm_kernel(x_ref, g_ref, o_ref):
    x = x_ref[...]
    ms = jnp.mean(x * x, axis=-1, keepdims=True)
    o_ref[...] = (x * lax.rsqrt(ms + NORM_EPS) * g_ref[...]).astype(o_ref.dtype)


def _rmsnorm(x2d, g, out_dtype, tm=512):
    t, d = x2d.shape
    return pl.pallas_call(
        _rmsnorm_kernel,
        grid=(t // tm,),
        in_specs=[pl.BlockSpec((tm, d), lambda i: (i, 0)),
                  pl.BlockSpec((1, d), lambda i: (0, 0))],
        out_specs=pl.BlockSpec((tm, d), lambda i: (i, 0)),
        out_shape=jax.ShapeDtypeStruct((t, d), out_dtype),
        compiler_params=_cparams(("parallel",)),
        name="rmsnorm",
    )(x2d, g.reshape(1, d))


def _mm_kernel(a_ref, w_ref, o_ref):
    o_ref[...] = _dot(a_ref[...], w_ref[...]).astype(o_ref.dtype)


def _mm_res_kernel(a_ref, w_ref, r_ref, o_ref):
    o_ref[...] = r_ref[...] + _dot(a_ref[...], w_ref[...])


def _matmul(a, w, layer, out_dtype=F32, residual=None, tm=512, tn=512, name="matmul"):
    m, k = a.shape
    n = w.shape[2]
    tm, tn = min(tm, m), min(tn, n)
    in_specs = [pl.BlockSpec((tm, k), lambda i, j: (i, 0)),
                pl.BlockSpec((None, k, tn), lambda i, j: (layer, 0, j))]
    args = [a, w]
    body = _mm_kernel
    if residual is not None:
        in_specs.append(pl.BlockSpec((tm, tn), lambda i, j: (i, j)))
        args.append(residual)
        body = _mm_res_kernel
    return pl.pallas_call(
        body,
        grid=(m // tm, n // tn),
        in_specs=in_specs,
        out_specs=pl.BlockSpec((tm, tn), lambda i, j: (i, j)),
        out_shape=jax.ShapeDtypeStruct((m, n), out_dtype),
        compiler_params=_cparams(("parallel", "parallel")),
        name=name,
    )(*args)


def _lru_kernel(x_ref, gate_ref, cw_ref, cb_ref, wa_ref, ba_ref, wx_ref, bx_ref, lam_ref,
                o_ref, tail_ref, h_ref):
    @pl.when(pl.program_id(1) == 0)
    def _():
        tail_ref[...] = jnp.zeros_like(tail_ref)
        h_ref[...] = jnp.zeros_like(h_ref)

    x = x_ref[...]
    n = x.shape[0]
    u = _causal_conv(x, tail_ref[...], cw_ref[...]) + cb_ref[...]
    tail_ref[...] = x[n - SUBLANES:, :]
    ubs = _heads(u, LRU_BLOCK, LRU_BLOCKS)
    r = jnp.concatenate([_dot(ub, wa_ref[b]) for b, ub in enumerate(ubs)], axis=1)
    i = jnp.concatenate([_dot(ub, wx_ref[b]) for b, ub in enumerate(ubs)], axis=1)
    r = _sigmoid(r + ba_ref[...])
    i = _sigmoid(i + bx_ref[...])
    log_a = (-LRU_C * r) * _softplus(-lam_ref[...])
    a = jnp.exp(log_a)
    xin = jnp.sqrt(-jnp.tanh(log_a) * (a * a + 1.0)) * (i * u)
    pos = _iota(a.shape, 0) & (SUBLANES - 1)
    s = 1
    while s < SUBLANES:
        keep = pos >= s
        a_sh = jnp.where(keep, pltpu.roll(a, s, 0), 1.0)
        x_sh = jnp.where(keep, pltpu.roll(xin, s, 0), 0.0)
        xin = a * x_sh + xin
        a = a * a_sh
        s *= 2
    carry = h_ref[0:1, :]
    groups = []
    for g in range(n // SUBLANES):
        rows = slice(g * SUBLANES, (g + 1) * SUBLANES)
        hg = xin[rows] + a[rows] * carry
        groups.append(hg)
        carry = hg[SUBLANES - 1:SUBLANES, :]
    h = jnp.concatenate(groups, axis=0)
    h_ref[...] = jnp.broadcast_to(carry, h_ref.shape)
    o_ref[...] = (h * _gelu_tanh(gate_ref[...])).astype(o_ref.dtype)


def _lru(proj, batch, seq, layer, cw, cb, wa, ba, wx, bx, lam):
    t = proj.shape[0]
    nsb = seq // SEQ_BLOCK
    row = lambda b, s: b * nsb + s
    vec = lambda a: a.reshape(a.shape[0], 1, BW)
    vspec = pl.BlockSpec((None, 1, BW), lambda b, s: (layer, 0, 0))
    wspec = pl.BlockSpec((None, LRU_BLOCKS, LRU_BLOCK, LRU_BLOCK), lambda b, s: (layer, 0, 0, 0))
    return pl.pallas_call(
        _lru_kernel,
        grid=(batch, nsb),
        in_specs=[pl.BlockSpec((SEQ_BLOCK, BW), lambda b, s: (row(b, s), OFF_AX // BW)),
                  pl.BlockSpec((SEQ_BLOCK, BW), lambda b, s: (row(b, s), OFF_AG // BW)),
                  pl.BlockSpec((None, CONV_WIDTH, BW), lambda b, s: (layer, 0, 0)),
                  vspec, wspec, vspec, wspec, vspec, vspec],
        out_specs=pl.BlockSpec((SEQ_BLOCK, BW), lambda b, s: (row(b, s), 0)),
        out_shape=jax.ShapeDtypeStruct((t, BW), BF16),
        scratch_shapes=[pltpu.VMEM((SUBLANES, BW), F32), pltpu.VMEM((SUBLANES, BW), F32)],
        compiler_params=_cparams(("parallel", "arbitrary")),
        name="rg_lru",
    )(proj, proj, cw, vec(cb), wa, vec(ba), wx, vec(bx), vec(lam))


def _gla_kernel(q_ref, k_ref, v_ref, g_ref, sm_ref, wd_ref, bd_ref, ng_ref, o_ref, st_ref):
    @pl.when(pl.program_id(1) == 0)
    def _():
        st_ref[...] = jnp.zeros_like(st_ref)

    nh, dk, dv = GLA_HEADS, GLA_DK, GLA_DV
    _, _, tril, _ = _block_masks(SEQ_BLOCK)
    log_alpha = _log_sigmoid(_dot(sm_ref[...], wd_ref[...], HI) + bd_ref[...]) / GLA_TAU
    bcum = _chunk_cumsum(log_alpha)
    b_last = _chunk_last(bcum)
    k = k_ref[...]
    q_decs = _heads(q_ref[...] * (GLA_DK ** -0.5) * jnp.exp(bcum), dk, nh)
    k_decs = _heads(k * jnp.exp(-bcum), dk, nh)
    k_tails = _heads(k * jnp.exp(b_last - bcum), dk, nh)
    b_lasts = _heads(b_last, dk, nh)
    vs = _heads(v_ref[...], dv, nh)
    attns = [jnp.where(tril, _dot_nt(qd, kd), 0.0) for qd, kd in zip(q_decs, k_decs)]
    o_intras = [_dot(a, v) for a, v in zip(attns, vs)]
    sts = [st_ref[h] for h in range(nh)]
    for c in range(N_CHUNKS):
        sl = slice(c * CHUNK, (c + 1) * CHUNK)
        outs = [oi[sl] + _dot_nt(qd[sl], st) for oi, qd, st in zip(o_intras, q_decs, sts)]
        kvs = [_dot_tn(v[sl], kt[sl]) for v, kt in zip(vs, k_tails)]
        sts = [st * jnp.exp(bl[c * CHUNK:c * CHUNK + 1, :]) + kv for st, bl, kv in zip(sts, b_lasts, kvs)]
        for h, o in enumerate(outs):
            cols = slice(h * dv, (h + 1) * dv)
            o = o * lax.rsqrt(jnp.mean(o * o, axis=-1, keepdims=True) + NORM_EPS) * ng_ref[...]
            o_ref[sl, cols] = (o * _silu(g_ref[sl, cols])).astype(o_ref.dtype)
    for h in range(nh):
        st_ref[h] = sts[h]


def _gla(proj, small, batch, seq, layer, w_decay, b_decay, norm_g):
    t = proj.shape[0]
    nsb = seq // SEQ_BLOCK
    wq = GLA_HEADS * GLA_DK
    row = lambda b, s: b * nsb + s
    depth = w_decay.shape[0]
    wd = jnp.zeros((depth, LANES, wq), F32).at[:, LANE_DECAY:LANE_DECAY + GLA_RANK].set(w_decay)
    return pl.pallas_call(
        _gla_kernel,
        grid=(batch, nsb),
        in_specs=[pl.BlockSpec((SEQ_BLOCK, wq), lambda b, s: (row(b, s), OFF_BQ // wq)),
                  pl.BlockSpec((SEQ_BLOCK, wq), lambda b, s: (row(b, s), OFF_BK // wq)),
                  pl.BlockSpec((SEQ_BLOCK, BW), lambda b, s: (row(b, s), OFF_BV // BW)),
                  pl.BlockSpec((SEQ_BLOCK, BW), lambda b, s: (row(b, s), OFF_BG // BW)),
                  pl.BlockSpec((SEQ_BLOCK, LANES), lambda b, s: (row(b, s), 0)),
                  pl.BlockSpec((None, LANES, wq), lambda b, s: (layer, 0, 0)),
                  pl.BlockSpec((None, 1, wq), lambda b, s: (layer, 0, 0)),
                  pl.BlockSpec((None, 1, GLA_DV), lambda b, s: (layer, 0, 0))],
        out_specs=pl.BlockSpec((SEQ_BLOCK, BW), lambda b, s: (row(b, s), 0)),
        out_shape=jax.ShapeDtypeStruct((t, BW), BF16),
        scratch_shapes=[pltpu.VMEM((GLA_HEADS, GLA_DV, GLA_DK), F32)],
        compiler_params=_cparams(("parallel", "arbitrary")),
        name="gla",
    )(proj, proj, proj, proj, small, wd, b_decay.reshape(depth, 1, wq), norm_g.reshape(depth, 1, GLA_DV))


def _unit_lower_inverse(lows, ri, ci):
    eye = (ri == ci).astype(F32)
    same16 = (ri // 16) == (ci // 16)
    same32 = (ri // 32) == (ci // 32)
    d1 = [jnp.where(same16, low, 0.0) for low in lows]
    d2 = [_dot(a, a) for a in d1]
    m = [eye - a for a in d1]
    d4 = [_dot(a, a) for a in d2]
    m = [a + _dot(a, b) for a, b in zip(m, d2)]
    d8 = [_dot(a, a) for a in d4]
    m = [a + _dot(a, b) for a, b in zip(m, d4)]
    m = [a + _dot(a, b) for a, b in zip(m, d8)]
    c1 = [jnp.where(same32 & ~same16, low, 0.0) for low in lows]
    t = [_dot(a, b) for a, b in zip(m, c1)]
    m = [a - _dot(b, a) for a, b in zip(m, t)]
    c2 = [jnp.where(same32, 0.0, low) for low in lows]
    t = [_dot(a, b) for a, b in zip(m, c2)]
    m = [a - _dot(b, a) for a, b in zip(m, t)]
    return m


def _gdn_kernel(q_ref, k_ref, v_ref, g_ref, sm_ref, cwq_ref, cwk_ref, cwv_ref, alog_ref, dtb_ref,
                ng_ref, o_ref, st_ref, tq_ref, tk_ref, tv_ref):
    @pl.when(pl.program_id(1) == 0)
    def _():
        st_ref[...] = jnp.zeros_like(st_ref)
        tq_ref[...] = jnp.zeros_like(tq_ref)
        tk_ref[...] = jnp.zeros_like(tk_ref)
        tv_ref[...] = jnp.zeros_like(tv_ref)

    n = SEQ_BLOCK
    nh, hd = GDN_HEADS, GDN_HEAD_DIM
    q_raw, k_raw, v_raw = q_ref[...], k_ref[...], v_ref[...]
    qs = _heads(_silu(_causal_conv(q_raw, tq_ref[...], cwq_ref[...])), hd, nh)
    ks = _heads(_silu(_causal_conv(k_raw, tk_ref[...], cwk_ref[...])), hd, nh)
    vs = _heads(_silu(_causal_conv(v_raw, tv_ref[...], cwv_ref[...])), hd, nh)
    tq_ref[...] = q_raw[n - SUBLANES:, :]
    tk_ref[...] = k_raw[n - SUBLANES:, :]
    tv_ref[...] = v_raw[n - SUBLANES:, :]
    sm = sm_ref[...]
    beta_lanes = _sigmoid(sm)
    gcum_lanes = _chunk_cumsum(-jnp.exp(alog_ref[...]) * _softplus(sm + dtb_ref[...]))
    ri, ci, tril, strict = _block_masks(n)

    qs = [q * lax.rsqrt(jnp.sum(q * q, axis=-1, keepdims=True) + NORM_EPS) * (hd ** -0.5) for q in qs]
    ks = [k * lax.rsqrt(jnp.sum(k * k, axis=-1, keepdims=True) + NORM_EPS) for k in ks]
    betas = [beta_lanes[:, LANE_BETA + h:LANE_BETA + h + 1] for h in range(nh)]
    gcums = [gcum_lanes[:, LANE_ALPHA + h:LANE_ALPHA + h + 1] for h in range(nh)]
    g_lasts = [_chunk_last(g) for g in gcums]
    gbs = [jnp.broadcast_to(g, (n, n)) for g in gcums]
    decays = [jnp.where(tril, jnp.exp(jnp.where(tril, gb - gb.T, 0.0)), 0.0) for gb in gbs]
    kks = [_dot_nt(k, k) for k in ks]
    lows = [jnp.where(strict, b * kk * d, 0.0) for b, kk, d in zip(betas, kks, decays)]
    qks = [_dot_nt(q, k) * d for q, k, d in zip(qs, ks, decays)]
    minvs = _unit_lower_inverse(lows, ri, ci)
    e_gs = [jnp.exp(g) for g in gcums]
    uws = [_dot(mi, jnp.concatenate([v * b, k * (b * e)], axis=1))
           for mi, v, k, b, e in zip(minvs, vs, ks, betas, e_gs)]
    ps = [_dot(qk, uw) for qk, uw in zip(qks, uws)]
    q_effs = [q * e - p[:, hd:] for q, e, p in zip(qs, e_gs, ps)]
    k_tails = [k * jnp.exp(gl - g) for k, gl, g in zip(ks, g_lasts, gcums)]
    sts = [st_ref[h] for h in range(nh)]
    for c in range(N_CHUNKS):
        sl = slice(c * CHUNK, (c + 1) * CHUNK)
        outs = [p[sl, :hd] + _dot(qe[sl], st) for p, qe, st in zip(ps, q_effs, sts)]
        abs_ = [_dot_tn(kt[sl], uw[sl]) for kt, uw in zip(k_tails, uws)]
        sts = [st * jnp.exp(gl[c * CHUNK:c * CHUNK + 1, :]) + ab[:, :hd] - _dot(ab[:, hd:], st)
               for gl, st, ab in zip(g_lasts, sts, abs_)]
        for h, o in enumerate(outs):
            cols = slice(h * hd, (h + 1) * hd)
            o = o * lax.rsqrt(jnp.mean(o * o, axis=-1, keepdims=True) + NORM_EPS) * ng_ref[...]
            o_ref[sl, cols] = (o * _silu(g_ref[sl, cols])).astype(o_ref.dtype)
    for h in range(nh):
        st_ref[h] = sts[h]


def _gdn(proj, small, batch, seq, layer, conv_w, a_log, dt_bias, norm_g):
    t = proj.shape[0]
    nsb = seq // SEQ_BLOCK
    hd = GDN_HEAD_DIM
    row = lambda b, s: b * nsb + s
    depth = a_log.shape[0]
    lane_row = lambda a: jnp.zeros((depth, 1, LANES), F32).at[:, 0, LANE_ALPHA:LANE_ALPHA + GDN_HEADS].set(a)
    lspec = pl.BlockSpec((None, 1, LANES), lambda b, s: (layer, 0, 0))
    return pl.pallas_call(
        _gdn_kernel,
        grid=(batch, nsb),
        in_specs=[pl.BlockSpec((SEQ_BLOCK, BW), lambda b, s: (row(b, s), OFF_CQ // BW)),
                  pl.BlockSpec((SEQ_BLOCK, BW), lambda b, s: (row(b, s), OFF_CK // BW)),
                  pl.BlockSpec((SEQ_BLOCK, BW), lambda b, s: (row(b, s), OFF_CV // BW)),
                  pl.BlockSpec((SEQ_BLOCK, BW), lambda b, s: (row(b, s), OFF_CG // BW)),
                  pl.BlockSpec((SEQ_BLOCK, LANES), lambda b, s: (row(b, s), 0)),
                  pl.BlockSpec((None, CONV_WIDTH, BW), lambda b, s: (layer, 0, 0)),
                  pl.BlockSpec((None, CONV_WIDTH, BW), lambda b, s: (layer, 0, 1)),
                  pl.BlockSpec((None, CONV_WIDTH, BW), lambda b, s: (layer, 0, 2)),
                  lspec, lspec,
                  pl.BlockSpec((None, 1, hd), lambda b, s: (layer, 0, 0))],
        out_specs=pl.BlockSpec((SEQ_BLOCK, BW), lambda b, s: (row(b, s), 0)),
        out_shape=jax.ShapeDtypeStruct((t, BW), BF16),
        scratch_shapes=[pltpu.VMEM((GDN_HEADS, hd, hd), F32)] + [pltpu.VMEM((SUBLANES, BW), F32)] * 3,
        compiler_params=_cparams(("parallel", "arbitrary")),
        name="gdn",
    )(proj, proj, proj, proj, small, conv_w, conv_w, conv_w, lane_row(a_log), lane_row(dt_bias),
      norm_g.reshape(depth, 1, hd))


def _mlstm_kernel(q_ref, k_ref, v_ref, og_ref, sm_ref, bi_ref, bf_ref, ng_ref, o_ref,
                  c_ref, n_ref, m_ref):
    @pl.when(pl.program_id(1) == 0)
    def _():
        c_ref[...] = jnp.zeros_like(c_ref)
        n_ref[...] = jnp.zeros_like(n_ref)
        m_ref[...] = jnp.zeros_like(m_ref)

    n = SEQ_BLOCK
    nh, dqk, dv = MLSTM_HEADS, MLSTM_DQK, MLSTM_DV
    hr = range(nh)
    _, _, tril, _ = _block_masks(n)
    qs = _heads(q_ref[...], dqk, nh)
    ks = _heads(k_ref[...] * (MLSTM_DQK ** -0.5), dqk, nh)
    vs = _heads(v_ref[...], dv, nh)
    sm = sm_ref[...]
    ig_lanes = sm + bi_ref[...]
    bcum_lanes = _chunk_cumsum(_log_sigmoid(sm + bf_ref[...]))
    igs = [ig_lanes[:, LANE_I + h:LANE_I + h + 1] for h in hr]
    bcums = [bcum_lanes[:, LANE_F + h:LANE_F + h + 1] for h in hr]
    b_lasts = [_chunk_last(b) for b in bcums]
    logds = [jnp.where(tril, b + _row_form(i - b), -jnp.inf) for b, i in zip(bcums, igs)]
    m_intras = [jnp.max(ld, axis=1, keepdims=True) for ld in logds]
    tails = [bl - b + i for bl, b, i in zip(b_lasts, bcums, igs)]
    spread = lambda xs: jnp.concatenate([jnp.broadcast_to(x, (CHUNK, 1)) for x in xs], axis=0)
    m_prev_rows, a_max_rows, s_olds, s_news = [], [], [], []
    for h in hr:
        m_st = m_ref[h, 0:1, 0:1]
        m_prev, a_max, s_old, s_new = [], [], [], []
        for c in range(N_CHUNKS):
            a_c = jnp.max(tails[h][c * CHUNK:(c + 1) * CHUNK], axis=0, keepdims=True)
            bl_c = bcums[h][(c + 1) * CHUNK - 1:(c + 1) * CHUNK, :]
            m_new = jnp.maximum(bl_c + m_st, a_c)
            m_prev.append(m_st)
            a_max.append(a_c)
            s_old.append(jnp.exp(bl_c + m_st - m_new))
            s_new.append(jnp.exp(a_c - m_new))
            m_st = m_new
        m_ref[h] = jnp.broadcast_to(m_st, m_ref.shape[1:])
        m_prev_rows.append(spread(m_prev))
        a_max_rows.append(spread(a_max))
        s_olds.append(s_old)
        s_news.append(s_new)
    k_ws = [k * jnp.exp(t - a) for k, t, a in zip(ks, tails, a_max_rows)]
    m_inters = [b + mp for b, mp in zip(bcums, m_prev_rows)]
    m_is = [jnp.maximum(mi, mx) for mi, mx in zip(m_inters, m_intras)]
    s_inters = [jnp.exp(mi - m) for mi, m in zip(m_inters, m_is)]
    scs = [_dot_nt(q, k) * jnp.exp(ld - m) for q, k, ld, m in zip(qs, ks, logds, m_is)]
    num_intras = [_dot(sc, v) for sc, v in zip(scs, vs)]
    den_intras = [jnp.sum(sc, axis=1, keepdims=True) for sc in scs]
    floors = [jnp.exp(-m) for m in m_is]
    c_sts = [c_ref[h] for h in hr]
    n_sts = [n_ref[h, 0:1, :] for h in hr]
    for c in range(N_CHUNKS):
        sl = slice(c * CHUNK, (c + 1) * CHUNK)
        nums = [ni[sl] + si[sl] * _dot(q[sl], cs) for ni, si, q, cs in zip(num_intras, s_inters, qs, c_sts)]
        dens = [di[sl] + si[sl] * jnp.sum(q[sl] * ns, axis=1, keepdims=True)
                for di, si, q, ns in zip(den_intras, s_inters, qs, n_sts)]
        kvs = [_dot_tn(kw[sl], v[sl]) for kw, v in zip(k_ws, vs)]
        c_sts = [cs * so[c] + kv * sn[c] for cs, so, kv, sn in zip(c_sts, s_olds, kvs, s_news)]
        n_sts = [ns * so[c] + jnp.sum(kw[sl], axis=0, keepdims=True) * sn[c]
                 for ns, so, kw, sn in zip(n_sts, s_olds, k_ws, s_news)]
        for h in hr:
            cols = slice(h * dv, (h + 1) * dv)
            hh = nums[h] / jnp.maximum(jnp.abs(dens[h]), floors[h][sl])
            hh = hh * lax.rsqrt(jnp.mean(hh * hh, axis=-1, keepdims=True) + NORM_EPS) * ng_ref[...]
            o_ref[sl, cols] = (hh * _sigmoid(og_ref[sl, cols])).astype(o_ref.dtype)
    for h in hr:
        c_ref[h] = c_sts[h]
        n_ref[h] = jnp.broadcast_to(n_sts[h], n_ref.shape[1:])


def _mlstm(proj, small, batch, seq, layer, b_i, b_f, norm_g):
    t = proj.shape[0]
    nsb = seq // SEQ_BLOCK
    nh, dqk, dv = MLSTM_HEADS, MLSTM_DQK, MLSTM_DV
    wq = nh * dqk
    row = lambda b, s: b * nsb + s
    depth = b_i.shape[0]
    lane_row = lambda a, off: jnp.zeros((depth, 1, LANES), F32).at[:, 0, off:off + nh].set(a)
    lspec = pl.BlockSpec((None, 1, LANES), lambda b, s: (layer, 0, 0))
    return pl.pallas_call(
        _mlstm_kernel,
        grid=(batch, nsb),
        in_specs=[pl.BlockSpec((SEQ_BLOCK, wq), lambda b, s: (row(b, s), OFF_DQ // wq)),
                  pl.BlockSpec((SEQ_BLOCK, wq), lambda b, s: (row(b, s), OFF_DK // wq)),
                  pl.BlockSpec((SEQ_BLOCK, BW), lambda b, s: (row(b, s), OFF_DV // BW)),
                  pl.BlockSpec((SEQ_BLOCK, BW), lambda b, s: (row(b, s), OFF_DO // BW)),
                  pl.BlockSpec((SEQ_BLOCK, LANES), lambda b, s: (row(b, s), 0)),
                  lspec, lspec,
                  pl.BlockSpec((None, 1, dv), lambda b, s: (layer, 0, 0))],
        out_specs=pl.BlockSpec((SEQ_BLOCK, BW), lambda b, s: (row(b, s), 0)),
        out_shape=jax.ShapeDtypeStruct((t, BW), BF16),
        scratch_shapes=[pltpu.VMEM((nh, dqk, dv), F32), pltpu.VMEM((nh, SUBLANES, dqk), F32),
                        pltpu.VMEM((nh, SUBLANES, LANES), F32)],
        compiler_params=_cparams(("parallel", "arbitrary")),
        name="mlstm",
    )(proj, proj, proj, proj, small, lane_row(b_i, LANE_I), lane_row(b_f, LANE_F),
      norm_g.reshape(depth, 1, dv))


def _merge_kernel(xn_ref, ya_ref, yb_ref, yc_ref, yd_ref, wg_ref, bg_ref, wb_ref, o_ref):
    xn = xn_ref[...]
    acc = None
    for n, y_ref in enumerate((ya_ref, yb_ref, yc_ref, yd_ref)):
        gate = _sigmoid(_dot(xn, wg_ref[n]) + bg_ref[n])
        term = gate * _dot(y_ref[...], wb_ref[n])
        acc = term if acc is None else acc + term
    o_ref[...] = acc.astype(o_ref.dtype)


def _merge(xn, ys, layer, wg, bg, wb, tm=512, tn=256):
    t, d = xn.shape
    tm = min(tm, t)
    depth, nb = wg.shape[:2]
    yspec = pl.BlockSpec((tm, BW), lambda i, j: (i, 0))
    return pl.pallas_call(
        _merge_kernel,
        grid=(t // tm, d // tn),
        in_specs=[pl.BlockSpec((tm, d), lambda i, j: (i, 0)), yspec, yspec, yspec, yspec,
                  pl.BlockSpec((None, nb, d, tn), lambda i, j: (layer, 0, 0, j)),
                  pl.BlockSpec((None, nb, 1, tn), lambda i, j: (layer, 0, 0, j)),
                  pl.BlockSpec((None, nb, BW, tn), lambda i, j: (layer, 0, 0, j))],
        out_specs=pl.BlockSpec((tm, tn), lambda i, j: (i, j)),
        out_shape=jax.ShapeDtypeStruct((t, d), BF16),
        compiler_params=_cparams(("parallel", "parallel")),
        name="gated_merge",
    )(xn, *ys, wg, bg.reshape(depth, nb, 1, d), wb)


def _router_kernel(x_ref, g_ref, wrh_ref, wrl_ref, br_ref, h_ref, info_ref, idx_ref, cnt_ref, carry_ref):
    @pl.when(pl.program_id(0) == 0)
    def _():
        carry_ref[...] = jnp.zeros_like(carry_ref)

    x = x_ref[...]
    tm = x.shape[0]
    h = x * lax.rsqrt(jnp.mean(x * x, axis=-1, keepdims=True) + NORM_EPS) * g_ref[...]
    h_ref[...] = _pack_pairs(h)
    h_hi = h.astype(BF16)
    h_lo = (h - h_hi.astype(F32)).astype(BF16)
    logits = (_dot(h_hi, wrh_ref[...]) + (_dot(h_lo, wrh_ref[...]) + _dot(h_hi, wrl_ref[...]))
              + br_ref[...])
    lane = _iota((tm, LANES), 1).astype(F32)
    big = float(LANES)
    gl = jnp.where(lane < N_GROUPS, logits, -jnp.inf)
    gmax = jnp.max(gl, axis=1, keepdims=True)
    grp = jnp.min(jnp.where(gl == gmax, lane, big), axis=1, keepdims=True)
    p_grp = 1.0 / jnp.sum(jnp.exp(gl - gmax), axis=1, keepdims=True)
    lo = N_GROUPS + grp * EXPERTS_PER_GROUP
    el = jnp.where((lane >= lo) & (lane < lo + EXPERTS_PER_GROUP), logits, -jnp.inf)
    v0 = jnp.max(el, axis=1, keepdims=True)
    i0 = jnp.min(jnp.where(el == v0, lane, big), axis=1, keepdims=True)
    el = jnp.where(lane == i0, -jnp.inf, el)
    v1 = jnp.max(el, axis=1, keepdims=True)
    i1 = jnp.min(jnp.where(el == v1, lane, big), axis=1, keepdims=True)
    e = jnp.exp(v1 - v0)
    w0 = p_grp / (1.0 + e)
    w1 = p_grp * e / (1.0 + e)
    e0 = i0 - N_GROUPS
    e1 = i1 - N_GROUPS
    oh0 = (lane == e0).astype(F32)
    oh1 = (lane == e1).astype(F32)
    both = oh0 + oh1
    strict = (_iota((tm, tm), 0) > _iota((tm, tm), 1)).astype(BF16)
    before = carry_ref[0:1, :] + _dot(strict, both.astype(BF16))
    r0 = jnp.sum(oh0 * before, axis=1, keepdims=True)
    r1 = jnp.sum(oh1 * before, axis=1, keepdims=True)
    total = carry_ref[0:1, :] + jnp.sum(both, axis=0, keepdims=True)
    carry_ref[...] = jnp.broadcast_to(total, carry_ref.shape)
    cnt_ref[...] = jnp.broadcast_to(total, cnt_ref.shape)
    info_ref[...] = jnp.where(lane == 0.0, w0, jnp.where(lane == 1.0, w1, 0.0))
    idx = jnp.where(lane == 0.0, e0, jnp.where(lane == 1.0, e1,
                                               jnp.where(lane == 2.0, r0, jnp.where(lane == 3.0, r1, 0.0))))
    idx_ref[...] = idx.astype(jnp.int32)


def _router(x2d, layer, g, w_grp, b_grp, w_exp, b_exp, tm=256):
    t, d = x2d.shape
    tm = min(tm, t)
    depth = g.shape[0]
    wr = (jnp.zeros((depth, d, LANES), F32).at[:, :, :N_GROUPS].set(w_grp)
          .at[:, :, N_GROUPS:N_GROUPS + N_EXPERTS].set(w_exp))
    br = (jnp.zeros((depth, 1, LANES), F32).at[:, 0, :N_GROUPS].set(b_grp)
          .at[:, 0, N_GROUPS:N_GROUPS + N_EXPERTS].set(b_exp))
    wr_hi = wr.astype(BF16)
    wr_lo = (wr - wr_hi.astype(F32)).astype(BF16)
    return pl.pallas_call(
        _router_kernel,
        grid=(t // tm,),
        in_specs=[pl.BlockSpec((tm, d), lambda i: (i, 0)),
                  pl.BlockSpec((None, 1, d), lambda i: (layer, 0, 0)),
                  pl.BlockSpec((None, d, LANES), lambda i: (layer, 0, 0)),
                  pl.BlockSpec((None, d, LANES), lambda i: (layer, 0, 0)),
                  pl.BlockSpec((None, 1, LANES), lambda i: (layer, 0, 0))],
        out_specs=[pl.BlockSpec((tm, d // 2), lambda i: (i, 0)),
                   pl.BlockSpec((tm, LANES), lambda i: (i, 0)),
                   pl.BlockSpec((tm, LANES), lambda i: (i, 0)),
                   pl.BlockSpec((SUBLANES, LANES), lambda i: (0, 0))],
        out_shape=[jax.ShapeDtypeStruct((t, d // 2), jnp.uint32),
                   jax.ShapeDtypeStruct((t, LANES), F32),
                   jax.ShapeDtypeStruct((t, LANES), jnp.int32),
                   jax.ShapeDtypeStruct((SUBLANES, LANES), F32)],
        scratch_shapes=[pltpu.VMEM((SUBLANES, LANES), F32)],
        compiler_params=_cparams(("arbitrary",)),
        name="router",
    )(x2d, g.reshape(depth, 1, d), wr_hi, wr_lo, br)


def _dispatch_kernel(pos_ref, zrow_ref, h_ref, xs_ref, zero_ref, sem, zsem):
    i = pl.program_id(0)
    tm = h_ref.shape[0]
    zt = zero_ref.shape[0]

    @pl.when(i == 0)
    def _():
        zero_ref[...] = jnp.zeros_like(zero_ref)

        def zero_copy(e):
            start = pl.multiple_of(jnp.maximum(zrow_ref[e], 0), zt)
            return pltpu.make_async_copy(zero_ref, xs_ref.at[pl.ds(start, zt)], zsem)

        def zissue(e, carry):
            @pl.when(zrow_ref[e] >= 0)
            def _():
                zero_copy(e).start()
            return carry

        def zdrain(e, carry):
            @pl.when(zrow_ref[e] >= 0)
            def _():
                zero_copy(e).wait()
            return carry

        lax.fori_loop(0, zrow_ref.shape[0], zissue, 0)
        lax.fori_loop(0, zrow_ref.shape[0], zdrain, 0)

    def row_copy(r, p):
        return pltpu.make_async_copy(h_ref.at[pl.ds(r, 1)], xs_ref.at[pl.ds(p, 1)], sem)

    def issue(r, carry):
        t = i * tm + r
        for k in range(2):
            row_copy(r, pos_ref[2 * t + k]).start()
        return carry

    lax.fori_loop(0, tm, issue, 0, unroll=8)
    for k in range(2):
        pltpu.make_async_copy(h_ref, xs_ref.at[pl.ds(0, tm)], sem).wait()


def _dispatch(pos, zrow, h, n_rows, tm=ROW_TILE):
    t, d = h.shape
    tm = min(tm, t)
    return pl.pallas_call(
        _dispatch_kernel,
        grid_spec=pltpu.PrefetchScalarGridSpec(
            num_scalar_prefetch=2,
            grid=(t // tm,),
            in_specs=[pl.BlockSpec((tm, d), lambda i, pos, zrow: (i, 0))],
            out_specs=pl.BlockSpec(memory_space=pl.ANY),
            scratch_shapes=[pltpu.VMEM((EXPERT_TILE, d), h.dtype), pltpu.SemaphoreType.DMA(()),
                            pltpu.SemaphoreType.DMA(())],
        ),
        out_shape=jax.ShapeDtypeStruct((n_rows, d), h.dtype),
        compiler_params=_cparams(("arbitrary",)),
        name="dispatch",
    )(pos, zrow, h)


def _combine_kernel(keep_x, pos_ref, x_ref, info_ref, g_ref, ys_ref, *rest):
    outs, (buf_ref, sem) = rest[:-2], rest[-2:]
    i = pl.program_id(0)
    n = pl.num_programs(0)
    tm = x_ref.shape[0]
    half = x_ref.shape[1] // 2

    def row_copy(slot, r, k, p):
        return pltpu.make_async_copy(ys_ref.at[pl.ds(p, 1)], buf_ref.at[slot, k, pl.ds(r, 1)],
                                     sem.at[slot])

    def gather_tile(tile, slot):
        def issue(r, carry):
            t = tile * tm + r
            for k in range(2):
                row_copy(slot, r, k, pos_ref[2 * t + k]).start()
            return carry
        lax.fori_loop(0, tm, issue, 0, unroll=8)

    @pl.when(i == 0)
    def _():
        gather_tile(0, 0)

    @pl.when(i + 1 < n)
    def _():
        gather_tile(i + 1, (i + 1) % 2)

    slot = i % 2

    for k in range(2):
        pltpu.make_async_copy(ys_ref.at[pl.ds(0, tm)], buf_ref.at[slot, k], sem.at[slot]).wait()
    info = info_ref[...]
    w0, w1 = info[:, 0:1], info[:, 1:2]
    lo0, hi0 = _unpack_pairs(buf_ref[slot, 0])
    lo1, hi1 = _unpack_pairs(buf_ref[slot, 1])
    x = x_ref[...]
    x_lo = x[:, :half] + w0 * lo0 + w1 * lo1
    x_hi = x[:, half:] + w0 * hi0 + w1 * hi1
    ms = (jnp.sum(x_lo * x_lo, axis=-1, keepdims=True)
          + jnp.sum(x_hi * x_hi, axis=-1, keepdims=True)) / x.shape[1]
    inv = lax.rsqrt(ms + NORM_EPS)
    g = g_ref[...]
    if keep_x:
        outs[0][:, :half] = x_lo
        outs[0][:, half:] = x_hi
    outs[-1][:, :half] = (x_lo * inv * g[:, :half]).astype(outs[-1].dtype)
    outs[-1][:, half:] = (x_hi * inv * g[:, half:]).astype(outs[-1].dtype)


def _combine(pos, x2d, info, ys, g, keep_x, norm_dtype, tm=ROW_TILE):
    t, d = x2d.shape
    tm = min(tm, t)
    row_spec = pl.BlockSpec((tm, d), lambda i, pos: (i, 0))
    out_specs = [row_spec] * (2 if keep_x else 1)
    out_shape = ([jax.ShapeDtypeStruct((t, d), F32)] if keep_x else []) + [jax.ShapeDtypeStruct((t, d), norm_dtype)]
    return pl.pallas_call(
        functools.partial(_combine_kernel, keep_x),
        grid_spec=pltpu.PrefetchScalarGridSpec(
            num_scalar_prefetch=1,
            grid=(t // tm,),
            in_specs=[row_spec,
                      pl.BlockSpec((tm, LANES), lambda i, pos: (i, 0)),
                      pl.BlockSpec((1, d), lambda i, pos: (0, 0)),
                      pl.BlockSpec(memory_space=pl.ANY)],
            out_specs=out_specs,
            scratch_shapes=[pltpu.VMEM((2, 2, tm, d // 2), ys.dtype),
                            pltpu.SemaphoreType.DMA((2,))],
        ),
        out_shape=out_shape,
        compiler_params=_cparams(("arbitrary",)),
        name="combine",
    )(pos, x2d, info, g.reshape(1, d), ys)


def _new_expert(te_ref, i):
    return (i == 0) | (te_ref[i] != te_ref[jnp.maximum(i - 1, 0)])


def _expert_kernel(te_ref, nu_ref, nx_ref, xs_ref, wg_hbm, wu_hbm, wd_hbm, ys_ref,
                   fg_ref, fu_ref, fd_ref, bg_ref, bu_ref, bd_ref, sem):
    i = pl.program_id(0)

    def fetch(e):
        return (pltpu.make_async_copy(wg_hbm.at[e], fg_ref, sem.at[0]),
                pltpu.make_async_copy(wu_hbm.at[e], fu_ref, sem.at[1]),
                pltpu.make_async_copy(wd_hbm.at[e], fd_ref, sem.at[2]))

    @pl.when(i == 0)
    def _():
        for copy in fetch(te_ref[0]):
            copy.start()

    @pl.when(_new_expert(te_ref, i))
    def _():
        for copy in fetch(te_ref[i]):
            copy.wait()
        bg_ref[...] = fg_ref[...].astype(BF16)
        bu_ref[...] = fu_ref[...].astype(BF16)
        bd_ref[...] = fd_ref[...].astype(BF16)

        @pl.when(nx_ref[i] >= 0)
        def _():
            for copy in fetch(nx_ref[i]):
                copy.start()

    @pl.when(i < nu_ref[0])
    def _():
        lo, hi = _unpack_pairs(xs_ref[...])
        x = jnp.concatenate([lo.astype(BF16), hi.astype(BF16)], axis=1)
        a = _dot(x, bg_ref[...])
        u = _dot(x, bu_ref[...])
        act = (_silu(a) * u).astype(BF16)
        ys_ref[...] = _pack_pairs(_dot(act, bd_ref[...]))

    @pl.when(i >= nu_ref[0])
    def _():
        ys_ref[...] = jnp.zeros_like(ys_ref)


def _experts(tile_expert, n_used, next_expert, xs, wg, wu, wd):
    p, dh = xs.shape
    d, f = wg.shape[1:]
    tm = EXPERT_TILE
    src_tile = lambda i, te, nu, nx: (jnp.maximum(jnp.minimum(i, nu[0] - 1), 0), 0)
    hbm = pl.BlockSpec(memory_space=pl.ANY)
    return pl.pallas_call(
        _expert_kernel,
        grid_spec=pltpu.PrefetchScalarGridSpec(
            num_scalar_prefetch=3,
            grid=(p // tm,),
            in_specs=[pl.BlockSpec((tm, dh), src_tile), hbm, hbm, hbm],
            out_specs=pl.BlockSpec((tm, dh), lambda i, te, nu, nx: (i, 0)),
            scratch_shapes=[pltpu.VMEM((d, f), F32), pltpu.VMEM((d, f), F32), pltpu.VMEM((f, d), F32),
                            pltpu.VMEM((d, f), BF16), pltpu.VMEM((d, f), BF16), pltpu.VMEM((f, d), BF16),
                            pltpu.SemaphoreType.DMA((3,))],
        ),
        out_shape=jax.ShapeDtypeStruct((p, dh), xs.dtype),
        compiler_params=_cparams(("arbitrary",)),
        name="experts",
    )(tile_expert, n_used, next_expert, xs, wg, wu, wd)


def _hier_moe(x2d, layer, ffn_g, w_grp, b_grp, w_exp, b_exp, wg, wu, wd, next_g, keep_x, norm_dtype):
    t, d = x2d.shape
    tm = EXPERT_TILE
    n_rows = 2 * t + N_EXPERTS * tm
    h, info, idx, cnt = _router(x2d, layer, ffn_g, w_grp, b_grp, w_exp, b_exp)
    counts = cnt[0, :N_EXPERTS].astype(jnp.int32)
    padded = ((counts + tm - 1) // tm) * tm
    ends = jnp.cumsum(padded)
    offs = ends - padded
    pos = (offs[idx[:, 0:2]] + idx[:, 2:4]).reshape(2 * t)
    n_used = (ends[-1] // tm).reshape(1).astype(jnp.int32)
    tail = n_used[0] + jnp.arange(N_EXPERTS, dtype=jnp.int32)
    zrow = jnp.concatenate([jnp.where(padded > 0, ends - tm, -1),
                            jnp.where(tail < n_rows // tm, tail * tm, -1)]).astype(jnp.int32)
    tiles = jnp.minimum(jnp.arange(n_rows // tm, dtype=jnp.int32), n_used[0] - 1)
    tile_expert = jnp.sum(((ends // tm)[None, :] <= tiles[:, None]).astype(jnp.int32), axis=1)
    tile_expert = jnp.minimum(tile_expert, N_EXPERTS - 1)
    ids = jnp.arange(N_EXPERTS, dtype=jnp.int32)
    later = (ids[None, :] > ids[:, None]) & (padded[None, :] > 0)
    following = jnp.min(jnp.where(later, ids[None, :], N_EXPERTS), axis=1)
    following = jnp.where(following < N_EXPERTS, following + layer * N_EXPERTS, -1).astype(jnp.int32)
    next_expert = following[tile_expert]
    tile_expert = tile_expert + layer * N_EXPERTS
    xs = _dispatch(pos, zrow, h, n_rows)
    ys = _experts(tile_expert, n_used, next_expert, xs, wg, wu, wd)
    return _combine(pos, x2d, info, ys, next_g, keep_x, norm_dtype)


def _repack_kernel(w_ref, big_ref, small_ref):
    c0 = 5 * BW
    c1 = c0 + GLA_RANK
    c2 = c1 + 4 * BW
    c3 = c2 + 2 * GDN_HEADS
    c4 = c3 + 3 * BW
    c5 = c4 + 2 * MLSTM_HEADS
    step = 512
    for dst, src, n in ((0, 0, c0), (c0, c1, c2 - c1), (c0 + c2 - c1, c3, c4 - c3)):
        for o in range(0, n, step):
            big_ref[:, dst + o:dst + o + step] = w_ref[src + o:src + o + step, :].T.astype(BF16)
    n_small = GLA_RANK + 2 * GDN_HEADS + 2 * MLSTM_HEADS
    small = jnp.concatenate([w_ref[c0:c1, :], w_ref[c2:c3, :], w_ref[c4:c5, :],
                             jnp.zeros((LANES - n_small, w_ref.shape[1]), F32)], axis=0)
    small_ref[...] = small.T.astype(BF16)


def _split_w_in(w, tk=256):
    depth, k, n = w.shape
    w = jnp.swapaxes(w, 1, 2)
    return pl.pallas_call(
        _repack_kernel,
        grid=(depth, k // tk),
        in_specs=[pl.BlockSpec((None, n, tk), lambda l, i: (l, 0, i))],
        out_specs=[pl.BlockSpec((None, tk, PROJ_BIG), lambda l, i: (l, i, 0)),
                   pl.BlockSpec((None, tk, LANES), lambda l, i: (l, i, 0))],
        out_shape=[jax.ShapeDtypeStruct((depth, k, PROJ_BIG), BF16),
                   jax.ShapeDtypeStruct((depth, k, LANES), BF16)],
        compiler_params=_cparams(("parallel", "parallel")),
        name="repack_w_in",
    )(w)


def kernel(x, mix_norm, w_in, lru_conv_w, lru_conv_b, lru_w_a, lru_b_a, lru_w_x, lru_b_x, lru_lambda,
           gla_w_decay, gla_b_decay, gla_norm, gdn_conv_w, gdn_a_log, gdn_dt_bias, gdn_norm,
           mlstm_b_i, mlstm_b_f, mlstm_norm, w_branch, w_merge_gate, b_merge_gate, w_out,
           ffn_norm, w_group_router, b_group_router, w_expert_router, b_expert_router,
           w_exp_gate, w_exp_up, w_exp_down, final_norm):
    batch, seq, d = x.shape
    depth = w_in.shape[0]
    x2d = x.reshape(batch * seq, d)
    w_big, w_small = _split_w_in(w_in)
    wg_merge = w_merge_gate.astype(BF16)
    wb_merge = w_branch.astype(BF16)
    w_o = w_out.astype(BF16)
    f = w_exp_gate.shape[-1]
    we_gate = w_exp_gate.reshape(depth * N_EXPERTS, d, f)
    we_up = w_exp_up.reshape(depth * N_EXPERTS, d, f)
    we_down = w_exp_down.reshape(depth * N_EXPERTS, f, d)
    xn = _rmsnorm(x2d, mix_norm[0], BF16)
    for l in range(depth):
        last = l == depth - 1
        proj = _matmul(xn, w_big, l, tm=1024, tn=1024, name="in_proj")
        small = _matmul(xn, w_small, l, name="in_proj_small")
        y_a = _lru(proj, batch, seq, l, lru_conv_w, lru_conv_b, lru_w_a, lru_b_a, lru_w_x, lru_b_x,
                   lru_lambda)
        y_b = _gla(proj, small, batch, seq, l, gla_w_decay, gla_b_decay, gla_norm)
        y_c = _gdn(proj, small, batch, seq, l, gdn_conv_w, gdn_a_log, gdn_dt_bias, gdn_norm)
        y_d = _mlstm(proj, small, batch, seq, l, mlstm_b_i, mlstm_b_f, mlstm_norm)
        merged = _merge(xn, (y_a, y_b, y_c, y_d), l, wg_merge, b_merge_gate, wb_merge)
        x2d = _matmul(merged, w_o, l, residual=x2d, tm=1024, name="out_proj")
        outs = _hier_moe(x2d, l, ffn_norm, w_group_router, b_group_router, w_expert_router,
                         b_expert_router, we_gate, we_up, we_down,
                         final_norm if last else mix_norm[l + 1], not last, F32 if last else BF16)
        if last:
            (y,) = outs
        else:
            x2d, xn = outs
    return y.reshape(batch, seq, d)
```

```python
import functools

import jax
import jax.numpy as jnp
from jax import lax
from jax.experimental import pallas as pl
from jax.experimental.pallas import tpu as pltpu

F32 = jnp.float32
BF16 = jnp.bfloat16
HI = lax.Precision.HIGHEST

D_MODEL = 4096
BW = D_MODEL // 4
CONV_WIDTH = 4
CHUNK = 64
NORM_EPS = 1e-6
LRU_BLOCKS = 8
LRU_BLOCK = BW // LRU_BLOCKS
LRU_C = 8.0
GLA_HEADS = 4
GLA_DV = BW // GLA_HEADS
GLA_DK = GLA_DV // 2
GLA_RANK = 16
GLA_TAU = 16.0
GDN_HEAD_DIM = 128
GDN_HEADS = BW // GDN_HEAD_DIM
MLSTM_HEADS = 4
MLSTM_DV = BW // MLSTM_HEADS
MLSTM_DQK = MLSTM_DV // 2
N_GROUPS = 4
EXPERTS_PER_GROUP = 8
N_EXPERTS = N_GROUPS * EXPERTS_PER_GROUP
D_EXPERT = D_MODEL // 8

LANES = 128
SUBLANES = 8
VMEM_LIMIT = 56 * 1024 * 1024

PROJ_BIG = 12 * BW
OFF_AX, OFF_AG = 0, BW
OFF_BQ, OFF_BK, OFF_BV, OFF_BG = 2 * BW, 2 * BW + 512, 3 * BW, 4 * BW
OFF_CQ, OFF_CK, OFF_CV, OFF_CG = 5 * BW, 6 * BW, 7 * BW, 8 * BW
OFF_DQ, OFF_DK, OFF_DV, OFF_DO = 9 * BW, 9 * BW + 512, 10 * BW, 11 * BW
LANE_DECAY, LANE_BETA, LANE_ALPHA, LANE_I, LANE_F = 0, 16, 24, 32, 36

SEQ_BLOCK = 256
N_CHUNKS = SEQ_BLOCK // CHUNK
EXPERT_TILE = 256
ROW_TILE = 256


def _cparams(sem):
    return pltpu.CompilerParams(dimension_semantics=sem, vmem_limit_bytes=VMEM_LIMIT)


def _softplus(x):
    return jnp.maximum(x, 0.0) + jnp.log1p(jnp.exp(-jnp.abs(x)))


def _log_sigmoid(x):
    return -_softplus(-x)


def _sigmoid(x):
    return 1.0 / (1.0 + jnp.exp(-x))


def _silu(x):
    return x * _sigmoid(x)


def _gelu_tanh(x):
    return 0.5 * x * (1.0 + jnp.tanh(0.7978845608028654 * (x + 0.044715 * (x * x * x))))


def _dot(a, b, precision=None):
    return jnp.dot(a, b, preferred_element_type=F32, precision=precision)


def _dot_nt(a, b, precision=None):
    return lax.dot_general(a, b, (((1,), (1,)), ((), ())), preferred_element_type=F32,
                           precision=precision)


def _dot_tn(a, b, precision=None):
    return lax.dot_general(a, b, (((0,), (0,)), ((), ())), preferred_element_type=F32,
                           precision=precision)


def _iota(shape, dim):
    return lax.broadcasted_iota(jnp.int32, shape, dim)


def _shift_rows(x, tail, s):
    r = pltpu.roll(x, s, 0)
    rt = pltpu.roll(tail, s, 0)
    head = jnp.where(_iota(tail.shape, 0) < s, rt, r[0:SUBLANES])
    return jnp.concatenate([head, r[SUBLANES:]], axis=0)


def _causal_conv(x, tail, w):
    y = x * w[CONV_WIDTH - 1:CONV_WIDTH, :]
    for s in range(1, CONV_WIDTH):
        y = y + _shift_rows(x, tail, s) * w[CONV_WIDTH - 1 - s:CONV_WIDTH - s, :]
    return y


def _chunk_cumsum(x):
    pos = _iota(x.shape, 0) & (CHUNK - 1)
    s = 1
    while s < CHUNK:
        x = x + jnp.where(pos >= s, pltpu.roll(x, s, 0), 0.0)
        s *= 2
    return x


def _chunk_last(x):
    return jnp.concatenate(
        [jnp.broadcast_to(x[(c + 1) * CHUNK - 1:(c + 1) * CHUNK], (CHUNK, x.shape[1]))
         for c in range(x.shape[0] // CHUNK)], axis=0)


def _row_form(col):
    n = col.shape[0]
    return jnp.broadcast_to(col, (n, n)).T


def _block_masks(n):
    ri = _iota((n, n), 0)
    ci = _iota((n, n), 1)
    same = (ri // CHUNK) == (ci // CHUNK)
    return ri, ci, same & (ri >= ci), same & (ri > ci)


def _pack_pairs(x):
    c = x.shape[1] // 2
    bits = lambda v: lax.bitcast_convert_type(v.astype(BF16).astype(F32), jnp.uint32)
    return (bits(x[:, :c]) >> 16) | bits(x[:, c:])


def _unpack_pairs(p):
    lo = lax.bitcast_convert_type(p << 16, F32)
    hi = lax.bitcast_convert_type(p & jnp.uint32(0xFFFF0000), F32)
    return lo, hi


def _heads(x, width, count):
    return [x[:, h * width:(h + 1) * width] for h in range(count)]


def _rmsnorm_kernel(x_ref, g_ref, o_ref):
    x = x_ref[...]
    ms = jnp.mean(x * x, axis=-1, keepdims=True)
    o_ref[...] = (x * lax.rsqrt(ms + NORM_EPS) * g_ref[...]).astype(o_ref.dtype)


def _rmsnorm(x2d, g, out_dtype, tm=512):
    t, d = x2d.shape
    return pl.pallas_call(
        _rmsnorm_kernel,
        grid=(t // tm,),
        in_specs=[pl.BlockSpec((tm, d), lambda i: (i, 0)),
                  pl.BlockSpec((1, d), lambda i: (0, 0))],
        out_specs=pl.BlockSpec((tm, d), lambda i: (i, 0)),
        out_shape=jax.ShapeDtypeStruct((t, d), out_dtype),
        compiler_params=_cparams(("parallel",)),
        name="rmsnorm",
    )(x2d, g.reshape(1, d))


def _mm_kernel(a_ref, w_ref, o_ref):
    o_ref[...] = _dot(a_ref[...], w_ref[...]).astype(o_ref.dtype)


def _mm_res_kernel(a_ref, w_ref, r_ref, o_ref):
    o_ref[...] = r_ref[...] + _dot(a_ref[...], w_ref[...])


def _matmul(a, w, layer, out_dtype=F32, residual=None, tm=512, tn=512, name="matmul"):
    m, k = a.shape
    n = w.shape[2]
    tm, tn = min(tm, m), min(tn, n)
    in_specs = [pl.BlockSpec((tm, k), lambda i, j: (i, 0)),
                pl.BlockSpec((None, k, tn), lambda i, j: (layer, 0, j))]
    args = [a, w]
    body = _mm_kernel
    if residual is not None:
        in_specs.append(pl.BlockSpec((tm, tn), lambda i, j: (i, j)))
        args.append(residual)
        body = _mm_res_kernel
    return pl.pallas_call(
        body,
        grid=(m // tm, n // tn),
        in_specs=in_specs,
        out_specs=pl.BlockSpec((tm, tn), lambda i, j: (i, j)),
        out_shape=jax.ShapeDtypeStruct((m, n), out_dtype),
        compiler_params=_cparams(("parallel", "parallel")),
        name=name,
    )(*args)


def _lru_kernel(x_ref, gate_ref, cw_ref, cb_ref, wa_ref, ba_ref, wx_ref, bx_ref, lam_ref,
                o_ref, tail_ref, h_ref):
    @pl.when(pl.program_id(1) == 0)
    def _():
        tail_ref[...] = jnp.zeros_like(tail_ref)
        h_ref[...] = jnp.zeros_like(h_ref)

    x = x_ref[...]
    n = x.shape[0]
    u = _causal_conv(x, tail_ref[...], cw_ref[...]) + cb_ref[...]
    tail_ref[...] = x[n - SUBLANES:, :]
    ubs = _heads(u, LRU_BLOCK, LRU_BLOCKS)
    r = jnp.concatenate([_dot(ub, wa_ref[b]) for b, ub in enumerate(ubs)], axis=1)
    i = jnp.concatenate([_dot(ub, wx_ref[b]) for b, ub in enumerate(ubs)], axis=1)
    r = _sigmoid(r + ba_ref[...])
    i = _sigmoid(i + bx_ref[...])
    log_a = (-LRU_C * r) * _softplus(-lam_ref[...])
    a = jnp.exp(log_a)
    xin = jnp.sqrt(-jnp.tanh(log_a) * (a * a + 1.0)) * (i * u)
    pos = _iota(a.shape, 0) & (SUBLANES - 1)
    s = 1
    while s < SUBLANES:
        keep = pos >= s
        a_sh = jnp.where(keep, pltpu.roll(a, s, 0), 1.0)
        x_sh = jnp.where(keep, pltpu.roll(xin, s, 0), 0.0)
        xin = a * x_sh + xin
        a = a * a_sh
        s *= 2
    carry = h_ref[0:1, :]
    groups = []
    for g in range(n // SUBLANES):
        rows = slice(g * SUBLANES, (g + 1) * SUBLANES)
        hg = xin[rows] + a[rows] * carry
        groups.append(hg)
        carry = hg[SUBLANES - 1:SUBLANES, :]
    h = jnp.concatenate(groups, axis=0)
    h_ref[...] = jnp.broadcast_to(carry, h_ref.shape)
    o_ref[...] = (h * _gelu_tanh(gate_ref[...])).astype(o_ref.dtype)


def _lru(proj, batch, seq, layer, cw, cb, wa, ba, wx, bx, lam):
    t = proj.shape[0]
    nsb = seq // SEQ_BLOCK
    row = lambda b, s: b * nsb + s
    vec = lambda a: a.reshape(a.shape[0], 1, BW)
    vspec = pl.BlockSpec((None, 1, BW), lambda b, s: (layer, 0, 0))
    wspec = pl.BlockSpec((None, LRU_BLOCKS, LRU_BLOCK, LRU_BLOCK), lambda b, s: (layer, 0, 0, 0))
    return pl.pallas_call(
        _lru_kernel,
        grid=(batch, nsb),
        in_specs=[pl.BlockSpec((SEQ_BLOCK, BW), lambda b, s: (row(b, s), OFF_AX // BW)),
                  pl.BlockSpec((SEQ_BLOCK, BW), lambda b, s: (row(b, s), OFF_AG // BW)),
                  pl.BlockSpec((None, CONV_WIDTH, BW), lambda b, s: (layer, 0, 0)),
                  vspec, wspec, vspec, wspec, vspec, vspec],
        out_specs=pl.BlockSpec((SEQ_BLOCK, BW), lambda b, s: (row(b, s), 0)),
        out_shape=jax.ShapeDtypeStruct((t, BW), BF16),
        scratch_shapes=[pltpu.VMEM((SUBLANES, BW), F32), pltpu.VMEM((SUBLANES, BW), F32)],
        compiler_params=_cparams(("parallel", "arbitrary")),
        name="rg_lru",
    )(proj, proj, cw, vec(cb), wa, vec(ba), wx, vec(bx), vec(lam))


def _gla_kernel(q_ref, k_ref, v_ref, g_ref, sm_ref, wd_ref, bd_ref, ng_ref, o_ref, st_ref):
    @pl.when(pl.program_id(1) == 0)
    def _():
        st_ref[...] = jnp.zeros_like(st_ref)

    nh, dk, dv = GLA_HEADS, GLA_DK, GLA_DV
    _, _, tril, _ = _block_masks(SEQ_BLOCK)
    log_alpha = _log_sigmoid(_dot(sm_ref[...], wd_ref[...], HI) + bd_ref[...]) / GLA_TAU
    bcum = _chunk_cumsum(log_alpha)
    b_last = _chunk_last(bcum)
    k = k_ref[...]
    q_decs = _heads(q_ref[...] * (GLA_DK ** -0.5) * jnp.exp(bcum), dk, nh)
    k_decs = _heads(k * jnp.exp(-bcum), dk, nh)
    k_tails = _heads(k * jnp.exp(b_last - bcum), dk, nh)
    b_lasts = _heads(b_last, dk, nh)
    vs = _heads(v_ref[...], dv, nh)
    attns = [jnp.where(tril, _dot_nt(qd, kd), 0.0) for qd, kd in zip(q_decs, k_decs)]
    o_intras = [_dot(a, v) for a, v in zip(attns, vs)]
    sts = [st_ref[h] for h in range(nh)]
    for c in range(N_CHUNKS):
        sl = slice(c * CHUNK, (c + 1) * CHUNK)
        outs = [oi[sl] + _dot_nt(qd[sl], st) for oi, qd, st in zip(o_intras, q_decs, sts)]
        kvs = [_dot_tn(v[sl], kt[sl]) for v, kt in zip(vs, k_tails)]
        sts = [st * jnp.exp(bl[c * CHUNK:c * CHUNK + 1, :]) + kv for st, bl, kv in zip(sts, b_lasts, kvs)]
        for h, o in enumerate(outs):
            cols = slice(h * dv, (h + 1) * dv)
            o = o * lax.rsqrt(jnp.mean(o * o, axis=-1, keepdims=True) + NORM_EPS) * ng_ref[...]
            o_ref[sl, cols] = (o * _silu(g_ref[sl, cols])).astype(o_ref.dtype)
    for h in range(nh):
        st_ref[h] = sts[h]


def _gla(proj, small, batch, seq, layer, w_decay, b_decay, norm_g):
    t = proj.shape[0]
    nsb = seq // SEQ_BLOCK
    wq = GLA_HEADS * GLA_DK
    row = lambda b, s: b * nsb + s
    depth = w_decay.shape[0]
    wd = jnp.zeros((depth, LANES, wq), F32).at[:, LANE_DECAY:LANE_DECAY + GLA_RANK].set(w_decay)
    return pl.pallas_call(
        _gla_kernel,
        grid=(batch, nsb),
        in_specs=[pl.BlockSpec((SEQ_BLOCK, wq), lambda b, s: (row(b, s), OFF_BQ // wq)),
                  pl.BlockSpec((SEQ_BLOCK, wq), lambda b, s: (row(b, s), OFF_BK // wq)),
                  pl.BlockSpec((SEQ_BLOCK, BW), lambda b, s: (row(b, s), OFF_BV // BW)),
                  pl.BlockSpec((SEQ_BLOCK, BW), lambda b, s: (row(b, s), OFF_BG // BW)),
                  pl.BlockSpec((SEQ_BLOCK, LANES), lambda b, s: (row(b, s), 0)),
                  pl.BlockSpec((None, LANES, wq), lambda b, s: (layer, 0, 0)),
                  pl.BlockSpec((None, 1, wq), lambda b, s: (layer, 0, 0)),
                  pl.BlockSpec((None, 1, GLA_DV), lambda b, s: (layer, 0, 0))],
        out_specs=pl.BlockSpec((SEQ_BLOCK, BW), lambda b, s: (row(b, s), 0)),
        out_shape=jax.ShapeDtypeStruct((t, BW), BF16),
        scratch_shapes=[pltpu.VMEM((GLA_HEADS, GLA_DV, GLA_DK), F32)],
        compiler_params=_cparams(("parallel", "arbitrary")),
        name="gla",
    )(proj, proj, proj, proj, small, wd, b_decay.reshape(depth, 1, wq), norm_g.reshape(depth, 1, GLA_DV))


def _unit_lower_inverse(lows, ri, ci):
    eye = (ri == ci).astype(F32)
    same16 = (ri // 16) == (ci // 16)
    same32 = (ri // 32) == (ci // 32)
    d1 = [jnp.where(same16, low, 0.0) for low in lows]
    d2 = [_dot(a, a) for a in d1]
    m = [eye - a for a in d1]
    d4 = [_dot(a, a) for a in d2]
    m = [a + _dot(a, b) for a, b in zip(m, d2)]
    d8 = [_dot(a, a) for a in d4]
    m = [a + _dot(a, b) for a, b in zip(m, d4)]
    m = [a + _dot(a, b) for a, b in zip(m, d8)]
    c1 = [jnp.where(same32 & ~same16, low, 0.0) for low in lows]
    t = [_dot(a, b) for a, b in zip(m, c1)]
    m = [a - _dot(b, a) for a, b in zip(m, t)]
    c2 = [jnp.where(same32, 0.0, low) for low in lows]
    t = [_dot(a, b) for a, b in zip(m, c2)]
    m = [a - _dot(b, a) for a, b in zip(m, t)]
    return m


def _gdn_kernel(q_ref, k_ref, v_ref, g_ref, sm_ref, cwq_ref, cwk_ref, cwv_ref, alog_ref, dtb_ref,
                ng_ref, o_ref, st_ref, tq_ref, tk_ref, tv_ref):
    @pl.when(pl.program_id(1) == 0)
    def _():
        st_ref[...] = jnp.zeros_like(st_ref)
        tq_ref[...] = jnp.zeros_like(tq_ref)
        tk_ref[...] = jnp.zeros_like(tk_ref)
        tv_ref[...] = jnp.zeros_like(tv_ref)

    n = SEQ_BLOCK
    nh, hd = GDN_HEADS, GDN_HEAD_DIM
    q_raw, k_raw, v_raw = q_ref[...], k_ref[...], v_ref[...]
    qs = _heads(_silu(_causal_conv(q_raw, tq_ref[...], cwq_ref[...])), hd, nh)
    ks = _heads(_silu(_causal_conv(k_raw, tk_ref[...], cwk_ref[...])), hd, nh)
    vs = _heads(_silu(_causal_conv(v_raw, tv_ref[...], cwv_ref[...])), hd, nh)
    tq_ref[...] = q_raw[n - SUBLANES:, :]
    tk_ref[...] = k_raw[n - SUBLANES:, :]
    tv_ref[...] = v_raw[n - SUBLANES:, :]
    sm = sm_ref[...]
    beta_lanes = _sigmoid(sm)
    gcum_lanes = _chunk_cumsum(-jnp.exp(alog_ref[...]) * _softplus(sm + dtb_ref[...]))
    ri, ci, tril, strict = _block_masks(n)

    qs = [q * lax.rsqrt(jnp.sum(q * q, axis=-1, keepdims=True) + NORM_EPS) * (hd ** -0.5) for q in qs]
    ks = [k * lax.rsqrt(jnp.sum(k * k, axis=-1, keepdims=True) + NORM_EPS) for k in ks]
    betas = [beta_lanes[:, LANE_BETA + h:LANE_BETA + h + 1] for h in range(nh)]
    gcums = [gcum_lanes[:, LANE_ALPHA + h:LANE_ALPHA + h + 1] for h in range(nh)]
    g_lasts = [_chunk_last(g) for g in gcums]
    gbs = [jnp.broadcast_to(g, (n, n)) for g in gcums]
    decays = [jnp.where(tril, jnp.exp(jnp.where(tril, gb - gb.T, 0.0)), 0.0) for gb in gbs]
    kks = [_dot_nt(k, k) for k in ks]
    lows = [jnp.where(strict, b * kk * d, 0.0) for b, kk, d in zip(betas, kks, decays)]
    qks = [_dot_nt(q, k) * d for q, k, d in zip(qs, ks, decays)]
    minvs = _unit_lower_inverse(lows, ri, ci)
    e_gs = [jnp.exp(g) for g in gcums]
    uws = [_dot(mi, jnp.concatenate([v * b, k * (b * e)], axis=1))
           for mi, v, k, b, e in zip(minvs, vs, ks, betas, e_gs)]
    ps = [_dot(qk, uw) for qk, uw in zip(qks, uws)]
    q_effs = [q * e - p[:, hd:] for q, e, p in zip(qs, e_gs, ps)]
    k_tails = [k * jnp.exp(gl - g) for k, gl, g in zip(ks, g_lasts, gcums)]
    sts = [st_ref[h] for h in range(nh)]
    for c in range(N_CHUNKS):
        sl = slice(c * CHUNK, (c + 1) * CHUNK)
        outs = [p[sl, :hd] + _dot(qe[sl], st) for p, qe, st in zip(ps, q_effs, sts)]
        abs_ = [_dot_tn(kt[sl], uw[sl]) for kt, uw in zip(k_tails, uws)]
        sts = [st * jnp.exp(gl[c * CHUNK:c * CHUNK + 1, :]) + ab[:, :hd] - _dot(ab[:, hd:], st)
               for gl, st, ab in zip(g_lasts, sts, abs_)]
        for h, o in enumerate(outs):
            cols = slice(h * hd, (h + 1) * hd)
            o = o * lax.rsqrt(jnp.mean(o * o, axis=-1, keepdims=True) + NORM_EPS) * ng_ref[...]
            o_ref[sl, cols] = (o * _silu(g_ref[sl, cols])).astype(o_ref.dtype)
    for h in range(nh):
        st_ref[h] = sts[h]


def _gdn(proj, small, batch, seq, layer, conv_w, a_log, dt_bias, norm_g):
    t = proj.shape[0]
    nsb = seq // SEQ_BLOCK
    hd = GDN_HEAD_DIM
    row = lambda b, s: b * nsb + s
    depth = a_log.shape[0]
    lane_row = lambda a: jnp.zeros((depth, 1, LANES), F32).at[:, 0, LANE_ALPHA:LANE_ALPHA + GDN_HEADS].set(a)
    lspec = pl.BlockSpec((None, 1, LANES), lambda b, s: (layer, 0, 0))
    return pl.pallas_call(
        _gdn_kernel,
        grid=(batch, nsb),
        in_specs=[pl.BlockSpec((SEQ_BLOCK, BW), lambda b, s: (row(b, s), OFF_CQ // BW)),
                  pl.BlockSpec((SEQ_BLOCK, BW), lambda b, s: (row(b, s), OFF_CK // BW)),
                  pl.BlockSpec((SEQ_BLOCK, BW), lambda b, s: (row(b, s), OFF_CV // BW)),
                  pl.BlockSpec((SEQ_BLOCK, BW), lambda b, s: (row(b, s), OFF_CG // BW)),
                  pl.BlockSpec((SEQ_BLOCK, LANES), lambda b, s: (row(b, s), 0)),
                  pl.BlockSpec((None, CONV_WIDTH, BW), lambda b, s: (layer, 0, 0)),
                  pl.BlockSpec((None, CONV_WIDTH, BW), lambda b, s: (layer, 0, 1)),
                  pl.BlockSpec((None, CONV_WIDTH, BW), lambda b, s: (layer, 0, 2)),
                  lspec, lspec,
                  pl.BlockSpec((None, 1, hd), lambda b, s: (layer, 0, 0))],
        out_specs=pl.BlockSpec((SEQ_BLOCK, BW), lambda b, s: (row(b, s), 0)),
        out_shape=jax.ShapeDtypeStruct((t, BW), BF16),
        scratch_shapes=[pltpu.VMEM((GDN_HEADS, hd, hd), F32)] + [pltpu.VMEM((SUBLANES, BW), F32)] * 3,
        compiler_params=_cparams(("parallel", "arbitrary")),
        name="gdn",
    )(proj, proj, proj, proj, small, conv_w, conv_w, conv_w, lane_row(a_log), lane_row(dt_bias),
      norm_g.reshape(depth, 1, hd))


def _mlstm_kernel(q_ref, k_ref, v_ref, og_ref, sm_ref, bi_ref, bf_ref, ng_ref, o_ref,
                  c_ref, n_ref, m_ref):
    @pl.when(pl.program_id(1) == 0)
    def _():
        c_ref[...] = jnp.zeros_like(c_ref)
        n_ref[...] = jnp.zeros_like(n_ref)
        m_ref[...] = jnp.zeros_like(m_ref)

    n = SEQ_BLOCK
    nh, dqk, dv = MLSTM_HEADS, MLSTM_DQK, MLSTM_DV
    hr = range(nh)
    _, _, tril, _ = _block_masks(n)
    qs = _heads(q_ref[...], dqk, nh)
    ks = _heads(k_ref[...] * (MLSTM_DQK ** -0.5), dqk, nh)
    vs = _heads(v_ref[...], dv, nh)
    sm = sm_ref[...]
    ig_lanes = sm + bi_ref[...]
    bcum_lanes = _chunk_cumsum(_log_sigmoid(sm + bf_ref[...]))
    igs = [ig_lanes[:, LANE_I + h:LANE_I + h + 1] for h in hr]
    bcums = [bcum_lanes[:, LANE_F + h:LANE_F + h + 1] for h in hr]
    b_lasts = [_chunk_last(b) for b in bcums]
    logds = [jnp.where(tril, b + _row_form(i - b), -jnp.inf) for b, i in zip(bcums, igs)]
    m_intras = [jnp.max(ld, axis=1, keepdims=True) for ld in logds]
    tails = [bl - b + i for bl, b, i in zip(b_lasts, bcums, igs)]
    spread = lambda xs: jnp.concatenate([jnp.broadcast_to(x, (CHUNK, 1)) for x in xs], axis=0)
    m_prev_rows, a_max_rows, s_olds, s_news = [], [], [], []
    for h in hr:
        m_st = m_ref[h, 0:1, 0:1]
        m_prev, a_max, s_old, s_new = [], [], [], []
        for c in range(N_CHUNKS):
            a_c = jnp.max(tails[h][c * CHUNK:(c + 1) * CHUNK], axis=0, keepdims=True)
            bl_c = bcums[h][(c + 1) * CHUNK - 1:(c + 1) * CHUNK, :]
            m_new = jnp.maximum(bl_c + m_st, a_c)
            m_prev.append(m_st)
            a_max.append(a_c)
            s_old.append(jnp.exp(bl_c + m_st - m_new))
            s_new.append(jnp.exp(a_c - m_new))
            m_st = m_new
        m_ref[h] = jnp.broadcast_to(m_st, m_ref.shape[1:])
        m_prev_rows.append(spread(m_prev))
        a_max_rows.append(spread(a_max))
        s_olds.append(s_old)
        s_news.append(s_new)
    k_ws = [k * jnp.exp(t - a) for k, t, a in zip(ks, tails, a_max_rows)]
    m_inters = [b + mp for b, mp in zip(bcums, m_prev_rows)]
    m_is = [jnp.maximum(mi, mx) for mi, mx in zip(m_inters, m_intras)]
    s_inters = [jnp.exp(mi - m) for mi, m in zip(m_inters, m_is)]
    scs = [_dot_nt(q, k) * jnp.exp(ld - m) for q, k, ld, m in zip(qs, ks, logds, m_is)]
    num_intras = [_dot(sc, v) for sc, v in zip(scs, vs)]
    den_intras = [jnp.sum(sc, axis=1, keepdims=True) for sc in scs]
    floors = [jnp.exp(-m) for m in m_is]
    c_sts = [c_ref[h] for h in hr]
    n_sts = [n_ref[h, 0:1, :] for h in hr]
    for c in range(N_CHUNKS):
        sl = slice(c * CHUNK, (c + 1) * CHUNK)
        nums = [ni[sl] + si[sl] * _dot(q[sl], cs) for ni, si, q, cs in zip(num_intras, s_inters, qs, c_sts)]
        dens = [di[sl] + si[sl] * jnp.sum(q[sl] * ns, axis=1, keepdims=True)
                for di, si, q, ns in zip(den_intras, s_inters, qs, n_sts)]
        kvs = [_dot_tn(kw[sl], v[sl]) for kw, v in zip(k_ws, vs)]
        c_sts = [cs * so[c] + kv * sn[c] for cs, so, kv, sn in zip(c_sts, s_olds, kvs, s_news)]
        n_sts = [ns * so[c] + jnp.sum(kw[sl], axis=0, keepdims=True) * sn[c]
                 for ns, so, kw, sn in zip(n_sts, s_olds, k_ws, s_news)]
        for h in hr:
            cols = slice(h * dv, (h + 1) * dv)
            hh = nums[h] / jnp.maximum(jnp.abs(dens[h]), floors[h][sl])
            hh = hh * lax.rsqrt(jnp.mean(hh * hh, axis=-1, keepdims=True) + NORM_EPS) * ng_ref[...]
            o_ref[sl, cols] = (hh * _sigmoid(og_ref[sl, cols])).astype(o_ref.dtype)
    for h in hr:
        c_ref[h] = c_sts[h]
        n_ref[h] = jnp.broadcast_to(n_sts[h], n_ref.shape[1:])


def _mlstm(proj, small, batch, seq, layer, b_i, b_f, norm_g):
    t = proj.shape[0]
    nsb = seq // SEQ_BLOCK
    nh, dqk, dv = MLSTM_HEADS, MLSTM_DQK, MLSTM_DV
    wq = nh * dqk
    row = lambda b, s: b * nsb + s
    depth = b_i.shape[0]
    lane_row = lambda a, off: jnp.zeros((depth, 1, LANES), F32).at[:, 0, off:off + nh].set(a)
    lspec = pl.BlockSpec((None, 1, LANES), lambda b, s: (layer, 0, 0))
    return pl.pallas_call(
        _mlstm_kernel,
        grid=(batch, nsb),
        in_specs=[pl.BlockSpec((SEQ_BLOCK, wq), lambda b, s: (row(b, s), OFF_DQ // wq)),
                  pl.BlockSpec((SEQ_BLOCK, wq), lambda b, s: (row(b, s), OFF_DK // wq)),
                  pl.BlockSpec((SEQ_BLOCK, BW), lambda b, s: (row(b, s), OFF_DV // BW)),
                  pl.BlockSpec((SEQ_BLOCK, BW), lambda b, s: (row(b, s), OFF_DO // BW)),
                  pl.BlockSpec((SEQ_BLOCK, LANES), lambda b, s: (row(b, s), 0)),
                  lspec, lspec,
                  pl.BlockSpec((None, 1, dv), lambda b, s: (layer, 0, 0))],
        out_specs=pl.BlockSpec((SEQ_BLOCK, BW), lambda b, s: (row(b, s), 0)),
        out_shape=jax.ShapeDtypeStruct((t, BW), BF16),
        scratch_shapes=[pltpu.VMEM((nh, dqk, dv), F32), pltpu.VMEM((nh, SUBLANES, dqk), F32),
                        pltpu.VMEM((nh, SUBLANES, LANES), F32)],
        compiler_params=_cparams(("parallel", "arbitrary")),
        name="mlstm",
    )(proj, proj, proj, proj, small, lane_row(b_i, LANE_I), lane_row(b_f, LANE_F),
      norm_g.reshape(depth, 1, dv))


def _merge_kernel(xn_ref, ya_ref, yb_ref, yc_ref, yd_ref, wg_ref, bg_ref, wb_ref, o_ref):
    xn = xn_ref[...]
    acc = None
    for n, y_ref in enumerate((ya_ref, yb_ref, yc_ref, yd_ref)):
        gate = _sigmoid(_dot(xn, wg_ref[n]) + bg_ref[n])
        term = gate * _dot(y_ref[...], wb_ref[n])
        acc = term if acc is None else acc + term
    o_ref[...] = acc.astype(o_ref.dtype)


def _merge(xn, ys, layer, wg, bg, wb, tm=512, tn=256):
    t, d = xn.shape
    tm = min(tm, t)
    depth, nb = wg.shape[:2]
    yspec = pl.BlockSpec((tm, BW), lambda j, i: (i, 0))
    return pl.pallas_call(
        _merge_kernel,
        grid=(d // tn, t // tm),
        in_specs=[pl.BlockSpec((tm, d), lambda j, i: (i, 0)), yspec, yspec, yspec, yspec,
                  pl.BlockSpec((None, nb, d, tn), lambda j, i: (layer, 0, 0, j)),
                  pl.BlockSpec((None, nb, 1, tn), lambda j, i: (layer, 0, 0, j)),
                  pl.BlockSpec((None, nb, BW, tn), lambda j, i: (layer, 0, 0, j))],
        out_specs=pl.BlockSpec((tm, tn), lambda j, i: (i, j)),
        out_shape=jax.ShapeDtypeStruct((t, d), BF16),
        compiler_params=_cparams(("parallel", "parallel")),
        name="gated_merge",
    )(xn, *ys, wg, bg.reshape(depth, nb, 1, d), wb)


def _router_kernel(x_ref, g_ref, wrh_ref, wrl_ref, br_ref, h_ref, info_ref, idx_ref, cnt_ref, carry_ref):
    @pl.when(pl.program_id(0) == 0)
    def _():
        carry_ref[...] = jnp.zeros_like(carry_ref)

    x = x_ref[...]
    tm = x.shape[0]
    h = x * lax.rsqrt(jnp.mean(x * x, axis=-1, keepdims=True) + NORM_EPS) * g_ref[...]
    h_ref[...] = _pack_pairs(h)
    h_hi = h.astype(BF16)
    h_lo = (h - h_hi.astype(F32)).astype(BF16)
    logits = (_dot(h_hi, wrh_ref[...]) + (_dot(h_lo, wrh_ref[...]) + _dot(h_hi, wrl_ref[...]))
              + br_ref[...])
    lane = _iota((tm, LANES), 1).astype(F32)
    big = float(LANES)
    gl = jnp.where(lane < N_GROUPS, logits, -jnp.inf)
    gmax = jnp.max(gl, axis=1, keepdims=True)
    grp = jnp.min(jnp.where(gl == gmax, lane, big), axis=1, keepdims=True)
    p_grp = 1.0 / jnp.sum(jnp.exp(gl - gmax), axis=1, keepdims=True)
    lo = N_GROUPS + grp * EXPERTS_PER_GROUP
    el = jnp.where((lane >= lo) & (lane < lo + EXPERTS_PER_GROUP), logits, -jnp.inf)
    v0 = jnp.max(el, axis=1, keepdims=True)
    i0 = jnp.min(jnp.where(el == v0, lane, big), axis=1, keepdims=True)
    el = jnp.where(lane == i0, -jnp.inf, el)
    v1 = jnp.max(el, axis=1, keepdims=True)
    i1 = jnp.min(jnp.where(el == v1, lane, big), axis=1, keepdims=True)
    e = jnp.exp(v1 - v0)
    w0 = p_grp / (1.0 + e)
    w1 = p_grp * e / (1.0 + e)
    e0 = i0 - N_GROUPS
    e1 = i1 - N_GROUPS
    oh0 = (lane == e0).astype(F32)
    oh1 = (lane == e1).astype(F32)
    both = oh0 + oh1
    strict = (_iota((tm, tm), 0) > _iota((tm, tm), 1)).astype(BF16)
    before = carry_ref[0:1, :] + _dot(strict, both.astype(BF16))
    r0 = jnp.sum(oh0 * before, axis=1, keepdims=True)
    r1 = jnp.sum(oh1 * before, axis=1, keepdims=True)
    total = carry_ref[0:1, :] + jnp.sum(both, axis=0, keepdims=True)
    carry_ref[...] = jnp.broadcast_to(total, carry_ref.shape)
    cnt_ref[...] = jnp.broadcast_to(total, cnt_ref.shape)
    info_ref[...] = jnp.where(lane == 0.0, w0, jnp.where(lane == 1.0, w1, 0.0))
    idx = jnp.where(lane == 0.0, e0, jnp.where(lane == 1.0, e1,
                                               jnp.where(lane == 2.0, r0, jnp.where(lane == 3.0, r1, 0.0))))
    idx_ref[...] = idx.astype(jnp.int32)


def _router(x2d, layer, g, w_grp, b_grp, w_exp, b_exp, tm=256):
    t, d = x2d.shape
    tm = min(tm, t)
    depth = g.shape[0]
    wr = (jnp.zeros((depth, d, LANES), F32).at[:, :, :N_GROUPS].set(w_grp)
          .at[:, :, N_GROUPS:N_GROUPS + N_EXPERTS].set(w_exp))
    br = (jnp.zeros((depth, 1, LANES), F32).at[:, 0, :N_GROUPS].set(b_grp)
          .at[:, 0, N_GROUPS:N_GROUPS + N_EXPERTS].set(b_exp))
    wr_hi = wr.astype(BF16)
    wr_lo = (wr - wr_hi.astype(F32)).astype(BF16)
    return pl.pallas_call(
        _router_kernel,
        grid=(t // tm,),
        in_specs=[pl.BlockSpec((tm, d), lambda i: (i, 0)),
                  pl.BlockSpec((None, 1, d), lambda i: (layer, 0, 0)),
                  pl.BlockSpec((None, d, LANES), lambda i: (layer, 0, 0)),
                  pl.BlockSpec((None, d, LANES), lambda i: (layer, 0, 0)),
                  pl.BlockSpec((None, 1, LANES), lambda i: (layer, 0, 0))],
        out_specs=[pl.BlockSpec((tm, d // 2), lambda i: (i, 0)),
                   pl.BlockSpec((tm, LANES), lambda i: (i, 0)),
                   pl.BlockSpec((tm, LANES), lambda i: (i, 0)),
                   pl.BlockSpec((SUBLANES, LANES), lambda i: (0, 0))],
        out_shape=[jax.ShapeDtypeStruct((t, d // 2), jnp.uint32),
                   jax.ShapeDtypeStruct((t, LANES), F32),
                   jax.ShapeDtypeStruct((t, LANES), jnp.int32),
                   jax.ShapeDtypeStruct((SUBLANES, LANES), F32)],
        scratch_shapes=[pltpu.VMEM((SUBLANES, LANES), F32)],
        compiler_params=_cparams(("arbitrary",)),
        name="router",
    )(x2d, g.reshape(depth, 1, d), wr_hi, wr_lo, br)


def _dispatch_kernel(pos_ref, zrow_ref, h_ref, xs_ref, zero_ref, sem, zsem):
    i = pl.program_id(0)
    tm = h_ref.shape[0]
    zt = zero_ref.shape[0]

    @pl.when(i == 0)
    def _():
        zero_ref[...] = jnp.zeros_like(zero_ref)

        def zero_copy(e):
            start = pl.multiple_of(jnp.maximum(zrow_ref[e], 0), zt)
            return pltpu.make_async_copy(zero_ref, xs_ref.at[pl.ds(start, zt)], zsem)

        def zissue(e, carry):
            @pl.when(zrow_ref[e] >= 0)
            def _():
                zero_copy(e).start()
            return carry

        def zdrain(e, carry):
            @pl.when(zrow_ref[e] >= 0)
            def _():
                zero_copy(e).wait()
            return carry

        lax.fori_loop(0, zrow_ref.shape[0], zissue, 0)
        lax.fori_loop(0, zrow_ref.shape[0], zdrain, 0)

    def row_copy(r, p):
        return pltpu.make_async_copy(h_ref.at[pl.ds(r, 1)], xs_ref.at[pl.ds(p, 1)], sem)

    def issue(r, carry):
        t = i * tm + r
        for k in range(2):
            row_copy(r, pos_ref[2 * t + k]).start()
        return carry

    lax.fori_loop(0, tm, issue, 0, unroll=8)
    for k in range(2):
        pltpu.make_async_copy(h_ref, xs_ref.at[pl.ds(0, tm)], sem).wait()


def _dispatch(pos, zrow, h, n_rows, tm=ROW_TILE):
    t, d = h.shape
    tm = min(tm, t)
    return pl.pallas_call(
        _dispatch_kernel,
        grid_spec=pltpu.PrefetchScalarGridSpec(
            num_scalar_prefetch=2,
            grid=(t // tm,),
            in_specs=[pl.BlockSpec((tm, d), lambda i, pos, zrow: (i, 0))],
            out_specs=pl.BlockSpec(memory_space=pl.ANY),
            scratch_shapes=[pltpu.VMEM((EXPERT_TILE, d), h.dtype), pltpu.SemaphoreType.DMA(()),
                            pltpu.SemaphoreType.DMA(())],
        ),
        out_shape=jax.ShapeDtypeStruct((n_rows, d), h.dtype),
        compiler_params=_cparams(("arbitrary",)),
        name="dispatch",
    )(pos, zrow, h)


def _combine_kernel(keep_x, pos_ref, x_ref, info_ref, g_ref, ys_ref, *rest):
    outs, (buf_ref, sem) = rest[:-2], rest[-2:]
    i = pl.program_id(0)
    n = pl.num_programs(0)
    tm = x_ref.shape[0]
    half = x_ref.shape[1] // 2

    def row_copy(slot, r, k, p):
        return pltpu.make_async_copy(ys_ref.at[pl.ds(p, 1)], buf_ref.at[slot, k, pl.ds(r, 1)],
                                     sem.at[slot])

    def gather_tile(tile, slot):
        def issue(r, carry):
            t = tile * tm + r
            for k in range(2):
                row_copy(slot, r, k, pos_ref[2 * t + k]).start()
            return carry
        lax.fori_loop(0, tm, issue, 0, unroll=8)

    @pl.when(i == 0)
    def _():
        gather_tile(0, 0)

    @pl.when(i + 1 < n)
    def _():
        gather_tile(i + 1, (i + 1) % 2)

    slot = i % 2

    for k in range(2):
        pltpu.make_async_copy(ys_ref.at[pl.ds(0, tm)], buf_ref.at[slot, k], sem.at[slot]).wait()
    info = info_ref[...]
    w0, w1 = info[:, 0:1], info[:, 1:2]
    lo0, hi0 = _unpack_pairs(buf_ref[slot, 0])
    lo1, hi1 = _unpack_pairs(buf_ref[slot, 1])
    x = x_ref[...]
    x_lo = x[:, :half] + w0 * lo0 + w1 * lo1
    x_hi = x[:, half:] + w0 * hi0 + w1 * hi1
    ms = (jnp.sum(x_lo * x_lo, axis=-1, keepdims=True)
          + jnp.sum(x_hi * x_hi, axis=-1, keepdims=True)) / x.shape[1]
    inv = lax.rsqrt(ms + NORM_EPS)
    g = g_ref[...]
    if keep_x:
        outs[0][:, :half] = x_lo
        outs[0][:, half:] = x_hi
    outs[-1][:, :half] = (x_lo * inv * g[:, :half]).astype(outs[-1].dtype)
    outs[-1][:, half:] = (x_hi * inv * g[:, half:]).astype(outs[-1].dtype)


def _combine(pos, x2d, info, ys, g, keep_x, norm_dtype, tm=ROW_TILE):
    t, d = x2d.shape
    tm = min(tm, t)
    row_spec = pl.BlockSpec((tm, d), lambda i, pos: (i, 0))
    out_specs = [row_spec] * (2 if keep_x else 1)
    out_shape = ([jax.ShapeDtypeStruct((t, d), F32)] if keep_x else []) + [jax.ShapeDtypeStruct((t, d), norm_dtype)]
    return pl.pallas_call(
        functools.partial(_combine_kernel, keep_x),
        grid_spec=pltpu.PrefetchScalarGridSpec(
            num_scalar_prefetch=1,
            grid=(t // tm,),
            in_specs=[row_spec,
                      pl.BlockSpec((tm, LANES), lambda i, pos: (i, 0)),
                      pl.BlockSpec((1, d), lambda i, pos: (0, 0)),
                      pl.BlockSpec(memory_space=pl.ANY)],
            out_specs=out_specs,
            scratch_shapes=[pltpu.VMEM((2, 2, tm, d // 2), ys.dtype),
                            pltpu.SemaphoreType.DMA((2,))],
        ),
        out_shape=out_shape,
        compiler_params=_cparams(("arbitrary",)),
        name="combine",
    )(pos, x2d, info, g.reshape(1, d), ys)


def _new_expert(te_ref, i):
    return (i == 0) | (te_ref[i] != te_ref[jnp.maximum(i - 1, 0)])


def _expert_kernel(te_ref, nu_ref, nx_ref, xs_ref, wg_hbm, wu_hbm, wd_hbm, ys_ref,
                   fg_ref, fu_ref, fd_ref, bg_ref, bu_ref, bd_ref, sem):
    i = pl.program_id(0)

    def fetch(e):
        return (pltpu.make_async_copy(wg_hbm.at[e], fg_ref, sem.at[0]),
                pltpu.make_async_copy(wu_hbm.at[e], fu_ref, sem.at[1]),
                pltpu.make_async_copy(wd_hbm.at[e], fd_ref, sem.at[2]))

    @pl.when(i == 0)
    def _():
        for copy in fetch(te_ref[0]):
            copy.start()

    @pl.when(_new_expert(te_ref, i))
    def _():
        for copy in fetch(te_ref[i]):
            copy.wait()
        bg_ref[...] = fg_ref[...].astype(BF16)
        bu_ref[...] = fu_ref[...].astype(BF16)
        bd_ref[...] = fd_ref[...].astype(BF16)

        @pl.when(nx_ref[i] >= 0)
        def _():
            for copy in fetch(nx_ref[i]):
                copy.start()

    @pl.when(i < nu_ref[0])
    def _():
        lo, hi = _unpack_pairs(xs_ref[...])
        x = jnp.concatenate([lo.astype(BF16), hi.astype(BF16)], axis=1)
        a = _dot(x, bg_ref[...])
        u = _dot(x, bu_ref[...])
        act = (_silu(a) * u).astype(BF16)
        ys_ref[...] = _pack_pairs(_dot(act, bd_ref[...]))

    @pl.when(i >= nu_ref[0])
    def _():
        ys_ref[...] = jnp.zeros_like(ys_ref)


def _experts(tile_expert, n_used, next_expert, xs, wg, wu, wd):
    p, dh = xs.shape
    d, f = wg.shape[1:]
    tm = EXPERT_TILE
    src_tile = lambda i, te, nu, nx: (jnp.maximum(jnp.minimum(i, nu[0] - 1), 0), 0)
    hbm = pl.BlockSpec(memory_space=pl.ANY)
    return pl.pallas_call(
        _expert_kernel,
        grid_spec=pltpu.PrefetchScalarGridSpec(
            num_scalar_prefetch=3,
            grid=(p // tm,),
            in_specs=[pl.BlockSpec((tm, dh), src_tile), hbm, hbm, hbm],
            out_specs=pl.BlockSpec((tm, dh), lambda i, te, nu, nx: (i, 0)),
            scratch_shapes=[pltpu.VMEM((d, f), F32), pltpu.VMEM((d, f), F32), pltpu.VMEM((f, d), F32),
                            pltpu.VMEM((d, f), BF16), pltpu.VMEM((d, f), BF16), pltpu.VMEM((f, d), BF16),
                            pltpu.SemaphoreType.DMA((3,))],
        ),
        out_shape=jax.ShapeDtypeStruct((p, dh), xs.dtype),
        compiler_params=_cparams(("arbitrary",)),
        name="experts",
    )(tile_expert, n_used, next_expert, xs, wg, wu, wd)


def _hier_moe(x2d, layer, ffn_g, w_grp, b_grp, w_exp, b_exp, wg, wu, wd, next_g, keep_x, norm_dtype):
    t, d = x2d.shape
    tm = EXPERT_TILE
    n_rows = 2 * t + N_EXPERTS * tm
    h, info, idx, cnt = _router(x2d, layer, ffn_g, w_grp, b_grp, w_exp, b_exp)
    counts = cnt[0, :N_EXPERTS].astype(jnp.int32)
    padded = ((counts + tm - 1) // tm) * tm
    ends = jnp.cumsum(padded)
    offs = ends - padded
    pos = (offs[idx[:, 0:2]] + idx[:, 2:4]).reshape(2 * t)
    n_used = (ends[-1] // tm).reshape(1).astype(jnp.int32)
    tail = n_used[0] + jnp.arange(N_EXPERTS, dtype=jnp.int32)
    zrow = jnp.concatenate([jnp.where(padded > 0, ends - tm, -1),
                            jnp.where(tail < n_rows // tm, tail * tm, -1)]).astype(jnp.int32)
    tiles = jnp.minimum(jnp.arange(n_rows // tm, dtype=jnp.int32), n_used[0] - 1)
    tile_expert = jnp.sum(((ends // tm)[None, :] <= tiles[:, None]).astype(jnp.int32), axis=1)
    tile_expert = jnp.minimum(tile_expert, N_EXPERTS - 1)
    ids = jnp.arange(N_EXPERTS, dtype=jnp.int32)
    later = (ids[None, :] > ids[:, None]) & (padded[None, :] > 0)
    following = jnp.min(jnp.where(later, ids[None, :], N_EXPERTS), axis=1)
    following = jnp.where(following < N_EXPERTS, following + layer * N_EXPERTS, -1).astype(jnp.int32)
    next_expert = following[tile_expert]
    tile_expert = tile_expert + layer * N_EXPERTS
    xs = _dispatch(pos, zrow, h, n_rows)
    ys = _experts(tile_expert, n_used, next_expert, xs, wg, wu, wd)
    return _combine(pos, x2d, info, ys, next_g, keep_x, norm_dtype)


def _repack_kernel(w_ref, big_ref, small_ref):
    c0 = 5 * BW
    c1 = c0 + GLA_RANK
    c2 = c1 + 4 * BW
    c3 = c2 + 2 * GDN_HEADS
    c4 = c3 + 3 * BW
    c5 = c4 + 2 * MLSTM_HEADS
    step = 512
    for dst, src, n in ((0, 0, c0), (c0, c1, c2 - c1), (c0 + c2 - c1, c3, c4 - c3)):
        for o in range(0, n, step):
            big_ref[:, dst + o:dst + o + step] = w_ref[src + o:src + o + step, :].T.astype(BF16)
    n_small = GLA_RANK + 2 * GDN_HEADS + 2 * MLSTM_HEADS
    small = jnp.concatenate([w_ref[c0:c1, :], w_ref[c2:c3, :], w_ref[c4:c5, :],
                             jnp.zeros((LANES - n_small, w_ref.shape[1]), F32)], axis=0)
    small_ref[...] = small.T.astype(BF16)


def _split_w_in(w, tk=256):
    depth, k, n = w.shape
    w = jnp.swapaxes(w, 1, 2)
    return pl.pallas_call(
        _repack_kernel,
        grid=(depth, k // tk),
        in_specs=[pl.BlockSpec((None, n, tk), lambda l, i: (l, 0, i))],
        out_specs=[pl.BlockSpec((None, tk, PROJ_BIG), lambda l, i: (l, i, 0)),
                   pl.BlockSpec((None, tk, LANES), lambda l, i: (l, i, 0))],
        out_shape=[jax.ShapeDtypeStruct((depth, k, PROJ_BIG), BF16),
                   jax.ShapeDtypeStruct((depth, k, LANES), BF16)],
        compiler_params=_cparams(("parallel", "parallel")),
        name="repack_w_in",
    )(w)


def kernel(x, mix_norm, w_in, lru_conv_w, lru_conv_b, lru_w_a, lru_b_a, lru_w_x, lru_b_x, lru_lambda,
           gla_w_decay, gla_b_decay, gla_norm, gdn_conv_w, gdn_a_log, gdn_dt_bias, gdn_norm,
           mlstm_b_i, mlstm_b_f, mlstm_norm, w_branch, w_merge_gate, b_merge_gate, w_out,
           ffn_norm, w_group_router, b_group_router, w_expert_router, b_expert_router,
           w_exp_gate, w_exp_up, w_exp_down, final_norm):
    batch, seq, d = x.shape
    depth = w_in.shape[0]
    x2d = x.reshape(batch * seq, d)
    w_big, w_small = _split_w_in(w_in)
    wg_merge = w_merge_gate.astype(BF16)
    wb_merge = w_branch.astype(BF16)
    w_o = w_out.astype(BF16)
    f = w_exp_gate.shape[-1]
    we_gate = w_exp_gate.reshape(depth * N_EXPERTS, d, f)
    we_up = w_exp_up.reshape(depth * N_EXPERTS, d, f)
    we_down = w_exp_down.reshape(depth * N_EXPERTS, f, d)
    xn = _rmsnorm(x2d, mix_norm[0], BF16)
    for l in range(depth):
        last = l == depth - 1
        proj = _matmul(xn, w_big, l, tm=1024, tn=1024, name="in_proj")
        small = _matmul(xn, w_small, l, name="in_proj_small")
        y_a = _lru(proj, batch, seq, l, lru_conv_w, lru_conv_b, lru_w_a, lru_b_a, lru_w_x, lru_b_x,
                   lru_lambda)
        y_b = _gla(proj, small, batch, seq, l, gla_w_decay, gla_b_decay, gla_norm)
        y_c = _gdn(proj, small, batch, seq, l, gdn_conv_w, gdn_a_log, gdn_dt_bias, gdn_norm)
        y_d = _mlstm(proj, small, batch, seq, l, mlstm_b_i, mlstm_b_f, mlstm_norm)
        merged = _merge(xn, (y_a, y_b, y_c, y_d), l, wg_merge, b_merge_gate, wb_merge)
        x2d = _matmul(merged, w_o, l, residual=x2d, tm=1024, name="out_proj")
        outs = _hier_moe(x2d, l, ffn_norm, w_group_router, b_group_router, w_expert_router,
                         b_expert_router, we_gate, we_up, we_down,
                         final_norm if last else mix_norm[l + 1], not last, F32 if last else BF16)
        if last:
            (y,) = outs
        else:
            x2d, xn = outs
    return y.reshape(batch, seq, d)
```
